```python
import math
import jax, jax.numpy as jnp
from jax import lax
import numpy as np

D_MODEL = 1024
BATCH = 16
SEQ = 2048
DEPTH = 2

PLE_DIM = 256
NORM_EPS = 1e-6

SSD_HEADS = 8
SSD_HEAD_DIM = 64
SSD_WIDTH = SSD_HEADS * SSD_HEAD_DIM
SSD_GROUPS = 2
SSD_STATE = 128
SSD_CONV = 4
SSD_CHUNK = 128
SSD_CONV_CH = SSD_WIDTH + 2 * SSD_GROUPS * SSD_STATE

DIFF_HEADS = 4
DIFF_QK_DIM = 32
DIFF_V_DIM = 2 * DIFF_QK_DIM
DIFF_QK_WIDTH = DIFF_HEADS * 2 * DIFF_QK_DIM
DIFF_WIDTH = DIFF_HEADS * DIFF_V_DIM
Q_BLOCK = 128
REL_BUCKETS = 32
REL_MAX_DIST = 128

MLSTM_HEADS = 4
MLSTM_HEAD_DIM = 64
MLSTM_WIDTH = MLSTM_HEADS * MLSTM_HEAD_DIM
MLSTM_CHUNK = 128

MIX_WIDTH = SSD_WIDTH + DIFF_WIDTH + MLSTM_WIDTH
FFN_HIDDEN = 256 * (-(-(8 * D_MODEL) // (3 * 256)))

IN_SPLITS = (SSD_WIDTH, SSD_CONV_CH, SSD_HEADS,
             DIFF_QK_WIDTH, DIFF_QK_WIDTH, DIFF_WIDTH,
             MLSTM_WIDTH, MLSTM_WIDTH, MLSTM_WIDTH, MLSTM_WIDTH, MLSTM_HEADS, MLSTM_HEADS)
IN_COLS = sum(IN_SPLITS)

kernel_name = "hybrid_ssd_diffattn_mlstm_trunk"


def rmsnorm(x, w):
    xf = x.astype(jnp.float32)
    var = jnp.mean(xf * xf, axis=-1, keepdims=True)
    return (xf * lax.rsqrt(var + NORM_EPS)).astype(x.dtype) * w


def split_cols(t, sizes):
    offs = []
    acc = 0
    for s in sizes[:-1]:
        acc += s
        offs.append(acc)
    return jnp.split(t, offs, axis=-1)


def causal_depthwise_conv(x, w, b):
    k = w.shape[0]
    c = x.shape[-1]
    y = lax.conv_general_dilated(x, w[:, None, :], window_strides=(1,), padding=[(k - 1, 0)],
                                 dimension_numbers=('NWC', 'WIO', 'NWC'), feature_group_count=c)
    return y + b


def segsum_exp(a):
    n = a.shape[-1]
    cs = jnp.cumsum(a, axis=-1)
    diff = cs[..., :, None] - cs[..., None, :]
    mask = jnp.tril(jnp.ones((n, n), dtype=bool))
    return jnp.where(mask, jnp.exp(jnp.where(mask, diff, 0.0)), 0.0)


def ssd_mixer(z, xbc, dt_raw, conv_w, conv_b, dt_bias, a_log, d_skip, norm_w):
    b_, s_, _ = xbc.shape
    dtp = xbc.dtype
    G, HG, P, N, L = SSD_GROUPS, SSD_HEADS // SSD_GROUPS, SSD_HEAD_DIM, SSD_STATE, SSD_CHUNK
    nc = s_ // L
    xbc = jax.nn.silu(causal_depthwise_conv(xbc, conv_w, conv_b))
    xs, bm, cm = jnp.split(xbc, [SSD_WIDTH, SSD_WIDTH + G * N], axis=-1)
    dt = jax.nn.softplus(dt_raw.astype(jnp.float32) + dt_bias.astype(jnp.float32))
    a = -jnp.exp(a_log.astype(jnp.float32))
    a_dt = dt * a
    x = xs.reshape(b_, nc, L, G, HG, P)
    xdt = x * dt.reshape(b_, nc, L, G, HG)[..., None].astype(dtp)
    bmc = bm.reshape(b_, nc, L, G, N)
    cmc = cm.reshape(b_, nc, L, G, N)
    a_c = a_dt.reshape(b_, nc, L, G, HG).transpose(0, 3, 4, 1, 2)
    a_cs = jnp.cumsum(a_c, axis=-1)
    decay = segsum_exp(a_c)
    cb = jnp.einsum('bclgn,bcsgn->bgcls', cmc, bmc)
    y_diag = jnp.einsum('bghcls,bcsghp->bclghp', cb[:, :, None] * decay.astype(dtp), xdt)
    dstate = jnp.exp(a_cs[..., -1:] - a_cs).astype(dtp)
    states = jnp.einsum('bclgn,bghcl,bclghp->bcghpn', bmc, dstate, xdt)
    chunk_decay = jnp.exp(a_cs[..., -1]).astype(dtp)

    def step(carry, inp):
        st, dec = inp
        return carry * dec[..., None, None] + st, carry

    _, prev = lax.scan(step, jnp.zeros_like(states[:, 0]),
                       (jnp.moveaxis(states, 1, 0), jnp.moveaxis(chunk_decay, 3, 0)))
    prev = jnp.moveaxis(prev, 0, 1)
    y_off = jnp.einsum('bclgn,bcghpn,bghcl->bclghp', cmc, prev, jnp.exp(a_cs).astype(dtp))
    y = (y_diag + y_off).reshape(b_, s_, G, HG, P) + x.reshape(b_, s_, G, HG, P) * d_skip.reshape(G, HG)[:, :, None]
    y = y.reshape(b_, s_, SSD_WIDTH)
    return rmsnorm(y * jax.nn.silu(z), norm_w)


def t5_causal_bucket(dist):
    max_exact = REL_BUCKETS // 2
    is_small = dist < max_exact
    d = jnp.maximum(dist, 1).astype(jnp.float32)
    large = max_exact + (jnp.log(d / max_exact) / math.log(REL_MAX_DIST / max_exact)
                         * (REL_BUCKETS - max_exact)).astype(jnp.int32)
    large = jnp.minimum(large, REL_BUCKETS - 1)
    return jnp.where(is_small, dist, large)


def diff_attention(q, k, v, lam, lam_init, rel_bias, norm_w):
    b_, s_, h_, _, d = q.shape
    scale = d ** -0.5
    outs = []
    for j in range(s_ // Q_BLOCK):
        q0 = j * Q_BLOCK
        kl = q0 + Q_BLOCK
        qb = q[:, q0:kl]
        kb = k[:, :kl]
        vb = v[:, :kl]
        sc = jnp.einsum('bqhmd,bkhmd->bhmqk', qb, kb).astype(jnp.float32) * scale
        dist = (q0 + jnp.arange(Q_BLOCK, dtype=jnp.int32))[:, None] - jnp.arange(kl, dtype=jnp.int32)[None, :]
        bias = rel_bias[t5_causal_bucket(jnp.maximum(dist, 0))]
        sc = sc + jnp.transpose(bias, (2, 0, 1))[None, :, None].astype(jnp.float32)
        sc = jnp.where(dist >= 0, sc, -1e30)
        a = jax.nn.softmax(sc, axis=-1)
        attn = a[:, :, 0] - lam * a[:, :, 1]
        outs.append(jnp.einsum('bhqk,bkhe->bqhe', attn.astype(v.dtype), vb))
    o = jnp.concatenate(outs, axis=1)
    o = rmsnorm(o, norm_w) * (1.0 - lam_init)
    return o.reshape(b_, s_, h_ * 2 * d)


def mlstm_mixer(q, k, v, o_raw, i_raw, f_raw, i_bias, f_bias, norm_w):
    b_, s_, _ = q.shape
    dtp = q.dtype
    f32 = jnp.float32
    H, d, L = MLSTM_HEADS, MLSTM_HEAD_DIM, MLSTM_CHUNK
    nc = s_ // L
    qc = q.reshape(b_, nc, L, H, d)
    kc = k.reshape(b_, nc, L, H, d) * (d ** -0.5)
    vc = v.reshape(b_, nc, L, H, d)
    log_i = (i_raw.astype(f32) + i_bias.astype(f32)).reshape(b_, nc, L, H).transpose(0, 3, 1, 2)
    log_f = jax.nn.log_sigmoid(f_raw.astype(f32) + f_bias.astype(f32)).reshape(b_, nc, L, H).transpose(0, 3, 1, 2)
    bcum = jnp.cumsum(log_f, axis=-1)
    mask = jnp.tril(jnp.ones((L, L), dtype=bool))
    log_d = jnp.where(mask, bcum[..., :, None] - bcum[..., None, :] + log_i[..., None, :], -jnp.inf)
    m_intra = jnp.max(log_d, axis=-1)
    g = bcum[..., -1]
    w_end = g[..., None] - bcum + log_i
    m_loc = jnp.max(w_end, axis=-1)
    e_end = jnp.exp(w_end - m_loc[..., None]).astype(dtp)
    c_loc = jnp.einsum('bhcs,bcshe,bcshd->bched', e_end, vc, kc)
    n_loc = jnp.einsum('bhcs,bcshd->bchd', e_end, kc)

    def step(carry, inp):
        cm, nm, mm = carry
        cl, nl, ml, gl = inp
        m_new = jnp.maximum(gl + mm, ml)
        a_prev = jnp.exp(gl + mm - m_new).astype(cm.dtype)
        a_loc = jnp.exp(ml - m_new).astype(cm.dtype)
        c_new = cm * a_prev[..., None, None] + cl * a_loc[..., None, None]
        n_new = nm * a_prev[..., None] + nl * a_loc[..., None]
        return (c_new, n_new, m_new), (cm, nm, mm)

    init = (jnp.zeros_like(c_loc[:, 0]), jnp.zeros_like(n_loc[:, 0]), jnp.zeros((b_, H), f32))
    _, (c_prev, n_prev, m_prev) = lax.scan(
        step, init, (jnp.moveaxis(c_loc, 1, 0), jnp.moveaxis(n_loc, 1, 0),
                     jnp.moveaxis(m_loc, 2, 0), jnp.moveaxis(g, 2, 0)))
    c_prev = jnp.moveaxis(c_prev, 0, 1)
    n_prev = jnp.moveaxis(n_prev, 0, 1)
    m_prev = jnp.moveaxis(m_prev, 0, 2)
    inter_log = bcum + m_prev[..., None]
    m_t = jnp.maximum(inter_log, m_intra)
    w_inter = jnp.exp(inter_log - m_t)
    w_intra = jnp.exp(log_d - m_t[..., None])
    a_mat = jnp.einsum('bclhd,bcshd->bhcls', qc, kc) * w_intra.astype(dtp)
    wi = w_inter.transpose(0, 2, 3, 1)
    num = (jnp.einsum('bhcls,bcshe->bclhe', a_mat, vc)
           + wi[..., None].astype(dtp) * jnp.einsum('bclhd,bched->bclhe', qc, c_prev))
    den = jnp.sum(a_mat.astype(f32), axis=-1) + w_inter * jnp.einsum('bclhd,bchd->bhcl', qc, n_prev).astype(f32)
    denom = jnp.maximum(jnp.abs(den), jnp.exp(-m_t))
    h_t = (num / denom.transpose(0, 2, 3, 1)[..., None].astype(dtp)).reshape(b_, s_, H, d)
    h_t = rmsnorm(h_t, norm_w.reshape(H, d))
    out = jax.nn.sigmoid(o_raw).reshape(b_, s_, H, d) * h_t
    return out.reshape(b_, s_, MLSTM_WIDTH)


def setup_inputs(seed: int = 0) -> dict:
    key = jax.random.key(seed)
    ks = jax.random.split(key, 32)
    f32 = jnp.float32

    def nrm(k, shape, scale):
        return jax.random.normal(k, shape, f32) * scale

    x = nrm(ks[0], (BATCH, SEQ, D_MODEL), 1.0)
    p = nrm(ks[1], (DEPTH, BATCH, SEQ, PLE_DIM), 1.0)
    norm1_w = 1.0 + nrm(ks[2], (DEPTH, D_MODEL), 0.02)
    w_in = nrm(ks[3], (DEPTH, D_MODEL, IN_COLS), D_MODEL ** -0.5)
    ssd_conv_w = nrm(ks[4], (DEPTH, SSD_CONV, SSD_CONV_CH), SSD_CONV ** -0.5)
    ssd_conv_b = nrm(ks[5], (DEPTH, SSD_CONV_CH), 0.02)
    dt0 = jnp.exp(jax.random.uniform(ks[6], (DEPTH, SSD_HEADS), f32, math.log(1e-3), math.log(1e-1)))
    ssd_dt_bias = dt0 + jnp.log(-jnp.expm1(-dt0))
    ssd_a_log = jnp.log(jax.random.uniform(ks[7], (DEPTH, SSD_HEADS), f32, 1.0, 16.0))
    ssd_d = 1.0 + nrm(ks[8], (DEPTH, SSD_HEADS), 0.02)
    ssd_norm_w = 1.0 + nrm(ks[9], (DEPTH, SSD_WIDTH), 0.02)
    diff_lq1 = nrm(ks[10], (DEPTH, DIFF_QK_DIM), 0.1)
    diff_lk1 = nrm(ks[11], (DEPTH, DIFF_QK_DIM), 0.1)
    diff_lq2 = nrm(ks[12], (DEPTH, DIFF_QK_DIM), 0.1)
    diff_lk2 = nrm(ks[13], (DEPTH, DIFF_QK_DIM), 0.1)
    diff_norm_w = 1.0 + nrm(ks[14], (DEPTH, DIFF_V_DIM), 0.02)
    rel_bias = nrm(ks[15], (REL_BUCKETS, DIFF_HEADS), 0.5)
    mlstm_i_bias = nrm(ks[16], (DEPTH, MLSTM_HEADS), 0.1)
    mlstm_f_bias = jnp.linspace(3.0, 6.0, MLSTM_HEADS, dtype=f32)[None, :] + nrm(ks[17], (DEPTH, MLSTM_HEADS), 0.1)
    mlstm_norm_w = 1.0 + nrm(ks[18], (DEPTH, MLSTM_WIDTH), 0.02)
    w_out = nrm(ks[19], (DEPTH, MIX_WIDTH, D_MODEL), MIX_WIDTH ** -0.5)
    norm2_w = 1.0 + nrm(ks[20], (DEPTH, D_MODEL), 0.02)
    w_ffn_gate = nrm(ks[21], (DEPTH, D_MODEL, FFN_HIDDEN), D_MODEL ** -0.5)
    w_ffn_up = nrm(ks[22], (DEPTH, D_MODEL, FFN_HIDDEN), D_MODEL ** -0.5)
    w_ffn_down = nrm(ks[23], (DEPTH, FFN_HIDDEN, D_MODEL), FFN_HIDDEN ** -0.5)
    ple_gate_w = nrm(ks[24], (DEPTH, D_MODEL, D_MODEL), D_MODEL ** -0.5)
    ple_proj_w = nrm(ks[25], (DEPTH, PLE_DIM, D_MODEL), PLE_DIM ** -0.5)
    final_norm_w = 1.0 + nrm(ks[26], (D_MODEL,), 0.02)
    return {"x": x, "p": p, "norm1_w": norm1_w, "w_in": w_in,
            "ssd_conv_w": ssd_conv_w, "ssd_conv_b": ssd_conv_b, "ssd_dt_bias": ssd_dt_bias,
            "ssd_a_log": ssd_a_log, "ssd_d": ssd_d, "ssd_norm_w": ssd_norm_w,
            "diff_lq1": diff_lq1, "diff_lk1": diff_lk1, "diff_lq2": diff_lq2, "diff_lk2": diff_lk2,
            "diff_norm_w": diff_norm_w, "rel_bias": rel_bias,
            "mlstm_i_bias": mlstm_i_bias, "mlstm_f_bias": mlstm_f_bias, "mlstm_norm_w": mlstm_norm_w,
            "w_out": w_out, "norm2_w": norm2_w, "w_ffn_gate": w_ffn_gate, "w_ffn_up": w_ffn_up,
            "w_ffn_down": w_ffn_down, "ple_gate_w": ple_gate_w, "ple_proj_w": ple_proj_w,
            "final_norm_w": final_norm_w}


def reference(x, p, norm1_w, w_in, ssd_conv_w, ssd_conv_b, ssd_dt_bias, ssd_a_log, ssd_d, ssd_norm_w,
              diff_lq1, diff_lk1, diff_lq2, diff_lk2, diff_norm_w, rel_bias,
              mlstm_i_bias, mlstm_f_bias, mlstm_norm_w, w_out, norm2_w, w_ffn_gate, w_ffn_up,
              w_ffn_down, ple_gate_w, ple_proj_w, final_norm_w):
    b_, s_, _ = x.shape
    f32 = jnp.float32
    h = x
    for i in range(DEPTH):
        u = rmsnorm(h, norm1_w[i])
        proj = u @ w_in[i]
        z, xbc, dt_raw, dq, dk, dv, mq, mk, mv, mo, mi, mf = split_cols(proj, IN_SPLITS)
        y_ssd = ssd_mixer(z, xbc, dt_raw, ssd_conv_w[i], ssd_conv_b[i], ssd_dt_bias[i],
                          ssd_a_log[i], ssd_d[i], ssd_norm_w[i])
        lam_init = 0.8 - 0.6 * math.exp(-0.3 * i)
        lam = (jnp.exp(jnp.sum(diff_lq1[i].astype(f32) * diff_lk1[i].astype(f32)))
               - jnp.exp(jnp.sum(diff_lq2[i].astype(f32) * diff_lk2[i].astype(f32))) + lam_init)
        y_diff = diff_attention(dq.reshape(b_, s_, DIFF_HEADS, 2, DIFF_QK_DIM),
                                dk.reshape(b_, s_, DIFF_HEADS, 2, DIFF_QK_DIM),
                                dv.reshape(b_, s_, DIFF_HEADS, DIFF_V_DIM),
                                lam, lam_init, rel_bias, diff_norm_w[i])
        y_mlstm = mlstm_mixer(mq, mk, mv, mo, mi, mf, mlstm_i_bias[i], mlstm_f_bias[i], mlstm_norm_w[i])
        h = h + jnp.concatenate([y_ssd, y_diff, y_mlstm], axis=-1) @ w_out[i]
        u = rmsnorm(h, norm2_w[i])
        h = h + (jax.nn.silu(u @ w_ffn_gate[i]) * (u @ w_ffn_up[i])) @ w_ffn_down[i]
        h = h + jax.nn.sigmoid(h @ ple_gate_w[i]) * (p[i] @ ple_proj_w[i])
    return rmsnorm(h, final_norm_w)
```

```python
import functools
import math

import numpy as np
import jax
import jax.numpy as jnp
from jax import lax
from jax.experimental import pallas as pl
from jax.experimental.pallas import tpu as pltpu

F32 = jnp.float32
BF16 = jnp.bfloat16

D_MODEL = 1024
PLE_DIM = 256
NORM_EPS = 1e-6
CHUNK = 128

SSD_HEADS = 8
SSD_HEAD_DIM = 64
SSD_WIDTH = 512
SSD_GROUPS = 2
SSD_STATE = 128
SSD_CONV = 4
SSD_CONV_CH = 1024
DIFF_HEADS = 4
DIFF_QK_DIM = 32
DIFF_V_DIM = 64
DIFF_WIDTH = 256
REL_BUCKETS = 32
REL_MAX_DIST = 128
MLSTM_HEADS = 4
MLSTM_HEAD_DIM = 64
MLSTM_WIDTH = 256
MIX_WIDTH = 1024
FFN_HIDDEN = 2816
IN_SPLITS = (512, 1024, 8, 256, 256, 256, 256, 256, 256, 256, 4, 4)

COL_XBC, COL_Z, COL_DQ, COL_DK, COL_DV = 0, 1024, 1536, 1792, 2048
COL_MQ, COL_MK, COL_MV, COL_MO, COL_SMALL = 2304, 2560, 2816, 3072, 3328
PROJ_COLS = 3456
SMALL_DT, SMALL_MI, SMALL_MF = 0, 8, 16

ROW_DT, ROW_CS, ROW_ECS, ROW_DST, ROW_A, ROW_WI, ROW_EN = 0, 8, 16, 24, 32, 40, 48
ROWS_USED = 56

V7X_VMEM_LIMIT = 56 * 1024 * 1024


def _t5_bucket_lower_bounds():
    max_exact = REL_BUCKETS // 2
    d = np.arange(0, 2 * CHUNK, dtype=np.int64)
    df = np.maximum(d, 1).astype(np.float32)
    large = max_exact + (np.log(df / np.float32(max_exact)) / np.float32(math.log(REL_MAX_DIST / max_exact))
                         * np.float32(REL_BUCKETS - max_exact)).astype(np.int32)
    large = np.minimum(large, REL_BUCKETS - 1)
    bucket = np.where(d < max_exact, d, large)
    assert np.all(np.diff(bucket) >= 0) and bucket[CHUNK] == REL_BUCKETS - 1
    return [int(np.argmax(bucket >= b)) for b in range(REL_BUCKETS)]


_BUCKET_LO = _t5_bucket_lower_bounds()


def _sigmoid(x):
    return 1.0 / (1.0 + jnp.exp(-x))


def _softplus(x):
    return jnp.maximum(x, 0.0) + jnp.log1p(jnp.exp(-jnp.abs(x)))


def _dot(a, b):
    return jnp.dot(a, b, preferred_element_type=F32)


def _dot_nt(a, b):
    return lax.dot_general(a, b, (((1,), (1,)), ((), ())), preferred_element_type=F32)


def _scan_lanes(x, combine, identity):
    n = x.shape[-1]
    lane = lax.broadcasted_iota(jnp.int32, x.shape, 1)
    sh = 1
    while sh < n:
        rolled = pltpu.roll(x, sh, 1)
        x = combine(x, jnp.where(lane >= sh, rolled, identity))
        sh *= 2
    return x


def _in_proj_kernel(x_ref, nw_ref, w_ref, o_ref, *, n_chunk):
    x = x_ref[...]
    var = jnp.mean(x * x, axis=-1, keepdims=True)
    u = ((x * lax.rsqrt(var + NORM_EPS)) * nw_ref[...]).astype(BF16)
    for n0 in range(0, PROJ_COLS, n_chunk):
        o_ref[:, n0:n0 + n_chunk] = _dot(u, w_ref[:, n0:n0 + n_chunk])


def _in_proj(h2d, norm_w, w_packed, *, tm):
    t = h2d.shape[0]
    const = lambda i: (0, 0)
    return pl.pallas_call(
        functools.partial(_in_proj_kernel, n_chunk=1152),
        grid=(t // tm,),
        in_specs=[pl.BlockSpec((tm, D_MODEL), lambda i: (i, 0)),
                  pl.BlockSpec((1, D_MODEL), const),
                  pl.BlockSpec((D_MODEL, PROJ_COLS), const, pipeline_mode=pl.Buffered(1))],
        out_specs=pl.BlockSpec((tm, PROJ_COLS), lambda i: (i, 0)),
        out_shape=jax.ShapeDtypeStruct((t, PROJ_COLS), F32),
        compiler_params=pltpu.CompilerParams(dimension_semantics=("arbitrary",),
                                             vmem_limit_bytes=V7X_VMEM_LIMIT),
        name="in_proj",
    )(h2d, norm_w.reshape(1, D_MODEL), w_packed)


def _mixer_kernel(lam_ref, relb_ref,
                  xbc_ref, z_ref, dq_ref, dk_ref, dv_ref, mq_ref, mk_ref, mv_ref, mo_ref, sm_ref,
                  convw_ref, convb_ref, dtb_ref, alog_ref, dfull_ref, ssdnw_ref,
                  dnw_ref, ib_ref, fb_ref, mnw_ref,
                  y_ref,
                  xpad, prev_t, ct, nm, mst, kb, vt, bias,
                  *, lam_init, n_blocks):
    L = CHUNK
    b = pl.program_id(0)
    c = pl.program_id(1)

    lane128 = lax.broadcasted_iota(jnp.int32, (L, 128), 1)
    lo = lane128 < 64
    row_i = lax.broadcasted_iota(jnp.int32, (L, L), 0)
    col_i = lax.broadcasted_iota(jnp.int32, (L, L), 1)
    tril = row_i >= col_i
    row_lo = row_i < 64

    @pl.when(jnp.logical_and(b == 0, c == 0))
    def _():
        for which, off in ((0, 0), (1, L)):
            dist = col_i - row_i + off
            for h in range(DIFF_HEADS):
                far = relb_ref[REL_BUCKETS - 1, h]
                v = jnp.full((L, L), far, F32)
                for bkt in range(REL_BUCKETS - 2, -1, -1):
                    v = jnp.where(dist < _BUCKET_LO[bkt + 1], relb_ref[bkt, h], v)
                bias[which, h] = v - far

    @pl.when(c == 0)
    def _():
        xpad[0:8, :] = jnp.zeros((8, SSD_CONV_CH), F32)
        prev_t[...] = jnp.zeros(prev_t.shape, F32)
        ct[...] = jnp.zeros(ct.shape, F32)
        nm[...] = jnp.zeros(nm.shape, F32)
        mst[...] = jnp.zeros(mst.shape, F32)

        def fill(t, carry):
            r0 = pl.multiple_of(t * L, L)
            kb[pl.ds(r0, L), :] = dk_ref[0, pl.ds(r0, L), :].astype(BF16)
            vt[t] = dv_ref[0, pl.ds(r0, L), :].T.astype(BF16)
            return carry

        lax.fori_loop(0, n_blocks, fill, 0)

    sm_t = sm_ref[0].T
    dt_raw = sm_t[SMALL_DT:SMALL_DT + 8]
    i_raw = sm_t[SMALL_MI:SMALL_MI + 8]
    f_raw = sm_t[SMALL_MF:SMALL_MF + 8]

    dt = _softplus(dt_raw + dtb_ref[...])
    a_dt = dt * (-jnp.exp(alog_ref[...]))
    cs = _scan_lanes(a_dt, jnp.add, 0.0)
    cs_last = jnp.broadcast_to(cs[:, L - 1:L], (8, L))
    ecs = jnp.exp(cs)
    dst = jnp.exp(cs_last - cs)
    cdec = jnp.exp(cs_last)

    log_i = i_raw + ib_ref[...]
    log_f = -_softplus(-(f_raw + fb_ref[...]))
    bc = _scan_lanes(log_f, jnp.add, 0.0)
    u_row = log_i - bc
    m_intra = bc + _scan_lanes(u_row, jnp.maximum, -jnp.inf)
    g_b = jnp.broadcast_to(bc[:, L - 1:L], (8, L))
    w_end = g_b - bc + log_i
    mloc_b = jnp.broadcast_to(jnp.max(w_end, axis=1, keepdims=True), (8, L))
    e_end = jnp.exp(w_end - mloc_b)
    m_prev = mst[...]
    inter_log = bc + m_prev
    m_t = jnp.maximum(inter_log, m_intra)
    w_inter = jnp.exp(inter_log - m_t)
    a_col = bc - m_t
    en = jnp.exp(-m_t)
    m_new = jnp.maximum(g_b + m_prev, mloc_b)
    a_prev = jnp.exp(g_b + m_prev - m_new)
    a_loc = jnp.exp(mloc_b - m_new)

    rows = jnp.concatenate([dt, cs, ecs, dst, a_col, w_inter, en,
                            jnp.zeros((128 - ROWS_USED, L), F32)], axis=0)
    colblk = rows.T

    def col(j):
        return colblk[:, j:j + 1]

    def pair(j0, k):
        return jnp.where(lo, col(j0 + 2 * k), col(j0 + 2 * k + 1))

    def pairs4(j0):
        return jnp.concatenate([pair(j0, k) for k in range(4)], axis=1)

    xbc = xbc_ref[0]
    xpad[8:8 + L, :] = xbc
    conv = convb_ref[...]
    for k in range(SSD_CONV):
        conv = conv + convw_ref[k:k + 1, :] * xpad[5 + k:5 + k + L, :]
    xpad[0:8, :] = xbc[L - 8:L, :]
    xc = conv * _sigmoid(conv)

    xs = xc[:, :SSD_WIDTH]
    bmat = xc[:, SSD_WIDTH:SSD_WIDTH + 256]
    cmat = xc[:, SSD_WIDTH + 256:]
    xdt = xs * pairs4(ROW_DT)
    xds = xdt * pairs4(ROW_DST)
    ecs_full = pairs4(ROW_ECS)
    dfull = dfull_ref[...]

    y_slabs = []
    for g in range(SSD_GROUPS):
        bm_g = bmat[:, 128 * g:128 * (g + 1)]
        cm_b = cmat[:, 128 * g:128 * (g + 1)].astype(BF16)
        cb = _dot_nt(cm_b, bm_g.astype(BF16))
        s_loc = _dot(bm_g.T.astype(BF16), xds[:, 256 * g:256 * (g + 1)].astype(BF16))
        prev = prev_t[g]
        y_off = _dot(cm_b, prev.astype(BF16)) * ecs_full[:, 256 * g:256 * (g + 1)]
        for kk in range(2):
            k = 2 * g + kk
            slab = xdt[:, 128 * k:128 * (k + 1)].astype(BF16)
            halves = []
            for hl in range(2):
                h = 2 * k + hl
                diff = col(ROW_CS + h) - cs[h:h + 1, :]
                dec = jnp.where(tril, jnp.exp(jnp.where(tril, diff, 0.0)), 0.0)
                halves.append(_dot((cb * dec).astype(BF16), slab))
            y_diag = jnp.where(lo, halves[0], halves[1])
            y_slabs.append(y_diag + y_off[:, 128 * kk:128 * (kk + 1)]
                           + xs[:, 128 * k:128 * (k + 1)] * dfull[:, 128 * k:128 * (k + 1)])
        cd = jnp.concatenate(
            [jnp.where(lo[0:1], cdec[2 * k:2 * k + 1, :], cdec[2 * k + 1:2 * k + 2, :])
             for k in (2 * g, 2 * g + 1)], axis=1)
        prev_t[g] = prev * cd + s_loc
    y = jnp.concatenate(y_slabs, axis=1)
    z = z_ref[0]
    yz = y * (z * _sigmoid(z))
    var = jnp.mean(yz * yz, axis=-1, keepdims=True)
    y_ssd = (yz * lax.rsqrt(var + NORM_EPS)) * ssdnw_ref[...]

    scale = DIFF_QK_DIM ** -0.5
    lam = lam_ref[0]
    q = dq_ref[0]
    lane256 = lax.broadcasted_iota(jnp.int32, (L, 256), 1)
    causal_t = jnp.concatenate([col_i >= row_i, col_i >= row_i], axis=1)
    prev_valid = jnp.full((L, 256), c, jnp.int32) >= 1
    c_prev = jnp.maximum(c - 1, 0)
    heads_t = []
    for h in range(DIFF_HEADS):
        q0 = jnp.where((lane256 >= 64 * h) & (lane256 < 64 * h + 32), q, 0.0)
        q1 = jnp.where((lane256 >= 64 * h + 32) & (lane256 < 64 * h + 64), q, 0.0)
        qbd = jnp.concatenate([q0, q1], axis=0).astype(BF16)

        def tile(carry, t, bias_t=None, mask=None, qbd=qbd, h=h):
            m, l, acc = carry
            r0 = pl.multiple_of(t * L, L)
            s = _dot_nt(kb[pl.ds(r0, L), :], qbd) * scale
            if bias_t is not None:
                s = s + jnp.concatenate([bias_t, bias_t], axis=1)
            if mask is not None:
                s = jnp.where(mask, s, -1e30)
            m_n = jnp.maximum(m, jnp.max(s, axis=0, keepdims=True))
            alpha = jnp.exp(m - m_n)
            p = jnp.exp(s - m_n)
            l = alpha * l + jnp.sum(p, axis=0, keepdims=True)
            acc = acc * alpha + _dot(vt[t, 64 * h:64 * (h + 1), :], p.astype(BF16))
            return m_n, l, acc

        carry = (jnp.full((1, 256), -1e30, F32), jnp.zeros((1, 256), F32),
                 jnp.zeros((DIFF_V_DIM, 256), F32))
        carry = lax.fori_loop(0, c_prev, lambda t, cr: tile(cr, t), carry)
        carry = tile(carry, c, bias_t=bias[0, h], mask=causal_t)
        carry = tile(carry, c_prev, bias_t=bias[1, h], mask=prev_valid)
        _, l, acc = carry
        inv = 1.0 / l
        o = acc[:, :L] * inv[:, :L] - lam * (acc[:, L:] * inv[:, L:])
        ovar = jnp.mean(o * o, axis=0, keepdims=True)
        heads_t.append(((o * lax.rsqrt(ovar + NORM_EPS)) * dnw_ref[...]) * (1.0 - lam_init))
    y_diff = jnp.concatenate(heads_t, axis=0).T

    bd_mask = row_lo == (col_i < 64)
    inv_sqrt_d = MLSTM_HEAD_DIM ** -0.5
    mq = mq_ref[0]
    mk = mk_ref[0]
    mv = mv_ref[0]
    mo = mo_ref[0]
    mnw = mnw_ref[...]
    m_slabs = []
    for k in range(2):
        sl = slice(128 * k, 128 * (k + 1))
        qs = mq[:, sl]
        ks = mk[:, sl]
        q_b = qs.astype(BF16)
        k_b = ks.astype(BF16)
        v_b = mv[:, sl].astype(BF16)
        nums, dens = [], []
        for hl in range(2):
            h = 2 * k + hl
            qm = jnp.where(lo if hl == 0 else jnp.logical_not(lo), qs, 0.0).astype(BF16)
            qk = _dot_nt(qm, k_b)
            arg = col(ROW_A + h) + u_row[h:h + 1, :]
            w_intra = jnp.exp(jnp.where(tril, arg, -jnp.inf))
            a_mat = (qk * inv_sqrt_d) * w_intra
            dens.append(jnp.sum(a_mat, axis=1, keepdims=True))
            nums.append(_dot(a_mat.astype(BF16), v_b))
        num_intra = jnp.where(lo, nums[0], nums[1])
        den_intra = jnp.where(lo, dens[0], dens[1])
        ct_prev = ct[k]
        nm_prev = nm[k]
        wi = pair(ROW_WI, k)
        num = num_intra + wi * _dot(q_b, ct_prev.astype(BF16))
        den = den_intra + wi * _dot(q_b, nm_prev.astype(BF16))
        denom = jnp.maximum(jnp.abs(den), pair(ROW_EN, k))
        hs = num / denom
        sq = hs * hs
        s_lo = jnp.sum(jnp.where(lo, sq, 0.0), axis=1, keepdims=True)
        s_hi = jnp.sum(jnp.where(lo, 0.0, sq), axis=1, keepdims=True)
        hvar = jnp.where(lo, s_lo, s_hi) * (1.0 / MLSTM_HEAD_DIM)
        hn = (hs * lax.rsqrt(hvar + NORM_EPS)) * mnw[:, sl]
        m_slabs.append(_sigmoid(mo[:, sl]) * hn)
        esc = jnp.where(row_lo, e_end[2 * k:2 * k + 1, :], e_end[2 * k + 1:2 * k + 2, :])
        ke = (ks.T * inv_sqrt_d) * esc
        ct_loc = _dot(ke.astype(BF16), v_b)
        n_loc = jnp.broadcast_to(jnp.sum(ke, axis=1, keepdims=True), (L, L))
        ap = jnp.where(row_lo, a_prev[2 * k:2 * k + 1, :], a_prev[2 * k + 1:2 * k + 2, :])
        al = jnp.where(row_lo, a_loc[2 * k:2 * k + 1, :], a_loc[2 * k + 1:2 * k + 2, :])
        ct[k] = ct_prev * ap + jnp.where(bd_mask, ct_loc, 0.0) * al
        nm[k] = nm_prev * ap + jnp.where(bd_mask, n_loc, 0.0) * al
    mst[...] = m_new
    y_ml = jnp.concatenate(m_slabs, axis=1)

    y_ref[0] = jnp.concatenate([y_ssd, y_diff, y_ml], axis=1).astype(BF16)


def _mixer(proj, lam, rel_bias, conv_w, conv_b, dt_bias, a_log, d_skip, ssd_norm_w,
           diff_norm_w, i_bias, f_bias, mlstm_norm_w, *, lam_init):
    bsz, s, _ = proj.shape
    L = CHUNK
    nb = s // L

    def tok(width, col0):
        return pl.BlockSpec((1, L, width), lambda b, c: (b, c, col0 // width))

    def seq(width, col0):
        return pl.BlockSpec((1, s, width), lambda b, c: (b, 0, col0 // width))

    def par(shape):
        return pl.BlockSpec(shape, lambda b, c: (0,) * len(shape))

    smem = pl.BlockSpec(memory_space=pltpu.SMEM)
    pad8 = lambda v: jnp.pad(v.astype(F32), (0, 8 - v.shape[0])).reshape(8, 1)
    params = [
        conv_w.astype(F32), conv_b.reshape(1, SSD_CONV_CH).astype(F32),
        dt_bias.reshape(8, 1).astype(F32), a_log.reshape(8, 1).astype(F32),
        jnp.repeat(d_skip.astype(F32), SSD_HEAD_DIM).reshape(1, SSD_WIDTH),
        ssd_norm_w.reshape(1, SSD_WIDTH).astype(F32),
        jnp.broadcast_to(diff_norm_w.astype(F32)[:, None], (DIFF_V_DIM, L)),
        pad8(i_bias), pad8(f_bias), mlstm_norm_w.reshape(1, MLSTM_WIDTH).astype(F32),
    ]
    in_specs = [smem, smem,
                tok(1024, COL_XBC), tok(512, COL_Z), tok(256, COL_DQ),
                seq(256, COL_DK), seq(256, COL_DV),
                tok(256, COL_MQ), tok(256, COL_MK), tok(256, COL_MV), tok(256, COL_MO),
                tok(128, COL_SMALL)] + [par(p.shape) for p in params]
    scratch = [
        pltpu.VMEM((L + 8, SSD_CONV_CH), F32),
        pltpu.VMEM((SSD_GROUPS, SSD_STATE, 256), F32),
        pltpu.VMEM((2, 128, 128), F32),
        pltpu.VMEM((2, 128, 128), F32),
        pltpu.VMEM((8, 128), F32),
        pltpu.VMEM((s, 256), BF16),
        pltpu.VMEM((nb, 256, L), BF16),
        pltpu.VMEM((2, DIFF_HEADS, L, L), F32),
    ]
    return pl.pallas_call(
        functools.partial(_mixer_kernel, lam_init=lam_init, n_blocks=nb),
        grid=(bsz, nb),
        in_specs=in_specs,
        out_specs=pl.BlockSpec((1, L, MIX_WIDTH), lambda b, c: (b, c, 0)),
        out_shape=jax.ShapeDtypeStruct((bsz, s, MIX_WIDTH), BF16),
        scratch_shapes=scratch,
        compiler_params=pltpu.CompilerParams(dimension_semantics=("arbitrary", "arbitrary"),
                                             vmem_limit_bytes=V7X_VMEM_LIMIT),
        name="mixer",
    )(lam.reshape(1).astype(F32), rel_bias.astype(F32),
      proj, proj, proj, proj, proj, proj, proj, proj, proj, proj, *params)


def _dense_kernel(h_ref, y_ref, p_ref, wo_ref, n2_ref, wg_ref, wu_ref, wd_ref, pg_ref, pp_ref, fn_ref,
                  o_ref, *, final, h_chunk):
    h1 = h_ref[...] + _dot(y_ref[...], wo_ref[...])
    var = jnp.mean(h1 * h1, axis=-1, keepdims=True)
    u = ((h1 * lax.rsqrt(var + NORM_EPS)) * n2_ref[...]).astype(BF16)
    ffn = jnp.zeros(h1.shape, F32)
    for c0 in range(0, FFN_HIDDEN, h_chunk):
        g = _dot(u, wg_ref[:, c0:c0 + h_chunk])
        up = _dot(u, wu_ref[:, c0:c0 + h_chunk])
        a = ((g * _sigmoid(g)) * up).astype(BF16)
        ffn = ffn + _dot(a, wd_ref[c0:c0 + h_chunk, :])
    h2 = h1 + ffn
    gate = _sigmoid(_dot(h2.astype(BF16), pg_ref[...]))
    h3 = h2 + gate * _dot(p_ref[...].astype(BF16), pp_ref[...])
    if final:
        fvar = jnp.mean(h3 * h3, axis=-1, keepdims=True)
        h3 = (h3 * lax.rsqrt(fvar + NORM_EPS)) * fn_ref[...]
    o_ref[...] = h3


def _dense(h2d, y2d, p2d, w_out, norm2_w, w_gate, w_up, w_down, ple_gate_w, ple_proj_w, final_norm_w,
           *, final, tm):
    t = h2d.shape[0]
    row = lambda width: pl.BlockSpec((tm, width), lambda i: (i, 0))
    res = lambda a: pl.BlockSpec(a.shape, lambda i: (0, 0), pipeline_mode=pl.Buffered(1))
    weights = [w_out.astype(BF16), norm2_w.reshape(1, D_MODEL).astype(F32), w_gate.astype(BF16),
               w_up.astype(BF16), w_down.astype(BF16), ple_gate_w.astype(BF16), ple_proj_w.astype(BF16),
               final_norm_w.reshape(1, D_MODEL).astype(F32)]
    return pl.pallas_call(
        functools.partial(_dense_kernel, final=final, h_chunk=1408),
        grid=(t // tm,),
        in_specs=[row(D_MODEL), row(MIX_WIDTH), row(PLE_DIM)] + [res(w) for w in weights],
        out_specs=row(D_MODEL),
        out_shape=jax.ShapeDtypeStruct((t, D_MODEL), F32),
        compiler_params=pltpu.CompilerParams(dimension_semantics=("arbitrary",),
                                             vmem_limit_bytes=V7X_VMEM_LIMIT),
        name="dense",
    )(h2d, y2d, p2d, *weights)


def _pack_w_in(w):
    z, xbc, dt, dq, dk, dv, mq, mk, mv, mo, mi, mf = jnp.split(w, np.cumsum(IN_SPLITS)[:-1], axis=1)
    d = w.shape[0]
    zeros = lambda n: jnp.zeros((d, n), w.dtype)
    small = jnp.concatenate([dt, mi, zeros(SMALL_MF - SMALL_MI - 4), mf, zeros(128 - SMALL_MF - 4)], axis=1)
    return jnp.concatenate([xbc, z, dq, dk, dv, mq, mk, mv, mo, small], axis=1).astype(BF16)


def kernel(x, p, norm1_w, w_in, ssd_conv_w, ssd_conv_b, ssd_dt_bias, ssd_a_log, ssd_d, ssd_norm_w, diff_lq1, diff_lk1, diff_lq2, diff_lk2, diff_norm_w, rel_bias, mlstm_i_bias, mlstm_f_bias, mlstm_norm_w, w_out, norm2_w, w_ffn_gate, w_ffn_up, w_ffn_down, ple_gate_w, ple_proj_w, final_norm_w):
    bsz, s, d = x.shape
    depth = w_in.shape[0]
    assert d == D_MODEL and s % CHUNK == 0
    t = bsz * s
    tm = 512 if t % 512 == 0 else CHUNK
    h = x.reshape(t, d).astype(F32)
    for i in range(depth):
        proj = _in_proj(h, norm1_w[i], _pack_w_in(w_in[i]), tm=tm).reshape(bsz, s, PROJ_COLS)
        lam_init = 0.8 - 0.6 * math.exp(-0.3 * i)
        lam = (jnp.exp(jnp.sum(diff_lq1[i].astype(F32) * diff_lk1[i].astype(F32)))
               - jnp.exp(jnp.sum(diff_lq2[i].astype(F32) * diff_lk2[i].astype(F32))) + lam_init)
        y = _mixer(proj, lam, rel_bias, ssd_conv_w[i], ssd_conv_b[i], ssd_dt_bias[i], ssd_a_log[i],
                   ssd_d[i], ssd_norm_w[i], diff_norm_w[i], mlstm_i_bias[i], mlstm_f_bias[i],
                   mlstm_norm_w[i], lam_init=lam_init)
        h = _dense(h, y.reshape(t, MIX_WIDTH), p[i].reshape(t, PLE_DIM).astype(F32), w_out[i], norm2_w[i],
                   w_ffn_gate[i], w_ffn_up[i], w_ffn_down[i], ple_gate_w[i], ple_proj_w[i], final_norm_w,
                   final=(i == depth - 1), tm=tm)
    return h.reshape(bsz, s, d).astype(x.dtype)
```

```python
import functools
import math

import numpy as np
import jax
import jax.numpy as jnp
from jax import lax
from jax.experimental import pallas as pl
from jax.experimental.pallas import tpu as pltpu

F32 = jnp.float32
BF16 = jnp.bfloat16

D_MODEL = 1024
PLE_DIM = 256
NORM_EPS = 1e-6
CHUNK = 128

SSD_HEADS = 8
SSD_HEAD_DIM = 64
SSD_WIDTH = 512
SSD_GROUPS = 2
SSD_STATE = 128
SSD_CONV = 4
SSD_CONV_CH = 1024
DIFF_HEADS = 4
DIFF_QK_DIM = 32
DIFF_V_DIM = 64
DIFF_WIDTH = 256
REL_BUCKETS = 32
REL_MAX_DIST = 128
MLSTM_HEADS = 4
MLSTM_HEAD_DIM = 64
MLSTM_WIDTH = 256
MIX_WIDTH = 1024
FFN_HIDDEN = 2816
IN_SPLITS = (512, 1024, 8, 256, 256, 256, 256, 256, 256, 256, 4, 4)

COL_XBC, COL_Z, COL_DQ, COL_DK, COL_DV = 0, 1024, 1536, 1792, 2048
COL_MQ, COL_MK, COL_MV, COL_MO, COL_SMALL = 2304, 2560, 2816, 3072, 3328
PROJ_COLS = 3456
SMALL_DT, SMALL_MI, SMALL_MF = 0, 8, 16

ROW_DT, ROW_CS, ROW_ECS, ROW_DST, ROW_A, ROW_WI, ROW_EN = 0, 8, 16, 24, 32, 40, 48
ROWS_USED = 56

V7X_VMEM_LIMIT = 56 * 1024 * 1024
MASKED = -1e30


def _t5_bucket_lower_bounds():
    max_exact = REL_BUCKETS // 2
    d = np.arange(0, 2 * CHUNK, dtype=np.int64)
    df = np.maximum(d, 1).astype(np.float32)
    large = max_exact + (np.log(df / np.float32(max_exact)) / np.float32(math.log(REL_MAX_DIST / max_exact))
                         * np.float32(REL_BUCKETS - max_exact)).astype(np.int32)
    large = np.minimum(large, REL_BUCKETS - 1)
    bucket = np.where(d < max_exact, d, large)
    assert np.all(np.diff(bucket) >= 0) and bucket[CHUNK] == REL_BUCKETS - 1
    return [int(np.argmax(bucket >= b)) for b in range(REL_BUCKETS)]


_BUCKET_LO = _t5_bucket_lower_bounds()


def _sigmoid(x):
    return 1.0 / (1.0 + jnp.exp(-x))


def _softplus(x):
    return jnp.maximum(x, 0.0) + jnp.log1p(jnp.exp(-jnp.abs(x)))


def _dot(a, b):
    return jnp.dot(a, b, preferred_element_type=F32)


def _dot_nt(a, b):
    return lax.dot_general(a, b, (((1,), (1,)), ((), ())), preferred_element_type=F32)


def _scan_lanes(x, combine, identity):
    n = x.shape[-1]
    lane = lax.broadcasted_iota(jnp.int32, x.shape, 1)
    sh = 1
    while sh < n:
        rolled = pltpu.roll(x, sh, 1)
        x = combine(x, jnp.where(lane >= sh, rolled, identity))
        sh *= 2
    return x


def _in_proj_kernel(x_ref, nw_ref, w_ref, o_ref, *, n_chunk):
    x = x_ref[...]
    var = jnp.mean(x * x, axis=-1, keepdims=True)
    u = ((x * lax.rsqrt(var + NORM_EPS)) * nw_ref[...]).astype(BF16)
    for n0 in range(0, PROJ_COLS, n_chunk):
        o_ref[:, n0:n0 + n_chunk] = _dot(u, w_ref[:, n0:n0 + n_chunk])


def _in_proj(h2d, norm_w, w_packed, *, tm):
    t = h2d.shape[0]
    const = lambda i: (0, 0)
    return pl.pallas_call(
        functools.partial(_in_proj_kernel, n_chunk=1152),
        grid=(t // tm,),
        in_specs=[pl.BlockSpec((tm, D_MODEL), lambda i: (i, 0)),
                  pl.BlockSpec((1, D_MODEL), const),
                  pl.BlockSpec((D_MODEL, PROJ_COLS), const, pipeline_mode=pl.Buffered(1))],
        out_specs=pl.BlockSpec((tm, PROJ_COLS), lambda i: (i, 0)),
        out_shape=jax.ShapeDtypeStruct((t, PROJ_COLS), F32),
        compiler_params=pltpu.CompilerParams(dimension_semantics=("arbitrary",),
                                             vmem_limit_bytes=V7X_VMEM_LIMIT),
        name="in_proj",
    )(h2d, norm_w.reshape(1, D_MODEL), w_packed)


def _mixer_kernel(lam_ref, relb_ref,
                  xbc_ref, z_ref, dq_ref, dk_ref, dv_ref, mq_ref, mk_ref, mv_ref, mo_ref, sm_ref,
                  convw_ref, convb_ref, dtb_ref, alog_ref, dfull_ref, ssdnw_ref,
                  dnw_ref, ib_ref, fb_ref, mnw_ref,
                  y_ref,
                  xpad, prev_t, ct, nm, mst, kb, vt, bias, qbd, acc_s,
                  *, lam_init, n_blocks):
    L = CHUNK
    b = pl.program_id(0)
    c = pl.program_id(1)

    lane128 = lax.broadcasted_iota(jnp.int32, (L, 128), 1)
    lo = lane128 < 64
    row_i = lax.broadcasted_iota(jnp.int32, (L, L), 0)
    col_i = lax.broadcasted_iota(jnp.int32, (L, L), 1)
    tril = row_i >= col_i
    row_lo = row_i < 64

    @pl.when(jnp.logical_and(b == 0, c == 0))
    def _():
        for h in range(DIFF_HEADS):
            bias[0, h] = jnp.full((L, L), MASKED, F32)
            bias[3, h] = jnp.zeros((L, L), F32)
            far = relb_ref[REL_BUCKETS - 1, h]
            for which, off in ((1, 0), (2, L)):
                dist = col_i - row_i + off
                v = jnp.full((L, L), far, F32)
                for bkt in range(REL_BUCKETS - 2, -1, -1):
                    v = jnp.where(dist < _BUCKET_LO[bkt + 1], relb_ref[bkt, h], v)
                bias[which, h] = jnp.where(dist >= 0, v - far, MASKED)

    @pl.when(c == 0)
    def _():
        xpad[0:8, :] = jnp.zeros((8, SSD_CONV_CH), F32)
        prev_t[...] = jnp.zeros(prev_t.shape, F32)
        ct[...] = jnp.zeros(ct.shape, F32)
        nm[...] = jnp.zeros(nm.shape, F32)
        mst[...] = jnp.zeros(mst.shape, F32)

        def fill(g, carry):
            r0 = pl.multiple_of(g * (2 * L), 2 * L)
            kb[pl.ds(r0, 2 * L), :] = dk_ref[0, pl.ds(r0, 2 * L), :].astype(BF16)
            vt[g, :, 0:L] = dv_ref[0, pl.ds(r0, L), :].T.astype(BF16)
            vt[g, :, L:2 * L] = dv_ref[0, pl.ds(r0 + L, L), :].T.astype(BF16)
            return carry

        lax.fori_loop(0, n_blocks // 2, fill, 0)

    sm_t = sm_ref[0].T
    dt_raw = sm_t[SMALL_DT:SMALL_DT + 8]
    i_raw = sm_t[SMALL_MI:SMALL_MI + 8]
    f_raw = sm_t[SMALL_MF:SMALL_MF + 8]

    dt = _softplus(dt_raw + dtb_ref[...])
    a_dt = dt * (-jnp.exp(alog_ref[...]))
    cs = _scan_lanes(a_dt, jnp.add, 0.0)
    cs_last = jnp.broadcast_to(cs[:, L - 1:L], (8, L))
    ecs = jnp.exp(cs)
    dst = jnp.exp(cs_last - cs)
    cdec = jnp.exp(cs_last)

    log_i = i_raw + ib_ref[...]
    log_f = -_softplus(-(f_raw + fb_ref[...]))
    bc = _scan_lanes(log_f, jnp.add, 0.0)
    u_row = log_i - bc
    m_intra = bc + _scan_lanes(u_row, jnp.maximum, -jnp.inf)
    g_b = jnp.broadcast_to(bc[:, L - 1:L], (8, L))
    w_end = g_b - bc + log_i
    mloc_b = jnp.broadcast_to(jnp.max(w_end, axis=1, keepdims=True), (8, L))
    e_end = jnp.exp(w_end - mloc_b)
    m_prev = mst[...]
    inter_log = bc + m_prev
    m_t = jnp.maximum(inter_log, m_intra)
    w_inter = jnp.exp(inter_log - m_t)
    a_col = bc - m_t
    en = jnp.exp(-m_t)
    m_new = jnp.maximum(g_b + m_prev, mloc_b)
    a_prev = jnp.exp(g_b + m_prev - m_new)
    a_loc = jnp.exp(mloc_b - m_new)

    rows = jnp.concatenate([dt, cs, ecs, dst, a_col, w_inter, en,
                            jnp.zeros((128 - ROWS_USED, L), F32)], axis=0)
    colblk = rows.T

    def col(j):
        return colblk[:, j:j + 1]

    def pair(j0, k):
        return jnp.where(lo, col(j0 + 2 * k), col(j0 + 2 * k + 1))

    def pairs4(j0):
        return jnp.concatenate([pair(j0, k) for k in range(4)], axis=1)

    xbc = xbc_ref[0]
    xpad[8:8 + L, :] = xbc
    conv = convb_ref[...]
    for k in range(SSD_CONV):
        conv = conv + convw_ref[k:k + 1, :] * xpad[5 + k:5 + k + L, :]
    xpad[0:8, :] = xbc[L - 8:L, :]
    xc = conv * _sigmoid(conv)

    xs = xc[:, :SSD_WIDTH]
    bmat = xc[:, SSD_WIDTH:SSD_WIDTH + 256]
    cmat = xc[:, SSD_WIDTH + 256:]
    xdt = xs * pairs4(ROW_DT)
    xds = xdt * pairs4(ROW_DST)
    ecs_full = pairs4(ROW_ECS)
    dfull = dfull_ref[...]

    y_slabs = []
    for g in range(SSD_GROUPS):
        bm_g = bmat[:, 128 * g:128 * (g + 1)]
        cm_b = cmat[:, 128 * g:128 * (g + 1)].astype(BF16)
        cb = _dot_nt(cm_b, bm_g.astype(BF16))
        s_loc = _dot(bm_g.T.astype(BF16), xds[:, 256 * g:256 * (g + 1)].astype(BF16))
        prev = prev_t[g]
        y_off = _dot(cm_b, prev.astype(BF16)) * ecs_full[:, 256 * g:256 * (g + 1)]
        for kk in range(2):
            k = 2 * g + kk
            slab = xdt[:, 128 * k:128 * (k + 1)].astype(BF16)
            halves = []
            for hl in range(2):
                h = 2 * k + hl
                diff = col(ROW_CS + h) - cs[h:h + 1, :]
                dec = jnp.where(tril, jnp.exp(jnp.where(tril, diff, 0.0)), 0.0)
                halves.append(_dot((cb * dec).astype(BF16), slab))
            y_diag = jnp.where(lo, halves[0], halves[1])
            y_slabs.append(y_diag + y_off[:, 128 * kk:128 * (kk + 1)]
                           + xs[:, 128 * k:128 * (k + 1)] * dfull[:, 128 * k:128 * (k + 1)])
        cd = jnp.concatenate(
            [jnp.where(lo[0:1], cdec[2 * k:2 * k + 1, :], cdec[2 * k + 1:2 * k + 2, :])
             for k in (2 * g, 2 * g + 1)], axis=1)
        prev_t[g] = prev * cd + s_loc
    y = jnp.concatenate(y_slabs, axis=1)
    z = z_ref[0]
    yz = y * (z * _sigmoid(z))
    var = jnp.mean(yz * yz, axis=-1, keepdims=True)
    y_ssd = (yz * lax.rsqrt(var + NORM_EPS)) * ssdnw_ref[...]

    scale = DIFF_QK_DIM ** -0.5
    lam = lam_ref[0]
    q = dq_ref[0]
    lane256 = lax.broadcasted_iota(jnp.int32, (L, 256), 1)
    for h in range(DIFF_HEADS):
        q0 = jnp.where((lane256 >= 64 * h) & (lane256 < 64 * h + 32), q, 0.0)
        q1 = jnp.where((lane256 >= 64 * h + 32) & (lane256 < 64 * h + 64), q, 0.0)
        qbd[h] = jnp.concatenate([q0, q1], axis=0).astype(BF16)
    acc_s[...] = jnp.zeros(acc_s.shape, F32)

    def pair_step(g, carry):
        r0 = pl.multiple_of(g * (2 * L), 2 * L)
        keys = kb[pl.ds(r0, 2 * L), :]
        kind0 = jnp.minimum(c - 2 * g, 2) + 1
        kind1 = jnp.minimum(c - 2 * g - 1, 2) + 1
        out = []
        for h in range(DIFF_HEADS):
            m, l = carry[2 * h], carry[2 * h + 1]
            add = jnp.concatenate([bias[kind0, h], bias[kind1, h]], axis=0)
            s = _dot_nt(keys, qbd[h]) * scale + jnp.concatenate([add, add], axis=1)
            m_n = jnp.maximum(m, jnp.max(s, axis=0, keepdims=True))
            alpha = jnp.exp(m - m_n)
            p = jnp.exp(s - m_n)
            l = alpha * l + jnp.sum(p, axis=0, keepdims=True)
            acc_s[h] = acc_s[h] * alpha + _dot(vt[g, 64 * h:64 * (h + 1), :], p.astype(BF16))
            out += [m_n, l]
        return tuple(out)

    init = (jnp.full((1, 256), MASKED, F32), jnp.zeros((1, 256), F32)) * DIFF_HEADS
    stats = lax.fori_loop(0, c // 2 + 1, pair_step, init)
    heads_t = []
    for h in range(DIFF_HEADS):
        inv = 1.0 / stats[2 * h + 1]
        acc = acc_s[h]
        o = acc[:, :L] * inv[:, :L] - lam * (acc[:, L:] * inv[:, L:])
        ovar = jnp.mean(o * o, axis=0, keepdims=True)
        heads_t.append(((o * lax.rsqrt(ovar + NORM_EPS)) * dnw_ref[...]) * (1.0 - lam_init))
    y_diff = jnp.concatenate(heads_t, axis=0).T

    bd_mask = row_lo == (col_i < 64)
    inv_sqrt_d = MLSTM_HEAD_DIM ** -0.5
    mq = mq_ref[0]
    mk = mk_ref[0]
    mv = mv_ref[0]
    mo = mo_ref[0]
    mnw = mnw_ref[...]
    m_slabs = []
    for k in range(2):
        sl = slice(128 * k, 128 * (k + 1))
        qs = mq[:, sl]
        ks = mk[:, sl]
        q_b = qs.astype(BF16)
        k_b = ks.astype(BF16)
        v_b = mv[:, sl].astype(BF16)
        nums, dens = [], []
        for hl in range(2):
            h = 2 * k + hl
            qm = jnp.where(lo if hl == 0 else jnp.logical_not(lo), qs, 0.0).astype(BF16)
            qk = _dot_nt(qm, k_b)
            arg = col(ROW_A + h) + u_row[h:h + 1, :]
            w_intra = jnp.exp(jnp.where(tril, arg, -jnp.inf))
            a_mat = (qk * inv_sqrt_d) * w_intra
            dens.append(jnp.sum(a_mat, axis=1, keepdims=True))
            nums.append(_dot(a_mat.astype(BF16), v_b))
        num_intra = jnp.where(lo, nums[0], nums[1])
        den_intra = jnp.where(lo, dens[0], dens[1])
        ct_prev = ct[k]
        nm_prev = nm[k]
        wi = pair(ROW_WI, k)
        num = num_intra + wi * _dot(q_b, ct_prev.astype(BF16))
        den = den_intra + wi * _dot(q_b, nm_prev.astype(BF16))
        denom = jnp.maximum(jnp.abs(den), pair(ROW_EN, k))
        hs = num / denom
        sq = hs * hs
        s_lo = jnp.sum(jnp.where(lo, sq, 0.0), axis=1, keepdims=True)
        s_hi = jnp.sum(jnp.where(lo, 0.0, sq), axis=1, keepdims=True)
        hvar = jnp.where(lo, s_lo, s_hi) * (1.0 / MLSTM_HEAD_DIM)
        hn = (hs * lax.rsqrt(hvar + NORM_EPS)) * mnw[:, sl]
        m_slabs.append(_sigmoid(mo[:, sl]) * hn)
        esc = jnp.where(row_lo, e_end[2 * k:2 * k + 1, :], e_end[2 * k + 1:2 * k + 2, :])
        ke = (ks.T * inv_sqrt_d) * esc
        ct_loc = _dot(ke.astype(BF16), v_b)
        n_loc = jnp.broadcast_to(jnp.sum(ke, axis=1, keepdims=True), (L, L))
        ap = jnp.where(row_lo, a_prev[2 * k:2 * k + 1, :], a_prev[2 * k + 1:2 * k + 2, :])
        al = jnp.where(row_lo, a_loc[2 * k:2 * k + 1, :], a_loc[2 * k + 1:2 * k + 2, :])
        ct[k] = ct_prev * ap + jnp.where(bd_mask, ct_loc, 0.0) * al
        nm[k] = nm_prev * ap + jnp.where(bd_mask, n_loc, 0.0) * al
    mst[...] = m_new
    y_ml = jnp.concatenate(m_slabs, axis=1)

    y_ref[0] = jnp.concatenate([y_ssd, y_diff, y_ml], axis=1).astype(BF16)


def _mixer(proj, lam, rel_bias, conv_w, conv_b, dt_bias, a_log, d_skip, ssd_norm_w,
           diff_norm_w, i_bias, f_bias, mlstm_norm_w, *, lam_init):
    bsz, s, _ = proj.shape
    L = CHUNK
    nb = s // L

    def tok(width, col0):
        return pl.BlockSpec((1, L, width), lambda b, c: (b, c, col0 // width))

    def seq(width, col0):
        return pl.BlockSpec((1, s, width), lambda b, c: (b, 0, col0 // width))

    def par(shape):
        return pl.BlockSpec(shape, lambda b, c: (0,) * len(shape))

    smem = pl.BlockSpec(memory_space=pltpu.SMEM)
    pad8 = lambda v: jnp.pad(v.astype(F32), (0, 8 - v.shape[0])).reshape(8, 1)
    params = [
        conv_w.astype(F32), conv_b.reshape(1, SSD_CONV_CH).astype(F32),
        dt_bias.reshape(8, 1).astype(F32), a_log.reshape(8, 1).astype(F32),
        jnp.repeat(d_skip.astype(F32), SSD_HEAD_DIM).reshape(1, SSD_WIDTH),
        ssd_norm_w.reshape(1, SSD_WIDTH).astype(F32),
        jnp.broadcast_to(diff_norm_w.astype(F32)[:, None], (DIFF_V_DIM, L)),
        pad8(i_bias), pad8(f_bias), mlstm_norm_w.reshape(1, MLSTM_WIDTH).astype(F32),
    ]
    in_specs = [smem, smem,
                tok(1024, COL_XBC), tok(512, COL_Z), tok(256, COL_DQ),
                seq(256, COL_DK), seq(256, COL_DV),
                tok(256, COL_MQ), tok(256, COL_MK), tok(256, COL_MV), tok(256, COL_MO),
                tok(128, COL_SMALL)] + [par(p.shape) for p in params]
    scratch = [
        pltpu.VMEM((L + 8, SSD_CONV_CH), F32),
        pltpu.VMEM((SSD_GROUPS, SSD_STATE, 256), F32),
        pltpu.VMEM((2, 128, 128), F32),
        pltpu.VMEM((2, 128, 128), F32),
        pltpu.VMEM((8, 128), F32),
        pltpu.VMEM((s, 256), BF16),
        pltpu.VMEM((nb // 2, 256, 2 * L), BF16),
        pltpu.VMEM((4, DIFF_HEADS, L, L), F32),
        pltpu.VMEM((DIFF_HEADS, 2 * L, 256), BF16),
        pltpu.VMEM((DIFF_HEADS, DIFF_V_DIM, 2 * L), F32),
    ]
    return pl.pallas_call(
        functools.partial(_mixer_kernel, lam_init=lam_init, n_blocks=nb),
        grid=(bsz, nb),
        in_specs=in_specs,
        out_specs=pl.BlockSpec((1, L, MIX_WIDTH), lambda b, c: (b, c, 0)),
        out_shape=jax.ShapeDtypeStruct((bsz, s, MIX_WIDTH), BF16),
        scratch_shapes=scratch,
        compiler_params=pltpu.CompilerParams(dimension_semantics=("arbitrary", "arbitrary"),
                                             vmem_limit_bytes=V7X_VMEM_LIMIT),
        name="mixer",
    )(lam.reshape(1).astype(F32), rel_bias.astype(F32),
      proj, proj, proj, proj, proj, proj, proj, proj, proj, proj, *params)


def _dense_kernel(h_ref, y_ref, p_ref, wo_ref, n2_ref, wg_ref, wu_ref, wd_ref, pg_ref, pp_ref, fn_ref,
                  o_ref, *, final, h_chunk):
    h1 = h_ref[...] + _dot(y_ref[...], wo_ref[...])
    var = jnp.mean(h1 * h1, axis=-1, keepdims=True)
    u = ((h1 * lax.rsqrt(var + NORM_EPS)) * n2_ref[...]).astype(BF16)
    ffn = jnp.zeros(h1.shape, F32)
    for c0 in range(0, FFN_HIDDEN, h_chunk):
        g = _dot(u, wg_ref[:, c0:c0 + h_chunk])
        up = _dot(u, wu_ref[:, c0:c0 + h_chunk])
        a = ((g * _sigmoid(g)) * up).astype(BF16)
        ffn = ffn + _dot(a, wd_ref[c0:c0 + h_chunk, :])
    h2 = h1 + ffn
    gate = _sigmoid(_dot(h2.astype(BF16), pg_ref[...]))
    h3 = h2 + gate * _dot(p_ref[...].astype(BF16), pp_ref[...])
    if final:
        fvar = jnp.mean(h3 * h3, axis=-1, keepdims=True)
        h3 = (h3 * lax.rsqrt(fvar + NORM_EPS)) * fn_ref[...]
    o_ref[...] = h3


def _dense(h2d, y2d, p2d, w_out, norm2_w, w_gate, w_up, w_down, ple_gate_w, ple_proj_w, final_norm_w,
           *, final, tm):
    t = h2d.shape[0]
    row = lambda width: pl.BlockSpec((tm, width), lambda i: (i, 0))
    res = lambda a: pl.BlockSpec(a.shape, lambda i: (0, 0), pipeline_mode=pl.Buffered(1))
    weights = [w_out.astype(BF16), norm2_w.reshape(1, D_MODEL).astype(F32), w_gate.astype(BF16),
               w_up.astype(BF16), w_down.astype(BF16), ple_gate_w.astype(BF16), ple_proj_w.astype(BF16),
               final_norm_w.reshape(1, D_MODEL).astype(F32)]
    return pl.pallas_call(
        functools.partial(_dense_kernel, final=final, h_chunk=1408),
        grid=(t // tm,),
        in_specs=[row(D_MODEL), row(MIX_WIDTH), row(PLE_DIM)] + [res(w) for w in weights],
        out_specs=row(D_MODEL),
        out_shape=jax.ShapeDtypeStruct((t, D_MODEL), F32),
        compiler_params=pltpu.CompilerParams(dimension_semantics=("arbitrary",),
                                             vmem_limit_bytes=V7X_VMEM_LIMIT),
        name="dense",
    )(h2d, y2d, p2d, *weights)


def _pack_w_in(w):
    z, xbc, dt, dq, dk, dv, mq, mk, mv, mo, mi, mf = jnp.split(w, np.cumsum(IN_SPLITS)[:-1], axis=1)
    d = w.shape[0]
    zeros = lambda n: jnp.zeros((d, n), w.dtype)
    small = jnp.concatenate([dt, mi, zeros(SMALL_MF - SMALL_MI - 4), mf, zeros(128 - SMALL_MF - 4)], axis=1)
    return jnp.concatenate([xbc, z, dq, dk, dv, mq, mk, mv, mo, small], axis=1).astype(BF16)


def kernel(x, p, norm1_w, w_in, ssd_conv_w, ssd_conv_b, ssd_dt_bias, ssd_a_log, ssd_d, ssd_norm_w, diff_lq1, diff_lk1, diff_lq2, diff_lk2, diff_norm_w, rel_bias, mlstm_i_bias, mlstm_f_bias, mlstm_norm_w, w_out, norm2_w, w_ffn_gate, w_ffn_up, w_ffn_down, ple_gate_w, ple_proj_w, final_norm_w):
    bsz, s, d = x.shape
    depth = w_in.shape[0]
    assert d == D_MODEL and s % (2 * CHUNK) == 0
    t = bsz * s
    tm = 512 if t % 512 == 0 else CHUNK
    h = x.reshape(t, d).astype(F32)
    for i in range(depth):
        proj = _in_proj(h, norm1_w[i], _pack_w_in(w_in[i]), tm=tm).reshape(bsz, s, PROJ_COLS)
        lam_init = 0.8 - 0.6 * math.exp(-0.3 * i)
        lam = (jnp.exp(jnp.sum(diff_lq1[i].astype(F32) * diff_lk1[i].astype(F32)))
               - jnp.exp(jnp.sum(diff_lq2[i].astype(F32) * diff_lk2[i].astype(F32))) + lam_init)
        y = _mixer(proj, lam, rel_bias, ssd_conv_w[i], ssd_conv_b[i], ssd_dt_bias[i], ssd_a_log[i],
                   ssd_d[i], ssd_norm_w[i], diff_norm_w[i], mlstm_i_bias[i], mlstm_f_bias[i],
                   mlstm_norm_w[i], lam_init=lam_init)
        h = _dense(h, y.reshape(t, MIX_WIDTH), p[i].reshape(t, PLE_DIM).astype(F32), w_out[i], norm2_w[i],
                   w_ffn_gate[i], w_ffn_up[i], w_ffn_down[i], ple_gate_w[i], ple_proj_w[i], final_norm_w,
                   final=(i == depth - 1), tm=tm)
    return h.reshape(bsz, s, d).astype(x.dtype)
```

```python
import functools
import math

import numpy as np
import jax
import jax.numpy as jnp
from jax import lax
from jax.experimental import pallas as pl
from jax.experimental.pallas import tpu as pltpu

F32 = jnp.float32
BF16 = jnp.bfloat16

D_MODEL = 1024
PLE_DIM = 256
NORM_EPS = 1e-6
CHUNK = 128
KEY_GROUP = 4

SSD_HEADS = 8
SSD_HEAD_DIM = 64
SSD_WIDTH = 512
SSD_GROUPS = 2
SSD_STATE = 128
SSD_CONV = 4
SSD_CONV_CH = 1024
DIFF_HEADS = 4
DIFF_QK_DIM = 32
DIFF_V_DIM = 64
DIFF_WIDTH = 256
REL_BUCKETS = 32
REL_MAX_DIST = 128
MLSTM_HEADS = 4
MLSTM_HEAD_DIM = 64
MLSTM_WIDTH = 256
MIX_WIDTH = 1024
FFN_HIDDEN = 2816
IN_SPLITS = (512, 1024, 8, 256, 256, 256, 256, 256, 256, 256, 4, 4)

COL_XBC, COL_Z, COL_DQ, COL_DK, COL_DV = 0, 1024, 1536, 1792, 2048
COL_MQ, COL_MK, COL_MV, COL_MO, COL_SMALL_A, COL_SMALL_B = 2304, 2560, 2816, 3072, 3328, 3456
PROJ_COLS = 3584
LANE_SSD, LANE_ML = 0, 8

ROW_TILE = 512
PROJ_N_CHUNK = 1792
FFN_CHUNK = 1408

V7X_VMEM_LIMIT = 56 * 1024 * 1024
MASKED = -1e30
LOG2E = math.log2(math.e)


def _t5_bucket_lower_bounds():
    max_exact = REL_BUCKETS // 2
    d = np.arange(0, 2 * CHUNK, dtype=np.int64)
    df = np.maximum(d, 1).astype(np.float32)
    large = max_exact + (np.log(df / np.float32(max_exact)) / np.float32(math.log(REL_MAX_DIST / max_exact))
                         * np.float32(REL_BUCKETS - max_exact)).astype(np.int32)
    large = np.minimum(large, REL_BUCKETS - 1)
    bucket = np.where(d < max_exact, d, large)
    assert np.all(np.diff(bucket) >= 0) and bucket[CHUNK] == REL_BUCKETS - 1
    return [int(np.argmax(bucket >= b)) for b in range(REL_BUCKETS)]


_BUCKET_LO = _t5_bucket_lower_bounds()


def _sigmoid(x):
    return 1.0 / (1.0 + jnp.exp(-x))


def _softplus(x):
    return jnp.maximum(x, 0.0) + jnp.log1p(jnp.exp(-jnp.abs(x)))


def _dot(a, b):
    return jnp.dot(a, b, preferred_element_type=F32)


def _dot_nt(a, b):
    return lax.dot_general(a, b, (((1,), (1,)), ((), ())), preferred_element_type=F32)


def _scan_rows(x, combine, identity):
    sub = lax.broadcasted_iota(jnp.int32, (8, x.shape[1]), 0)
    tiles, carry = [], None
    for i in range(x.shape[0] // 8):
        t = x[8 * i:8 * (i + 1)]
        for sh in (1, 2, 4):
            t = combine(t, jnp.where(sub >= sh, pltpu.roll(t, sh, 0), identity))
        if carry is not None:
            t = combine(t, carry)
        carry = t[7:8]
        tiles.append(t)
    return jnp.concatenate(tiles, axis=0)


def _in_proj_kernel(x_ref, nw_ref, w_ref, o_ref, *, n_chunk):
    x = x_ref[...]
    var = jnp.mean(x * x, axis=-1, keepdims=True)
    u = ((x * lax.rsqrt(var + NORM_EPS)) * nw_ref[...]).astype(BF16)
    for n0 in range(0, PROJ_COLS, n_chunk):
        o_ref[:, n0:n0 + n_chunk] = _dot(u, w_ref[:, n0:n0 + n_chunk])


def _in_proj(h2d, norm_w, w_packed, *, tm):
    t = h2d.shape[0]
    const = lambda i: (0, 0)
    return pl.pallas_call(
        functools.partial(_in_proj_kernel, n_chunk=PROJ_N_CHUNK),
        grid=(t // tm,),
        in_specs=[pl.BlockSpec((tm, D_MODEL), lambda i: (i, 0)),
                  pl.BlockSpec((1, D_MODEL), const),
                  pl.BlockSpec((D_MODEL, PROJ_COLS), const, pipeline_mode=pl.Buffered(1))],
        out_specs=pl.BlockSpec((tm, PROJ_COLS), lambda i: (i, 0)),
        out_shape=jax.ShapeDtypeStruct((t, PROJ_COLS), F32),
        compiler_params=pltpu.CompilerParams(dimension_semantics=("arbitrary",),
                                             vmem_limit_bytes=V7X_VMEM_LIMIT),
        name="in_proj",
    )(h2d, norm_w.reshape(1, D_MODEL), w_packed)


def _mixer_kernel(lam_ref, relb_ref,
                  xbc_ref, z_ref, dq_ref, dk_ref, dv_ref, mq_ref, mk_ref, mv_ref, mo_ref, sma_ref, smb_ref,
                  convw_ref, convb_ref, biasa_ref, biasb_ref, alog_ref, dfull_ref, ssdnw_ref,
                  dnw_ref, mnw_ref,
                  y_ref,
                  xpad, prev_t, ct, nm, mst, kb, vb, bias, qbd, rstat, acc_s,
                  *, lam_init, n_blocks):
    L = CHUNK
    b = pl.program_id(0)
    c = pl.program_id(1)

    lane128 = lax.broadcasted_iota(jnp.int32, (L, 128), 1)
    lo = lane128 < 64
    row_i = lax.broadcasted_iota(jnp.int32, (L, L), 0)
    col_i = lax.broadcasted_iota(jnp.int32, (L, L), 1)
    tril = row_i >= col_i
    row_lo = row_i < 64

    @pl.when(jnp.logical_and(b == 0, c == 0))
    def _():
        for h in range(DIFF_HEADS):
            bias[0, h] = jnp.full((L, L), MASKED, F32)
            bias[3, h] = jnp.zeros((L, L), F32)
            far = relb_ref[REL_BUCKETS - 1, h]
            for which, off in ((1, 0), (2, L)):
                dist = row_i - col_i + off
                v = jnp.full((L, L), far, F32)
                for bkt in range(REL_BUCKETS - 2, -1, -1):
                    v = jnp.where(dist < _BUCKET_LO[bkt + 1], relb_ref[bkt, h], v)
                bias[which, h] = jnp.where(dist >= 0, (v - far) * LOG2E, MASKED)

    @pl.when(c == 0)
    def _():
        xpad[0:8, :] = jnp.zeros((8, SSD_CONV_CH), F32)
        prev_t[...] = jnp.zeros(prev_t.shape, F32)
        ct[...] = jnp.zeros(ct.shape, F32)
        nm[...] = jnp.zeros(nm.shape, F32)
        mst[...] = jnp.zeros(mst.shape, F32)

        def fill(t, carry):
            r0 = pl.multiple_of(t * L, L)
            kb[pl.ds(r0, L), :] = dk_ref[0, pl.ds(r0, L), :].astype(BF16)
            v = dv_ref[0, pl.ds(r0, L), :]
            for h in range(DIFF_HEADS):
                own = lo if h % 2 == 0 else jnp.logical_not(lo)
                vb[h, pl.ds(r0, L), :] = jnp.where(own, v[:, 128 * (h // 2):128 * (h // 2 + 1)], 1.0).astype(BF16)
            return carry

        lax.fori_loop(0, n_blocks, fill, 0)

    pre_a = sma_ref[0] + biasa_ref[...]
    log_f = -_softplus(-(smb_ref[0] + biasb_ref[...]))
    ssd_lane = lane128 < LANE_ML
    a_neg = jnp.where(ssd_lane[0:1], -jnp.exp(alog_ref[...]), 0.0)
    dt_c = _softplus(pre_a)
    csum = _scan_rows(jnp.where(ssd_lane, dt_c * a_neg, log_f), jnp.add, 0.0)
    last = csum[L - 1:L]
    u_c = pre_a - csum
    m_intra = csum + _scan_rows(u_c, jnp.maximum, -jnp.inf)
    w_end = last - csum + pre_a
    m_loc = jnp.max(w_end, axis=0, keepdims=True)
    e_end = jnp.exp(w_end - m_loc)
    m_prev = mst[0:1]
    inter_log = csum + m_prev
    m_t = jnp.maximum(inter_log, m_intra)
    w_inter = jnp.exp(inter_log - m_t)
    a_col = csum - m_t
    en = jnp.exp(-m_t)
    m_new = jnp.maximum(last + m_prev, m_loc)
    a_prev = jnp.exp(last + m_prev - m_new)
    a_loc = jnp.exp(m_loc - m_new)
    rows_t = jnp.where(ssd_lane, csum, u_c).T

    def bcast(arr, j):
        return jnp.broadcast_to(arr[:, j:j + 1], (arr.shape[0], 128))

    def pair(arr, j0, k):
        return jnp.where(lo[:arr.shape[0]], bcast(arr, j0 + 2 * k), bcast(arr, j0 + 2 * k + 1))

    xbc = xbc_ref[0]
    xpad[8:8 + L, :] = xbc
    conv = convb_ref[...]
    for k in range(SSD_CONV):
        conv = conv + convw_ref[k:k + 1, :] * xpad[5 + k:5 + k + L, :]
    xpad[0:8, :] = xbc[L - 8:L, :]
    xc = conv * _sigmoid(conv)

    xs = xc[:, :SSD_WIDTH]
    bmat = xc[:, SSD_WIDTH:SSD_WIDTH + 256]
    cmat = xc[:, SSD_WIDTH + 256:]
    dfull = dfull_ref[...]
    cs_b = [bcast(csum, LANE_SSD + h) for h in range(SSD_HEADS)]
    xdt, xds, ecs, cdec = [], [], [], []
    for k in range(4):
        cs_p = jnp.where(lo, cs_b[2 * k], cs_b[2 * k + 1])
        last_p = pair(last, LANE_SSD, k)
        xdt.append(xs[:, 128 * k:128 * (k + 1)] * pair(dt_c, LANE_SSD, k))
        xds.append((xdt[k] * jnp.exp(last_p - cs_p)).astype(BF16))
        ecs.append(jnp.exp(cs_p))
        cdec.append(jnp.exp(last_p))

    y_slabs = []
    for g in range(SSD_GROUPS):
        bm_g = bmat[:, 128 * g:128 * (g + 1)]
        cm_b = cmat[:, 128 * g:128 * (g + 1)].astype(BF16)
        cb = _dot_nt(cm_b, bm_g.astype(BF16))
        s_loc = _dot(bm_g.T.astype(BF16), jnp.concatenate(xds[2 * g:2 * g + 2], axis=1))
        prev = prev_t[g]
        y_off = _dot(cm_b, prev.astype(BF16))
        for kk in range(2):
            k = 2 * g + kk
            slab = xdt[k].astype(BF16)
            halves = []
            for hl in range(2):
                h = 2 * k + hl
                diff = cs_b[h] - rows_t[LANE_SSD + h:LANE_SSD + h + 1, :]
                dec = jnp.where(tril, jnp.exp(jnp.where(tril, diff, 0.0)), 0.0)
                halves.append(_dot((cb * dec).astype(BF16), slab))
            y_diag = jnp.where(lo, halves[0], halves[1])
            y_slabs.append(y_diag + y_off[:, 128 * kk:128 * (kk + 1)] * ecs[k]
                           + xs[:, 128 * k:128 * (k + 1)] * dfull[:, 128 * k:128 * (k + 1)])
        prev_t[g] = prev * jnp.concatenate(cdec[2 * g:2 * g + 2], axis=1) + s_loc
    y = jnp.concatenate(y_slabs, axis=1)
    z = z_ref[0]
    yz = y * (z * _sigmoid(z))
    var = jnp.mean(yz * yz, axis=-1, keepdims=True)
    y_ssd = (yz * lax.rsqrt(var + NORM_EPS)) * ssdnw_ref[...]

    c1 = (DIFF_QK_DIM ** -0.5) * LOG2E
    lam = lam_ref[0]
    q = dq_ref[0]
    for h in range(DIFF_HEADS):
        qs = q[:, 128 * (h // 2):128 * (h // 2 + 1)]
        base = 64 * (h % 2)
        q0 = jnp.where((lane128 >= base) & (lane128 < base + 32), qs, 0.0)
        q1 = jnp.where((lane128 >= base + 32) & (lane128 < base + 64), qs, 0.0)
        qbd[h] = jnp.concatenate([q0, q1], axis=0).astype(BF16)
    n_groups = c // KEY_GROUP + 1
    gw = KEY_GROUP * L

    def scores(g, h):
        r0 = pl.multiple_of(g * gw, gw)
        keys = kb[pl.ds(r0, gw), 128 * (h // 2):128 * (h // 2 + 1)]
        add = jnp.concatenate(
            [bias[jnp.clip(c - KEY_GROUP * g - j, -1, 2) + 1, h] for j in range(KEY_GROUP)], axis=1)
        return _dot_nt(qbd[h], keys) * c1 + jnp.concatenate([add, add], axis=0)

    rstat[...] = jnp.full(rstat.shape, MASKED, F32)

    def pass1(g, carry):
        for h in range(DIFF_HEADS):
            s = scores(g, h)
            m = rstat[h]
            for j in range(KEY_GROUP):
                m = jnp.maximum(m, s[:, L * j:L * (j + 1)])
            rstat[h] = m
        return carry

    lax.fori_loop(0, n_groups, pass1, 0)
    for h in range(DIFF_HEADS):
        rstat[h] = jnp.broadcast_to(jnp.max(rstat[h], axis=1, keepdims=True), (2 * L, L))
    acc_s[...] = jnp.zeros(acc_s.shape, F32)

    def pass2(g, carry):
        r0 = pl.multiple_of(g * gw, gw)
        for h in range(DIFF_HEADS):
            s = scores(g, h)
            m = rstat[h]
            p = jnp.concatenate([jnp.exp2(s[:, L * j:L * (j + 1)] - m) for j in range(KEY_GROUP)], axis=1)
            acc_s[h] = acc_s[h] + _dot(p.astype(BF16), vb[h, pl.ds(r0, gw), :])
        return carry

    lax.fori_loop(0, n_groups, pass2, 0)
    d_slabs = []
    for k in range(2):
        halves = []
        for hl in range(2):
            h = 2 * k + hl
            acc = acc_s[h]
            r = acc * (1.0 / pltpu.roll(acc, 64, 1))
            halves.append(r[:L] - lam * r[L:])
        o = jnp.where(lo, halves[0], halves[1])
        sq = o * o
        s_lo = jnp.sum(jnp.where(lo, sq, 0.0), axis=1, keepdims=True)
        s_hi = jnp.sum(jnp.where(lo, 0.0, sq), axis=1, keepdims=True)
        ovar = jnp.where(lo, s_lo, s_hi) * (1.0 / DIFF_V_DIM)
        d_slabs.append(((o * lax.rsqrt(ovar + NORM_EPS)) * dnw_ref[:, 128 * k:128 * (k + 1)])
                       * (1.0 - lam_init))
    y_diff = jnp.concatenate(d_slabs, axis=1)

    bd_mask = row_lo == (col_i < 64)
    inv_sqrt_d = MLSTM_HEAD_DIM ** -0.5
    ones_b = jnp.ones((L, 128), BF16)
    mq = mq_ref[0]
    mk = mk_ref[0]
    mv = mv_ref[0]
    mo = mo_ref[0]
    mnw = mnw_ref[...]
    m_slabs = []
    for k in range(2):
        sl = slice(128 * k, 128 * (k + 1))
        qs = mq[:, sl]
        ks = mk[:, sl]
        vs = mv[:, sl]
        q_b = qs.astype(BF16)
        k_b = ks.astype(BF16)
        v1 = jnp.concatenate([vs.astype(BF16), ones_b], axis=1)
        intra = []
        for hl in range(2):
            h = 2 * k + hl
            qm = jnp.where(lo if hl == 0 else jnp.logical_not(lo), qs, 0.0).astype(BF16)
            qk = _dot_nt(qm, k_b)
            arg = bcast(a_col, LANE_ML + h) + rows_t[LANE_ML + h:LANE_ML + h + 1, :]
            w_intra = jnp.exp(jnp.where(tril, arg, -jnp.inf))
            intra.append(_dot(((qk * inv_sqrt_d) * w_intra).astype(BF16), v1))
        num_intra = jnp.where(lo, intra[0][:, :128], intra[1][:, :128])
        den_intra = jnp.where(lo, intra[0][:, 128:], intra[1][:, 128:])
        ct_prev = ct[k]
        nm_prev = nm[k]
        wi = pair(w_inter, LANE_ML, k)
        num = num_intra + wi * _dot(q_b, ct_prev.astype(BF16))
        den = den_intra + wi * _dot(q_b, nm_prev.astype(BF16))
        denom = jnp.maximum(jnp.abs(den), pair(en, LANE_ML, k))
        hs = num / denom
        sq = hs * hs
        s_lo = jnp.sum(jnp.where(lo, sq, 0.0), axis=1, keepdims=True)
        s_hi = jnp.sum(jnp.where(lo, 0.0, sq), axis=1, keepdims=True)
        hvar = jnp.where(lo, s_lo, s_hi) * (1.0 / MLSTM_HEAD_DIM)
        hn = (hs * lax.rsqrt(hvar + NORM_EPS)) * mnw[:, sl]
        m_slabs.append(_sigmoid(mo[:, sl]) * hn)
        e_p = pair(e_end, LANE_ML, k)
        loc = _dot((ks.T * inv_sqrt_d).astype(BF16),
                   jnp.concatenate([(vs * e_p).astype(BF16), e_p.astype(BF16)], axis=1))
        ap = pair(a_prev, LANE_ML, k)
        al = pair(a_loc, LANE_ML, k)
        ct[k] = ct_prev * ap + jnp.where(bd_mask, loc[:, :128], 0.0) * al
        nm[k] = nm_prev * ap + jnp.where(bd_mask, loc[:, 128:], 0.0) * al
    mst[...] = jnp.broadcast_to(m_new, mst.shape)
    y_ml = jnp.concatenate(m_slabs, axis=1)

    y_ref[0] = jnp.concatenate([y_ssd, y_diff, y_ml], axis=1).astype(BF16)


def _mixer(proj, lam, rel_bias, conv_w, conv_b, dt_bias, a_log, d_skip, ssd_norm_w,
           diff_norm_w, i_bias, f_bias, mlstm_norm_w, *, lam_init):
    bsz, s, _ = proj.shape
    L = CHUNK
    nb = s // L

    def tok(width, col0):
        return pl.BlockSpec((1, L, width), lambda b, c: (b, c, col0 // width))

    def seq(width, col0):
        return pl.BlockSpec((1, s, width), lambda b, c: (b, 0, col0 // width))

    def par(shape):
        return pl.BlockSpec(shape, lambda b, c: (0,) * len(shape))

    smem = pl.BlockSpec(memory_space=pltpu.SMEM)

    def lanes(*placed):
        row = jnp.zeros((128,), F32)
        for off, v in placed:
            row = row.at[off:off + v.shape[0]].set(v.astype(F32))
        return row.reshape(1, 128)

    params = [
        conv_w.astype(F32), conv_b.reshape(1, SSD_CONV_CH).astype(F32),
        lanes((LANE_SSD, dt_bias), (LANE_ML, i_bias)), lanes((LANE_ML, f_bias)), lanes((LANE_SSD, a_log)),
        jnp.repeat(d_skip.astype(F32), SSD_HEAD_DIM).reshape(1, SSD_WIDTH),
        ssd_norm_w.reshape(1, SSD_WIDTH).astype(F32),
        jnp.tile(diff_norm_w.astype(F32), DIFF_HEADS).reshape(1, DIFF_WIDTH),
        mlstm_norm_w.reshape(1, MLSTM_WIDTH).astype(F32),
    ]
    in_specs = [smem, smem,
                tok(1024, COL_XBC), tok(512, COL_Z), tok(256, COL_DQ),
                seq(256, COL_DK), seq(256, COL_DV),
                tok(256, COL_MQ), tok(256, COL_MK), tok(256, COL_MV), tok(256, COL_MO),
                tok(128, COL_SMALL_A), tok(128, COL_SMALL_B)] + [par(p.shape) for p in params]
    scratch = [
        pltpu.VMEM((L + 8, SSD_CONV_CH), F32),
        pltpu.VMEM((SSD_GROUPS, SSD_STATE, 256), F32),
        pltpu.VMEM((2, 128, 128), F32),
        pltpu.VMEM((2, 128, 128), F32),
        pltpu.VMEM((8, 128), F32),
        pltpu.VMEM((s, 256), BF16),
        pltpu.VMEM((DIFF_HEADS, s, 128), BF16),
        pltpu.VMEM((4, DIFF_HEADS, L, L), F32),
        pltpu.VMEM((DIFF_HEADS, 2 * L, 128), BF16),
        pltpu.VMEM((DIFF_HEADS, 2 * L, L), F32),
        pltpu.VMEM((DIFF_HEADS, 2 * L, 128), F32),
    ]
    return pl.pallas_call(
        functools.partial(_mixer_kernel, lam_init=lam_init, n_blocks=nb),
        grid=(bsz, nb),
        in_specs=in_specs,
        out_specs=pl.BlockSpec((1, L, MIX_WIDTH), lambda b, c: (b, c, 0)),
        out_shape=jax.ShapeDtypeStruct((bsz, s, MIX_WIDTH), BF16),
        scratch_shapes=scratch,
        compiler_params=pltpu.CompilerParams(dimension_semantics=("arbitrary", "arbitrary"),
                                             vmem_limit_bytes=V7X_VMEM_LIMIT),
        name="mixer",
    )(lam.reshape(1).astype(F32), rel_bias.astype(F32),
      *([proj] * 11), *params)


def _dense_kernel(h_ref, y_ref, p_ref, wo_ref, n2_ref, wg_ref, wu_ref, wd_ref, pg_ref, pp_ref, fn_ref,
                  o_ref, *, final, h_chunk):
    h1 = h_ref[...] + _dot(y_ref[...], wo_ref[...])
    var = jnp.mean(h1 * h1, axis=-1, keepdims=True)
    u = ((h1 * lax.rsqrt(var + NORM_EPS)) * n2_ref[...]).astype(BF16)
    ffn = jnp.zeros(h1.shape, F32)
    for c0 in range(0, FFN_HIDDEN, h_chunk):
        g = _dot(u, wg_ref[:, c0:c0 + h_chunk])
        up = _dot(u, wu_ref[:, c0:c0 + h_chunk])
        a = ((g * _sigmoid(g)) * up).astype(BF16)
        ffn = ffn + _dot(a, wd_ref[c0:c0 + h_chunk, :])
    h2 = h1 + ffn
    gate = _sigmoid(_dot(h2.astype(BF16), pg_ref[...]))
    h3 = h2 + gate * _dot(p_ref[...].astype(BF16), pp_ref[...])
    if final:
        fvar = jnp.mean(h3 * h3, axis=-1, keepdims=True)
        h3 = (h3 * lax.rsqrt(fvar + NORM_EPS)) * fn_ref[...]
    o_ref[...] = h3


def _dense(h2d, y2d, p2d, w_out, norm2_w, w_gate, w_up, w_down, ple_gate_w, ple_proj_w, final_norm_w,
           *, final, tm):
    t = h2d.shape[0]
    row = lambda width: pl.BlockSpec((tm, width), lambda i: (i, 0))
    res = lambda a: pl.BlockSpec(a.shape, lambda i: (0, 0), pipeline_mode=pl.Buffered(1))
    weights = [w_out.astype(BF16), norm2_w.reshape(1, D_MODEL).astype(F32), w_gate.astype(BF16),
               w_up.astype(BF16), w_down.astype(BF16), ple_gate_w.astype(BF16), ple_proj_w.astype(BF16),
               final_norm_w.reshape(1, D_MODEL).astype(F32)]
    return pl.pallas_call(
        functools.partial(_dense_kernel, final=final, h_chunk=FFN_CHUNK),
        grid=(t // tm,),
        in_specs=[row(D_MODEL), row(MIX_WIDTH), row(PLE_DIM)] + [res(w) for w in weights],
        out_specs=row(D_MODEL),
        out_shape=jax.ShapeDtypeStruct((t, D_MODEL), F32),
        compiler_params=pltpu.CompilerParams(dimension_semantics=("arbitrary",),
                                             vmem_limit_bytes=V7X_VMEM_LIMIT),
        name="dense",
    )(h2d, y2d, p2d, *weights)


def _pack_w_in(w):
    z, xbc, dt, dq, dk, dv, mq, mk, mv, mo, mi, mf = jnp.split(w, np.cumsum(IN_SPLITS)[:-1], axis=1)
    d = w.shape[0]
    zeros = lambda n: jnp.zeros((d, n), w.dtype)
    small_a = jnp.concatenate([dt, mi, zeros(128 - LANE_ML - MLSTM_HEADS)], axis=1)
    small_b = jnp.concatenate([zeros(LANE_ML), mf, zeros(128 - LANE_ML - MLSTM_HEADS)], axis=1)
    return jnp.concatenate([xbc, z, dq, dk, dv, mq, mk, mv, mo, small_a, small_b], axis=1).astype(BF16)


def kernel(x, p, norm1_w, w_in, ssd_conv_w, ssd_conv_b, ssd_dt_bias, ssd_a_log, ssd_d, ssd_norm_w, diff_lq1, diff_lk1, diff_lq2, diff_lk2, diff_norm_w, rel_bias, mlstm_i_bias, mlstm_f_bias, mlstm_norm_w, w_out, norm2_w, w_ffn_gate, w_ffn_up, w_ffn_down, ple_gate_w, ple_proj_w, final_norm_w):
    bsz, s, d = x.shape
    depth = w_in.shape[0]
    assert d == D_MODEL and s % (KEY_GROUP * CHUNK) == 0
    t = bsz * s
    tm = ROW_TILE if t % ROW_TILE == 0 else CHUNK
    h = x.reshape(t, d).astype(F32)
    for i in range(depth):
        proj = _in_proj(h, norm1_w[i], _pack_w_in(w_in[i]), tm=tm).reshape(bsz, s, PROJ_COLS)
        lam_init = 0.8 - 0.6 * math.exp(-0.3 * i)
        lam = (jnp.exp(jnp.sum(diff_lq1[i].astype(F32) * diff_lk1[i].astype(F32)))
               - jnp.exp(jnp.sum(diff_lq2[i].astype(F32) * diff_lk2[i].astype(F32))) + lam_init)
        y = _mixer(proj, lam, rel_bias, ssd_conv_w[i], ssd_conv_b[i], ssd_dt_bias[i], ssd_a_log[i],
                   ssd_d[i], ssd_norm_w[i], diff_norm_w[i], mlstm_i_bias[i], mlstm_f_bias[i],
                   mlstm_norm_w[i], lam_init=lam_init)
        h = _dense(h, y.reshape(t, MIX_WIDTH), p[i].reshape(t, PLE_DIM).astype(F32), w_out[i], norm2_w[i],
                   w_ffn_gate[i], w_ffn_up[i], w_ffn_down[i], ple_gate_w[i], ple_proj_w[i], final_norm_w,
                   final=(i == depth - 1), tm=tm)
    return h.reshape(bsz, s, d).astype(x.dtype)
```

```python
import functools
import math

import numpy as np
import jax
import jax.numpy as jnp
from jax import lax
from jax.experimental import pallas as pl
from jax.experimental.pallas import tpu as pltpu

F32 = jnp.float32
BF16 = jnp.bfloat16

D_MODEL = 1024
PLE_DIM = 256
NORM_EPS = 1e-6
CHUNK = 128
KEY_GROUP = 4

SSD_HEADS = 8
SSD_HEAD_DIM = 64
SSD_WIDTH = 512
SSD_GROUPS = 2
SSD_STATE = 128
SSD_CONV = 4
SSD_CONV_CH = 1024
DIFF_HEADS = 4
DIFF_QK_DIM = 32
DIFF_V_DIM = 64
DIFF_WIDTH = 256
REL_BUCKETS = 32
REL_MAX_DIST = 128
MLSTM_HEADS = 4
MLSTM_HEAD_DIM = 64
MLSTM_WIDTH = 256
MIX_WIDTH = 1024
FFN_HIDDEN = 2816
IN_SPLITS = (512, 1024, 8, 256, 256, 256, 256, 256, 256, 256, 4, 4)

COL_XBC, COL_Z, COL_DQ, COL_DK, COL_DV = 0, 1024, 1536, 1792, 2048
COL_MQ, COL_MK, COL_MV, COL_MO, COL_SMALL_A, COL_SMALL_B = 2304, 2560, 2816, 3072, 3328, 3456
PROJ_COLS = 3584
LANE_SSD, LANE_ML = 0, 8

ROW_TILE = 512
PROJ_N_CHUNK = 1792
FFN_CHUNK = 1408

V7X_VMEM_LIMIT = 56 * 1024 * 1024
MASKED = -1e30
LOG2E = math.log2(math.e)


def _t5_bucket_lower_bounds():
    max_exact = REL_BUCKETS // 2
    d = np.arange(0, 2 * CHUNK, dtype=np.int64)
    df = np.maximum(d, 1).astype(np.float32)
    large = max_exact + (np.log(df / np.float32(max_exact)) / np.float32(math.log(REL_MAX_DIST / max_exact))
                         * np.float32(REL_BUCKETS - max_exact)).astype(np.int32)
    large = np.minimum(large, REL_BUCKETS - 1)
    bucket = np.where(d < max_exact, d, large)
    assert np.all(np.diff(bucket) >= 0) and bucket[CHUNK] == REL_BUCKETS - 1
    return [int(np.argmax(bucket >= b)) for b in range(REL_BUCKETS)]


_BUCKET_LO = _t5_bucket_lower_bounds()


def _sigmoid(x):
    return 1.0 / (1.0 + jnp.exp(-x))


def _softplus(x):
    return jnp.maximum(x, 0.0) + jnp.log(1.0 + jnp.exp(-jnp.abs(x)))


def _dot(a, b):
    return jnp.dot(a, b, preferred_element_type=F32)


def _dot_nt(a, b):
    return lax.dot_general(a, b, (((1,), (1,)), ((), ())), preferred_element_type=F32)


def _scan_rows(x, combine, identity):
    sub = lax.broadcasted_iota(jnp.int32, (8, x.shape[1]), 0)
    tiles, carry = [], None
    for i in range(x.shape[0] // 8):
        t = x[8 * i:8 * (i + 1)]
        for sh in (1, 2, 4):
            t = combine(t, jnp.where(sub >= sh, pltpu.roll(t, sh, 0), identity))
        if carry is not None:
            t = combine(t, carry)
        carry = t[7:8]
        tiles.append(t)
    return jnp.concatenate(tiles, axis=0)


def _in_proj_kernel(x_ref, nw_ref, w_ref, o_ref, *, n_chunk):
    x = x_ref[...]
    var = jnp.mean(x * x, axis=-1, keepdims=True)
    u = ((x * lax.rsqrt(var + NORM_EPS)) * nw_ref[...]).astype(BF16)
    for n0 in range(0, PROJ_COLS, n_chunk):
        o_ref[:, n0:n0 + n_chunk] = _dot(u, w_ref[:, n0:n0 + n_chunk])


def _in_proj(h2d, norm_w, w_packed, *, tm):
    t = h2d.shape[0]
    const = lambda i: (0, 0)
    return pl.pallas_call(
        functools.partial(_in_proj_kernel, n_chunk=PROJ_N_CHUNK),
        grid=(t // tm,),
        in_specs=[pl.BlockSpec((tm, D_MODEL), lambda i: (i, 0)),
                  pl.BlockSpec((1, D_MODEL), const),
                  pl.BlockSpec((D_MODEL, PROJ_COLS), const, pipeline_mode=pl.Buffered(1))],
        out_specs=pl.BlockSpec((tm, PROJ_COLS), lambda i: (i, 0)),
        out_shape=jax.ShapeDtypeStruct((t, PROJ_COLS), F32),
        compiler_params=pltpu.CompilerParams(dimension_semantics=("arbitrary",),
                                             vmem_limit_bytes=V7X_VMEM_LIMIT),
        name="in_proj",
    )(h2d, norm_w.reshape(1, D_MODEL), w_packed)


def _mixer_kernel(lam_ref, relb_ref,
                  xbc_ref, z_ref, dq_ref, dk_ref, dv_ref, mq_ref, mk_ref, mv_ref, mo_ref, sma_ref, smb_ref,
                  convw_ref, convb_ref, biasa_ref, biasb_ref, alog_ref, dfull_ref, ssdnw_ref,
                  dnw_ref, mnw_ref,
                  y_ref,
                  xpad, prev_t, ct, nm, mst, kb, vb, bias, qbd, rstat, acc_s, s_even, s_odd,
                  *, lam_init, n_blocks):
    L = CHUNK
    b = pl.program_id(0)
    c = pl.program_id(1)

    lane128 = lax.broadcasted_iota(jnp.int32, (L, 128), 1)
    lo = lane128 < 64
    row_i = lax.broadcasted_iota(jnp.int32, (L, L), 0)
    col_i = lax.broadcasted_iota(jnp.int32, (L, L), 1)
    tril = row_i >= col_i
    row_lo = row_i < 64

    @pl.when(jnp.logical_and(b == 0, c == 0))
    def _():
        for h in range(DIFF_HEADS):
            bias[0, h] = jnp.full((L, L), MASKED, F32)
            bias[3, h] = jnp.zeros((L, L), F32)
            far = relb_ref[REL_BUCKETS - 1, h]
            for which, off in ((1, 0), (2, L)):
                dist = row_i - col_i + off
                v = jnp.full((L, L), far, F32)
                for bkt in range(REL_BUCKETS - 2, -1, -1):
                    v = jnp.where(dist < _BUCKET_LO[bkt + 1], relb_ref[bkt, h], v)
                bias[which, h] = jnp.where(dist >= 0, (v - far) * LOG2E, MASKED)

    @pl.when(c == 0)
    def _():
        xpad[0:8, :] = jnp.zeros((8, SSD_CONV_CH), F32)
        prev_t[...] = jnp.zeros(prev_t.shape, F32)
        ct[...] = jnp.zeros(ct.shape, F32)
        nm[...] = jnp.zeros(nm.shape, F32)
        mst[...] = jnp.zeros(mst.shape, F32)

        def fill(t, carry):
            r0 = pl.multiple_of(t * L, L)
            kb[pl.ds(r0, L), :] = dk_ref[0, pl.ds(r0, L), :].astype(BF16)
            v = dv_ref[0, pl.ds(r0, L), :]
            for h in range(DIFF_HEADS):
                own = lo if h % 2 == 0 else jnp.logical_not(lo)
                vb[h, pl.ds(r0, L), :] = jnp.where(own, v[:, 128 * (h // 2):128 * (h // 2 + 1)], 1.0).astype(BF16)
            return carry

        lax.fori_loop(0, n_blocks, fill, 0)

    pre_a = sma_ref[0] + biasa_ref[...]
    log_f = -_softplus(-(smb_ref[0] + biasb_ref[...]))
    ssd_lane = lane128 < LANE_ML
    a_neg = jnp.where(ssd_lane[0:1], -jnp.exp(alog_ref[...]), 0.0)
    dt_c = _softplus(pre_a)
    csum = _scan_rows(jnp.where(ssd_lane, dt_c * a_neg, log_f), jnp.add, 0.0)
    last = csum[L - 1:L]
    u_c = pre_a - csum
    m_intra = csum + _scan_rows(u_c, jnp.maximum, -jnp.inf)
    w_end = last - csum + pre_a
    m_loc = jnp.max(w_end, axis=0, keepdims=True)
    e_end = jnp.exp(w_end - m_loc)
    m_prev = mst[0:1]
    inter_log = csum + m_prev
    m_t = jnp.maximum(inter_log, m_intra)
    w_inter = jnp.exp(inter_log - m_t)
    a_col = csum - m_t
    en = jnp.exp(-m_t)
    m_new = jnp.maximum(last + m_prev, m_loc)
    a_prev = jnp.exp(last + m_prev - m_new)
    a_loc = jnp.exp(m_loc - m_new)
    rows_t = jnp.where(ssd_lane, csum, u_c).T

    def bcast(arr, j):
        return jnp.broadcast_to(arr[:, j:j + 1], (arr.shape[0], 128))

    def pair(arr, j0, k):
        return jnp.where(lo[:arr.shape[0]], bcast(arr, j0 + 2 * k), bcast(arr, j0 + 2 * k + 1))

    xbc = xbc_ref[0]
    xpad[8:8 + L, :] = xbc
    conv = convb_ref[...]
    for k in range(SSD_CONV):
        conv = conv + convw_ref[k:k + 1, :] * xpad[5 + k:5 + k + L, :]
    xpad[0:8, :] = xbc[L - 8:L, :]
    xc = conv * _sigmoid(conv)

    xs = xc[:, :SSD_WIDTH]
    bmat = xc[:, SSD_WIDTH:SSD_WIDTH + 256]
    cmat = xc[:, SSD_WIDTH + 256:]
    dfull = dfull_ref[...]
    cs_b = [bcast(csum, LANE_SSD + h) for h in range(SSD_HEADS)]
    xdt, xds, ecs, cdec = [], [], [], []
    for k in range(4):
        cs_p = jnp.where(lo, cs_b[2 * k], cs_b[2 * k + 1])
        last_p = pair(last, LANE_SSD, k)
        xdt.append(xs[:, 128 * k:128 * (k + 1)] * pair(dt_c, LANE_SSD, k))
        xds.append((xdt[k] * jnp.exp(last_p - cs_p)).astype(BF16))
        ecs.append(jnp.exp(cs_p))
        cdec.append(jnp.exp(last_p))

    y_slabs = []
    for g in range(SSD_GROUPS):
        bm_g = bmat[:, 128 * g:128 * (g + 1)]
        cm_b = cmat[:, 128 * g:128 * (g + 1)].astype(BF16)
        cb = _dot_nt(cm_b, bm_g.astype(BF16))
        s_loc = _dot(bm_g.T.astype(BF16), jnp.concatenate(xds[2 * g:2 * g + 2], axis=1))
        prev = prev_t[g]
        y_off = _dot(cm_b, prev.astype(BF16))
        for kk in range(2):
            k = 2 * g + kk
            slab = xdt[k].astype(BF16)
            halves = []
            for hl in range(2):
                h = 2 * k + hl
                diff = cs_b[h] - rows_t[LANE_SSD + h:LANE_SSD + h + 1, :]
                dec = jnp.where(tril, jnp.exp(jnp.where(tril, diff, 0.0)), 0.0)
                halves.append(_dot((cb * dec).astype(BF16), slab))
            y_diag = jnp.where(lo, halves[0], halves[1])
            y_slabs.append(y_diag + y_off[:, 128 * kk:128 * (kk + 1)] * ecs[k]
                           + xs[:, 128 * k:128 * (k + 1)] * dfull[:, 128 * k:128 * (k + 1)])
        prev_t[g] = prev * jnp.concatenate(cdec[2 * g:2 * g + 2], axis=1) + s_loc
    y = jnp.concatenate(y_slabs, axis=1)
    z = z_ref[0]
    yz = y * (z * _sigmoid(z))
    var = jnp.mean(yz * yz, axis=-1, keepdims=True)
    y_ssd = (yz * lax.rsqrt(var + NORM_EPS)) * ssdnw_ref[...]

    c1 = (DIFF_QK_DIM ** -0.5) * LOG2E
    lam = lam_ref[0]
    q = dq_ref[0] * c1
    for h in range(DIFF_HEADS):
        qs = q[:, 128 * (h // 2):128 * (h // 2 + 1)]
        base = 64 * (h % 2)
        q0 = jnp.where((lane128 >= base) & (lane128 < base + 32), qs, 0.0)
        q1 = jnp.where((lane128 >= base + 32) & (lane128 < base + 64), qs, 0.0)
        qbd[h] = jnp.concatenate([q0, q1], axis=0).astype(BF16)
    n_groups = c // KEY_GROUP + 1
    gw = KEY_GROUP * L

    def fill_scores(g, dst):
        r0 = pl.multiple_of(g * gw, gw)
        for h in range(DIFF_HEADS):
            keys = kb[pl.ds(r0, gw), 128 * (h // 2):128 * (h // 2 + 1)]
            add = jnp.concatenate(
                [bias[jnp.clip(c - KEY_GROUP * g - j, -1, 2) + 1, h] for j in range(KEY_GROUP)], axis=1)
            dst[h] = _dot_nt(qbd[h], keys) + jnp.concatenate([add, add], axis=0)

    def consume(g, src):
        r0 = pl.multiple_of(g * gw, gw)
        for h in range(DIFF_HEADS):
            s = src[h]
            t = s[:, :L]
            for j in range(1, KEY_GROUP):
                t = jnp.maximum(t, s[:, L * j:L * (j + 1)])
            m_old = rstat[h]
            m_new = jnp.maximum(m_old, jnp.max(t, axis=1, keepdims=True))
            p = jnp.concatenate([jnp.exp2(s[:, L * j:L * (j + 1)] - m_new) for j in range(KEY_GROUP)], axis=1)
            acc_s[h] = acc_s[h] * jnp.exp2(m_old - m_new) + _dot(p.astype(BF16), vb[h, pl.ds(r0, gw), :])
            rstat[h] = m_new

    rstat[...] = jnp.full(rstat.shape, MASKED, F32)
    acc_s[...] = jnp.zeros(acc_s.shape, F32)
    fill_scores(0, s_even)

    def key_step(g, carry):
        nxt = jnp.minimum(g + 1, n_blocks // KEY_GROUP - 1)

        @pl.when(g % 2 == 0)
        def _():
            fill_scores(nxt, s_odd)
            consume(g, s_even)

        @pl.when(g % 2 == 1)
        def _():
            fill_scores(nxt, s_even)
            consume(g, s_odd)

        return carry

    lax.fori_loop(0, n_groups, key_step, 0)
    d_slabs = []
    for k in range(2):
        halves = []
        for hl in range(2):
            h = 2 * k + hl
            acc = acc_s[h]
            r = acc * (1.0 / pltpu.roll(acc, 64, 1))
            halves.append(r[:L] - lam * r[L:])
        o = jnp.where(lo, halves[0], halves[1])
        sq = o * o
        s_lo = jnp.sum(jnp.where(lo, sq, 0.0), axis=1, keepdims=True)
        s_hi = jnp.sum(jnp.where(lo, 0.0, sq), axis=1, keepdims=True)
        ovar = jnp.where(lo, s_lo, s_hi) * (1.0 / DIFF_V_DIM)
        d_slabs.append(((o * lax.rsqrt(ovar + NORM_EPS)) * dnw_ref[:, 128 * k:128 * (k + 1)])
                       * (1.0 - lam_init))
    y_diff = jnp.concatenate(d_slabs, axis=1)

    bd_mask = row_lo == (col_i < 64)
    inv_sqrt_d = MLSTM_HEAD_DIM ** -0.5
    ones_b = jnp.ones((L, 128), BF16)
    mq = mq_ref[0]
    mk = mk_ref[0]
    mv = mv_ref[0]
    mo = mo_ref[0]
    mnw = mnw_ref[...]
    m_slabs = []
    for k in range(2):
        sl = slice(128 * k, 128 * (k + 1))
        qs = mq[:, sl]
        ks = mk[:, sl]
        vs = mv[:, sl]
        q_b = qs.astype(BF16)
        k_b = ks.astype(BF16)
        v1 = jnp.concatenate([vs.astype(BF16), ones_b], axis=1)
        intra = []
        for hl in range(2):
            h = 2 * k + hl
            qm = jnp.where(lo if hl == 0 else jnp.logical_not(lo), qs, 0.0).astype(BF16)
            qk = _dot_nt(qm, k_b)
            arg = bcast(a_col, LANE_ML + h) + rows_t[LANE_ML + h:LANE_ML + h + 1, :]
            w_intra = jnp.exp(jnp.where(tril, arg, -jnp.inf))
            intra.append(_dot(((qk * inv_sqrt_d) * w_intra).astype(BF16), v1))
        num_intra = jnp.where(lo, intra[0][:, :128], intra[1][:, :128])
        den_intra = jnp.where(lo, intra[0][:, 128:], intra[1][:, 128:])
        ct_prev = ct[k]
        nm_prev = nm[k]
        wi = pair(w_inter, LANE_ML, k)
        num = num_intra + wi * _dot(q_b, ct_prev.astype(BF16))
        den = den_intra + wi * _dot(q_b, nm_prev.astype(BF16))
        denom = jnp.maximum(jnp.abs(den), pair(en, LANE_ML, k))
        hs = num / denom
        sq = hs * hs
        s_lo = jnp.sum(jnp.where(lo, sq, 0.0), axis=1, keepdims=True)
        s_hi = jnp.sum(jnp.where(lo, 0.0, sq), axis=1, keepdims=True)
        hvar = jnp.where(lo, s_lo, s_hi) * (1.0 / MLSTM_HEAD_DIM)
        hn = (hs * lax.rsqrt(hvar + NORM_EPS)) * mnw[:, sl]
        m_slabs.append(_sigmoid(mo[:, sl]) * hn)
        e_p = pair(e_end, LANE_ML, k)
        loc = _dot((ks.T * inv_sqrt_d).astype(BF16),
                   jnp.concatenate([(vs * e_p).astype(BF16), e_p.astype(BF16)], axis=1))
        ap = pair(a_prev, LANE_ML, k)
        al = pair(a_loc, LANE_ML, k)
        ct[k] = ct_prev * ap + jnp.where(bd_mask, loc[:, :128], 0.0) * al
        nm[k] = nm_prev * ap + jnp.where(bd_mask, loc[:, 128:], 0.0) * al
    mst[...] = jnp.broadcast_to(m_new, mst.shape)
    y_ml = jnp.concatenate(m_slabs, axis=1)

    y_ref[0] = jnp.concatenate([y_ssd, y_diff, y_ml], axis=1).astype(BF16)


def _mixer(proj, lam, rel_bias, conv_w, conv_b, dt_bias, a_log, d_skip, ssd_norm_w,
           diff_norm_w, i_bias, f_bias, mlstm_norm_w, *, lam_init):
    bsz, s, _ = proj.shape
    L = CHUNK
    nb = s // L

    def tok(width, col0):
        return pl.BlockSpec((1, L, width), lambda b, c: (b, c, col0 // width))

    def seq(width, col0):
        return pl.BlockSpec((1, s, width), lambda b, c: (b, 0, col0 // width))

    def par(shape):
        return pl.BlockSpec(shape, lambda b, c: (0,) * len(shape))

    smem = pl.BlockSpec(memory_space=pltpu.SMEM)

    def lanes(*placed):
        row = jnp.zeros((128,), F32)
        for off, v in placed:
            row = row.at[off:off + v.shape[0]].set(v.astype(F32))
        return row.reshape(1, 128)

    params = [
        conv_w.astype(F32), conv_b.reshape(1, SSD_CONV_CH).astype(F32),
        lanes((LANE_SSD, dt_bias), (LANE_ML, i_bias)), lanes((LANE_ML, f_bias)), lanes((LANE_SSD, a_log)),
        jnp.repeat(d_skip.astype(F32), SSD_HEAD_DIM).reshape(1, SSD_WIDTH),
        ssd_norm_w.reshape(1, SSD_WIDTH).astype(F32),
        jnp.tile(diff_norm_w.astype(F32), DIFF_HEADS).reshape(1, DIFF_WIDTH),
        mlstm_norm_w.reshape(1, MLSTM_WIDTH).astype(F32),
    ]
    in_specs = [smem, smem,
                tok(1024, COL_XBC), tok(512, COL_Z), tok(256, COL_DQ),
                seq(256, COL_DK), seq(256, COL_DV),
                tok(256, COL_MQ), tok(256, COL_MK), tok(256, COL_MV), tok(256, COL_MO),
                tok(128, COL_SMALL_A), tok(128, COL_SMALL_B)] + [par(p.shape) for p in params]
    scratch = [
        pltpu.VMEM((L + 8, SSD_CONV_CH), F32),
        pltpu.VMEM((SSD_GROUPS, SSD_STATE, 256), F32),
        pltpu.VMEM((2, 128, 128), F32),
        pltpu.VMEM((2, 128, 128), F32),
        pltpu.VMEM((8, 128), F32),
        pltpu.VMEM((s, 256), BF16),
        pltpu.VMEM((DIFF_HEADS, s, 128), BF16),
        pltpu.VMEM((4, DIFF_HEADS, L, L), F32),
        pltpu.VMEM((DIFF_HEADS, 2 * L, 128), BF16),
        pltpu.VMEM((DIFF_HEADS, 2 * L, L), F32),
        pltpu.VMEM((DIFF_HEADS, 2 * L, 128), F32),
        pltpu.VMEM((DIFF_HEADS, 2 * L, KEY_GROUP * L), F32),
        pltpu.VMEM((DIFF_HEADS, 2 * L, KEY_GROUP * L), F32),
    ]
    return pl.pallas_call(
        functools.partial(_mixer_kernel, lam_init=lam_init, n_blocks=nb),
        grid=(bsz, nb),
        in_specs=in_specs,
        out_specs=pl.BlockSpec((1, L, MIX_WIDTH), lambda b, c: (b, c, 0)),
        out_shape=jax.ShapeDtypeStruct((bsz, s, MIX_WIDTH), BF16),
        scratch_shapes=scratch,
        compiler_params=pltpu.CompilerParams(dimension_semantics=("arbitrary", "arbitrary"),
                                             vmem_limit_bytes=V7X_VMEM_LIMIT),
        name="mixer",
    )(lam.reshape(1).astype(F32), rel_bias.astype(F32),
      *([proj] * 11), *params)


def _dense_kernel(h_ref, y_ref, p_ref, wo_ref, n2_ref, wg_ref, wu_ref, wd_ref, pg_ref, pp_ref, fn_ref,
                  o_ref, *, final, h_chunk):
    h1 = h_ref[...] + _dot(y_ref[...], wo_ref[...])
    var = jnp.mean(h1 * h1, axis=-1, keepdims=True)
    u = ((h1 * lax.rsqrt(var + NORM_EPS)) * n2_ref[...]).astype(BF16)
    ffn = jnp.zeros(h1.shape, F32)
    for c0 in range(0, FFN_HIDDEN, h_chunk):
        g = _dot(u, wg_ref[:, c0:c0 + h_chunk])
        up = _dot(u, wu_ref[:, c0:c0 + h_chunk])
        a = ((g * _sigmoid(g)) * up).astype(BF16)
        ffn = ffn + _dot(a, wd_ref[c0:c0 + h_chunk, :])
    h2 = h1 + ffn
    gate = _sigmoid(_dot(h2.astype(BF16), pg_ref[...]))
    h3 = h2 + gate * _dot(p_ref[...].astype(BF16), pp_ref[...])
    if final:
        fvar = jnp.mean(h3 * h3, axis=-1, keepdims=True)
        h3 = (h3 * lax.rsqrt(fvar + NORM_EPS)) * fn_ref[...]
    o_ref[...] = h3


def _dense(h2d, y2d, p2d, w_out, norm2_w, w_gate, w_up, w_down, ple_gate_w, ple_proj_w, final_norm_w,
           *, final, tm):
    t = h2d.shape[0]
    row = lambda width: pl.BlockSpec((tm, width), lambda i: (i, 0))
    res = lambda a: pl.BlockSpec(a.shape, lambda i: (0, 0), pipeline_mode=pl.Buffered(1))
    weights = [w_out.astype(BF16), norm2_w.reshape(1, D_MODEL).astype(F32), w_gate.astype(BF16),
               w_up.astype(BF16), w_down.astype(BF16), ple_gate_w.astype(BF16), ple_proj_w.astype(BF16),
               final_norm_w.reshape(1, D_MODEL).astype(F32)]
    return pl.pallas_call(
        functools.partial(_dense_kernel, final=final, h_chunk=FFN_CHUNK),
        grid=(t // tm,),
        in_specs=[row(D_MODEL), row(MIX_WIDTH), row(PLE_DIM)] + [res(w) for w in weights],
        out_specs=row(D_MODEL),
        out_shape=jax.ShapeDtypeStruct((t, D_MODEL), F32),
        compiler_params=pltpu.CompilerParams(dimension_semantics=("arbitrary",),
                                             vmem_limit_bytes=V7X_VMEM_LIMIT),
        name="dense",
    )(h2d, y2d, p2d, *weights)


def _pack_w_in(w):
    z, xbc, dt, dq, dk, dv, mq, mk, mv, mo, mi, mf = jnp.split(w, np.cumsum(IN_SPLITS)[:-1], axis=1)
    d = w.shape[0]
    zeros = lambda n: jnp.zeros((d, n), w.dtype)
    small_a = jnp.concatenate([dt, mi, zeros(128 - LANE_ML - MLSTM_HEADS)], axis=1)
    small_b = jnp.concatenate([zeros(LANE_ML), mf, zeros(128 - LANE_ML - MLSTM_HEADS)], axis=1)
    return jnp.concatenate([xbc, z, dq, dk, dv, mq, mk, mv, mo, small_a, small_b], axis=1).astype(BF16)


def kernel(x, p, norm1_w, w_in, ssd_conv_w, ssd_conv_b, ssd_dt_bias, ssd_a_log, ssd_d, ssd_norm_w, diff_lq1, diff_lk1, diff_lq2, diff_lk2, diff_norm_w, rel_bias, mlstm_i_bias, mlstm_f_bias, mlstm_norm_w, w_out, norm2_w, w_ffn_gate, w_ffn_up, w_ffn_down, ple_gate_w, ple_proj_w, final_norm_w):
    bsz, s, d = x.shape
    depth = w_in.shape[0]
    assert d == D_MODEL and s % (KEY_GROUP * CHUNK) == 0
    t = bsz * s
    tm = ROW_TILE if t % ROW_TILE == 0 else CHUNK
    h = x.reshape(t, d).astype(F32)
    for i in range(depth):
        proj = _in_proj(h, norm1_w[i], _pack_w_in(w_in[i]), tm=tm).reshape(bsz, s, PROJ_COLS)
        lam_init = 0.8 - 0.6 * math.exp(-0.3 * i)
        lam = (jnp.exp(jnp.sum(diff_lq1[i].astype(F32) * diff_lk1[i].astype(F32)))
               - jnp.exp(jnp.sum(diff_lq2[i].astype(F32) * diff_lk2[i].astype(F32))) + lam_init)
        y = _mixer(proj, lam, rel_bias, ssd_conv_w[i], ssd_conv_b[i], ssd_dt_bias[i], ssd_a_log[i],
                   ssd_d[i], ssd_norm_w[i], diff_norm_w[i], mlstm_i_bias[i], mlstm_f_bias[i],
                   mlstm_norm_w[i], lam_init=lam_init)
        h = _dense(h, y.reshape(t, MIX_WIDTH), p[i].reshape(t, PLE_DIM).astype(F32), w_out[i], norm2_w[i],
                   w_ffn_gate[i], w_ffn_up[i], w_ffn_down[i], ple_gate_w[i], ple_proj_w[i], final_norm_w,
                   final=(i == depth - 1), tm=tm)
    return h.reshape(bsz, s, d).astype(x.dtype)
```

```python
import functools
import math

import numpy as np
import jax
import jax.numpy as jnp
from jax import lax
from jax.experimental import pallas as pl
from jax.experimental.pallas import tpu as pltpu

F32 = jnp.float32
BF16 = jnp.bfloat16

D_MODEL = 1024
PLE_DIM = 256
NORM_EPS = 1e-6
CHUNK = 128
KEY_GROUP = 4

SSD_HEADS = 8
SSD_HEAD_DIM = 64
SSD_WIDTH = 512
SSD_GROUPS = 2
SSD_STATE = 128
SSD_CONV = 4
SSD_CONV_CH = 1024
DIFF_HEADS = 4
DIFF_QK_DIM = 32
DIFF_V_DIM = 64
DIFF_WIDTH = 256
REL_BUCKETS = 32
REL_MAX_DIST = 128
MLSTM_HEADS = 4
MLSTM_HEAD_DIM = 64
MLSTM_WIDTH = 256
MIX_WIDTH = 1024
FFN_HIDDEN = 2816
IN_SPLITS = (512, 1024, 8, 256, 256, 256, 256, 256, 256, 256, 4, 4)

F_Z, F_MO, F_SMALL_A, F_SMALL_B, F_COLS = 0, 512, 768, 896, 1024
B_DQ, B_DK, B_DV, B_MQ, B_MK, B_MV, B_COLS = 0, 256, 512, 768, 1024, 1280, 1536
W_XBC, W_ZMO, W_SMALL, W_QKV = 0, 1024, 1792, 2048
PROJ_COLS = W_QKV + B_COLS
LANE_SSD, LANE_ML = 0, 8
ATTN_C1 = (DIFF_QK_DIM ** -0.5) * math.log2(math.e)

ROW_TILE = 512
FFN_CHUNK = 1408

V7X_VMEM_LIMIT = 56 * 1024 * 1024
MASKED = -1e30
LOG2E = math.log2(math.e)


def _t5_bucket_lower_bounds():
    max_exact = REL_BUCKETS // 2
    d = np.arange(0, 2 * CHUNK, dtype=np.int64)
    df = np.maximum(d, 1).astype(np.float32)
    large = max_exact + (np.log(df / np.float32(max_exact)) / np.float32(math.log(REL_MAX_DIST / max_exact))
                         * np.float32(REL_BUCKETS - max_exact)).astype(np.int32)
    large = np.minimum(large, REL_BUCKETS - 1)
    bucket = np.where(d < max_exact, d, large)
    assert np.all(np.diff(bucket) >= 0) and bucket[CHUNK] == REL_BUCKETS - 1
    return [int(np.argmax(bucket >= b)) for b in range(REL_BUCKETS)]


_BUCKET_LO = _t5_bucket_lower_bounds()


def _sigmoid(x):
    return 1.0 / (1.0 + jnp.exp(-x))


def _softplus(x):
    return jnp.maximum(x, 0.0) + jnp.log(1.0 + jnp.exp(-jnp.abs(x)))


def _dot(a, b):
    return jnp.dot(a, b, preferred_element_type=F32)


def _dot_nt(a, b):
    return lax.dot_general(a, b, (((1,), (1,)), ((), ())), preferred_element_type=F32)


def _scan_rows(x, combine, identity):
    sub = lax.broadcasted_iota(jnp.int32, (8, x.shape[1]), 0)
    tiles, carry = [], None
    for i in range(x.shape[0] // 8):
        t = x[8 * i:8 * (i + 1)]
        for sh in (1, 2, 4):
            t = combine(t, jnp.where(sub >= sh, pltpu.roll(t, sh, 0), identity))
        if carry is not None:
            t = combine(t, carry)
        carry = t[7:8]
        tiles.append(t)
    return jnp.concatenate(tiles, axis=0)


def _in_proj_kernel(x_ref, nw_ref, w_ref, convw_ref, convb_ref, ox_ref, of_ref, ob_ref, xpad, *, tiles_per_seq):
    tm = x_ref.shape[0]
    i = pl.program_id(0)

    @pl.when(i == 0)
    def _():
        xpad[...] = jnp.zeros(xpad.shape, F32)

    @pl.when(lax.rem(i + tiles_per_seq - 1, tiles_per_seq) == 0)
    def _():
        xpad[0:8, :] = jnp.zeros((8, SSD_CONV_CH), F32)

    conv = convb_ref[...]
    for k in range(SSD_CONV):
        conv = conv + convw_ref[k:k + 1, :] * xpad[5 + k:5 + k + tm, :]
    ox_ref[...] = conv * _sigmoid(conv)
    xpad[0:8, :] = xpad[tm:tm + 8, :]

    x = x_ref[...]
    var = jnp.mean(x * x, axis=-1, keepdims=True)
    u = ((x * lax.rsqrt(var + NORM_EPS)) * nw_ref[...]).astype(BF16)
    xpad[8:8 + tm, :] = _dot(u, w_ref[:, W_XBC:W_ZMO])

    zmo = _dot(u, w_ref[:, W_ZMO:W_SMALL])
    z = zmo[:, :SSD_WIDTH]
    of_ref[:, F_Z:F_MO] = z * _sigmoid(z)
    of_ref[:, F_MO:F_SMALL_A] = _sigmoid(zmo[:, SSD_WIDTH:])
    of_ref[:, F_SMALL_A:F_COLS] = _dot(u, w_ref[:, W_SMALL:W_QKV])

    qkv = _dot(u, w_ref[:, W_QKV:])
    ob_ref[:, B_DQ:B_DK] = (qkv[:, B_DQ:B_DK] * ATTN_C1).astype(BF16)
    ob_ref[:, B_DK:] = qkv[:, B_DK:].astype(BF16)


def _in_proj(h2d, norm_w, w_packed, conv_w, conv_b, *, tm, seq_len):
    t = h2d.shape[0]
    n = t // tm
    const = lambda i: (0, 0)
    cur = lambda width: pl.BlockSpec((tm, width), lambda i: (jnp.minimum(i, n - 1), 0))
    return pl.pallas_call(
        functools.partial(_in_proj_kernel, tiles_per_seq=seq_len // tm),
        grid=(n + 1,),
        in_specs=[cur(D_MODEL),
                  pl.BlockSpec((1, D_MODEL), const),
                  pl.BlockSpec((D_MODEL, PROJ_COLS), const, pipeline_mode=pl.Buffered(1)),
                  pl.BlockSpec((SSD_CONV, SSD_CONV_CH), const),
                  pl.BlockSpec((1, SSD_CONV_CH), const)],
        out_specs=[pl.BlockSpec((tm, SSD_CONV_CH), lambda i: (jnp.maximum(i - 1, 0), 0)),
                   cur(F_COLS), cur(B_COLS)],
        out_shape=[jax.ShapeDtypeStruct((t, SSD_CONV_CH), F32), jax.ShapeDtypeStruct((t, F_COLS), F32),
                   jax.ShapeDtypeStruct((t, B_COLS), BF16)],
        scratch_shapes=[pltpu.VMEM((tm + 8, SSD_CONV_CH), F32)],
        compiler_params=pltpu.CompilerParams(dimension_semantics=("arbitrary",),
                                             vmem_limit_bytes=V7X_VMEM_LIMIT),
        name="in_proj",
    )(h2d, norm_w.reshape(1, D_MODEL), w_packed, conv_w.astype(F32), conv_b.reshape(1, SSD_CONV_CH).astype(F32))


def _mixer_kernel(lam_ref, relb_ref,
                  xbc_ref, z_ref, mo_ref, sma_ref, smb_ref, dq_ref, dk_ref, dv_ref, mq_ref, mk_ref, mv_ref,
                  biasa_ref, biasb_ref, alog_ref, dfull_ref, ssdnw_ref, dnw_ref, mnw_ref,
                  y_ref,
                  prev_t, ct, nm, mst, vb, bias, qbd, rstat, acc_s, s_even, s_odd,
                  *, lam_init, n_blocks):
    L = CHUNK
    b = pl.program_id(0)
    c = pl.program_id(1)

    lane128 = lax.broadcasted_iota(jnp.int32, (L, 128), 1)
    lo = lane128 < 64
    row_i = lax.broadcasted_iota(jnp.int32, (L, L), 0)
    col_i = lax.broadcasted_iota(jnp.int32, (L, L), 1)
    tril = row_i >= col_i
    row_lo = row_i < 64

    @pl.when(jnp.logical_and(b == 0, c == 0))
    def _():
        for h in range(DIFF_HEADS):
            bias[0, h] = jnp.full((L, L), MASKED, F32)
            bias[3, h] = jnp.zeros((L, L), F32)
            far = relb_ref[REL_BUCKETS - 1, h]
            for which, off in ((1, 0), (2, L)):
                dist = row_i - col_i + off
                v = jnp.full((L, L), far, F32)
                for bkt in range(REL_BUCKETS - 2, -1, -1):
                    v = jnp.where(dist < _BUCKET_LO[bkt + 1], relb_ref[bkt, h], v)
                bias[which, h] = jnp.where(dist >= 0, (v - far) * LOG2E, MASKED)

    @pl.when(c == 0)
    def _():
        prev_t[...] = jnp.zeros(prev_t.shape, F32)
        ct[...] = jnp.zeros(ct.shape, F32)
        nm[...] = jnp.zeros(nm.shape, F32)
        mst[...] = jnp.zeros(mst.shape, F32)

        def fill(t, carry):
            r0 = pl.multiple_of(t * L, L)
            v = dv_ref[0, pl.ds(r0, L), :]
            for h in range(DIFF_HEADS):
                own = lo if h % 2 == 0 else jnp.logical_not(lo)
                vb[h, pl.ds(r0, L), :] = jnp.where(own, v[:, 128 * (h // 2):128 * (h // 2 + 1)],
                                                   jnp.ones((), BF16))
            return carry

        lax.fori_loop(0, n_blocks, fill, 0)

    pre_a = sma_ref[0] + biasa_ref[...]
    log_f = -_softplus(-(smb_ref[0] + biasb_ref[...]))
    ssd_lane = lane128 < LANE_ML
    a_neg = jnp.where(ssd_lane[0:1], -jnp.exp(alog_ref[...]), 0.0)
    dt_c = _softplus(pre_a)
    csum = _scan_rows(jnp.where(ssd_lane, dt_c * a_neg, log_f), jnp.add, 0.0)
    last = csum[L - 1:L]
    u_c = pre_a - csum
    m_intra = csum + _scan_rows(u_c, jnp.maximum, -jnp.inf)
    w_end = last - csum + pre_a
    m_loc = jnp.max(w_end, axis=0, keepdims=True)
    e_end = jnp.exp(w_end - m_loc)
    m_prev = mst[0:1]
    inter_log = csum + m_prev
    m_t = jnp.maximum(inter_log, m_intra)
    w_inter = jnp.exp(inter_log - m_t)
    a_col = csum - m_t
    en = jnp.exp(-m_t)
    m_new = jnp.maximum(last + m_prev, m_loc)
    a_prev = jnp.exp(last + m_prev - m_new)
    a_loc = jnp.exp(m_loc - m_new)
    rows_t = jnp.where(ssd_lane, csum, u_c).T

    def bcast(arr, j):
        return jnp.broadcast_to(arr[:, j:j + 1], (arr.shape[0], 128))

    def pair(arr, j0, k):
        return jnp.where(lo[:arr.shape[0]], bcast(arr, j0 + 2 * k), bcast(arr, j0 + 2 * k + 1))

    xc = xbc_ref[0]
    xs = xc[:, :SSD_WIDTH]
    bmat = xc[:, SSD_WIDTH:SSD_WIDTH + 256]
    cmat = xc[:, SSD_WIDTH + 256:]
    dfull = dfull_ref[...]
    cs_b = [bcast(csum, LANE_SSD + h) for h in range(SSD_HEADS)]
    xdt, xds, ecs, cdec = [], [], [], []
    for k in range(4):
        cs_p = jnp.where(lo, cs_b[2 * k], cs_b[2 * k + 1])
        last_p = pair(last, LANE_SSD, k)
        xdt.append(xs[:, 128 * k:128 * (k + 1)] * pair(dt_c, LANE_SSD, k))
        xds.append((xdt[k] * jnp.exp(last_p - cs_p)).astype(BF16))
        ecs.append(jnp.exp(cs_p))
        cdec.append(jnp.exp(last_p))

    y_slabs = []
    for g in range(SSD_GROUPS):
        bm_g = bmat[:, 128 * g:128 * (g + 1)]
        cm_b = cmat[:, 128 * g:128 * (g + 1)].astype(BF16)
        cb = _dot_nt(cm_b, bm_g.astype(BF16))
        s_loc = _dot(bm_g.T.astype(BF16), jnp.concatenate(xds[2 * g:2 * g + 2], axis=1))
        prev = prev_t[g]
        y_off = _dot(cm_b, prev.astype(BF16))
        for kk in range(2):
            k = 2 * g + kk
            slab = xdt[k].astype(BF16)
            halves = []
            for hl in range(2):
                h = 2 * k + hl
                diff = cs_b[h] - rows_t[LANE_SSD + h:LANE_SSD + h + 1, :]
                dec = jnp.where(tril, jnp.exp(jnp.where(tril, diff, 0.0)), 0.0)
                halves.append(_dot((cb * dec).astype(BF16), slab))
            y_diag = jnp.where(lo, halves[0], halves[1])
            y_slabs.append(y_diag + y_off[:, 128 * kk:128 * (kk + 1)] * ecs[k]
                           + xs[:, 128 * k:128 * (k + 1)] * dfull[:, 128 * k:128 * (k + 1)])
        prev_t[g] = prev * jnp.concatenate(cdec[2 * g:2 * g + 2], axis=1) + s_loc
    y = jnp.concatenate(y_slabs, axis=1)
    yz = y * z_ref[0]
    var = jnp.mean(yz * yz, axis=-1, keepdims=True)
    y_ssd = (yz * lax.rsqrt(var + NORM_EPS)) * ssdnw_ref[...]

    bd_mask = row_lo == (col_i < 64)
    inv_sqrt_d = MLSTM_HEAD_DIM ** -0.5
    ones_b = jnp.ones((L, 128), BF16)
    mq = mq_ref[0]
    mk = mk_ref[0]
    mv = mv_ref[0]
    mo = mo_ref[0]
    mnw = mnw_ref[...]
    m_slabs = []
    for k in range(2):
        sl = slice(128 * k, 128 * (k + 1))
        q_b = mq[:, sl]
        k_b = mk[:, sl]
        v_b = mv[:, sl]
        v1 = jnp.concatenate([v_b, ones_b], axis=1)
        intra = []
        for hl in range(2):
            h = 2 * k + hl
            qm = jnp.where(lo if hl == 0 else jnp.logical_not(lo), q_b, jnp.zeros((), BF16))
            qk = _dot_nt(qm, k_b)
            arg = bcast(a_col, LANE_ML + h) + rows_t[LANE_ML + h:LANE_ML + h + 1, :]
            w_intra = jnp.exp(jnp.where(tril, arg, -jnp.inf))
            intra.append(_dot(((qk * inv_sqrt_d) * w_intra).astype(BF16), v1))
        num_intra = jnp.where(lo, intra[0][:, :128], intra[1][:, :128])
        den_intra = jnp.where(lo, intra[0][:, 128:], intra[1][:, 128:])
        ct_prev = ct[k]
        nm_prev = nm[k]
        wi = pair(w_inter, LANE_ML, k)
        num = num_intra + wi * _dot(q_b, ct_prev.astype(BF16))
        den = den_intra + wi * _dot(q_b, nm_prev.astype(BF16))
        denom = jnp.maximum(jnp.abs(den), pair(en, LANE_ML, k))
        hs = num / denom
        sq = hs * hs
        s_lo = jnp.sum(jnp.where(lo, sq, 0.0), axis=1, keepdims=True)
        s_hi = jnp.sum(jnp.where(lo, 0.0, sq), axis=1, keepdims=True)
        hvar = jnp.where(lo, s_lo, s_hi) * (1.0 / MLSTM_HEAD_DIM)
        hn = (hs * lax.rsqrt(hvar + NORM_EPS)) * mnw[:, sl]
        m_slabs.append(mo[:, sl] * hn)
        e_p = pair(e_end, LANE_ML, k)
        loc = _dot((k_b.astype(F32).T * inv_sqrt_d).astype(BF16),
                   jnp.concatenate([(v_b.astype(F32) * e_p).astype(BF16), e_p.astype(BF16)], axis=1))
        ap = pair(a_prev, LANE_ML, k)
        al = pair(a_loc, LANE_ML, k)
        ct[k] = ct_prev * ap + jnp.where(bd_mask, loc[:, :128], 0.0) * al
        nm[k] = nm_prev * ap + jnp.where(bd_mask, loc[:, 128:], 0.0) * al
    mst[...] = jnp.broadcast_to(m_new, mst.shape)
    y_ml = jnp.concatenate(m_slabs, axis=1)

    lam = lam_ref[0]
    q = dq_ref[0]
    zero_b = jnp.zeros((), BF16)
    for h in range(DIFF_HEADS):
        qs = q[:, 128 * (h // 2):128 * (h // 2 + 1)]
        base = 64 * (h % 2)
        q0 = jnp.where((lane128 >= base) & (lane128 < base + 32), qs, zero_b)
        q1 = jnp.where((lane128 >= base + 32) & (lane128 < base + 64), qs, zero_b)
        qbd[h] = jnp.concatenate([q0, q1], axis=0)
    n_groups = c // KEY_GROUP + 1
    gw = KEY_GROUP * L

    def fill_scores(g, dst):
        r0 = pl.multiple_of(g * gw, gw)
        for h in range(DIFF_HEADS):
            keys = dk_ref[0, pl.ds(r0, gw), 128 * (h // 2):128 * (h // 2 + 1)]
            add = jnp.concatenate(
                [bias[jnp.clip(c - KEY_GROUP * g - j, -1, 2) + 1, h] for j in range(KEY_GROUP)], axis=1)
            dst[h] = _dot_nt(qbd[h], keys) + jnp.concatenate([add, add], axis=0)

    def consume(g, src):
        r0 = pl.multiple_of(g * gw, gw)
        for h in range(DIFF_HEADS):
            s = src[h]
            t = s[:, :L]
            for j in range(1, KEY_GROUP):
                t = jnp.maximum(t, s[:, L * j:L * (j + 1)])
            m_old = rstat[h]
            m_new = jnp.maximum(m_old, jnp.max(t, axis=1, keepdims=True))
            p = jnp.concatenate([jnp.exp2(s[:, L * j:L * (j + 1)] - m_new) for j in range(KEY_GROUP)], axis=1)
            acc_s[h] = acc_s[h] * jnp.exp2(m_old - m_new) + _dot(p.astype(BF16), vb[h, pl.ds(r0, gw), :])
            rstat[h] = m_new

    rstat[...] = jnp.full(rstat.shape, MASKED, F32)
    acc_s[...] = jnp.zeros(acc_s.shape, F32)
    fill_scores(0, s_even)

    def key_step(g, carry):
        @pl.when(g % 2 == 0)
        def _():
            fill_scores(g + 1, s_odd)
            consume(g, s_even)

        @pl.when(g % 2 == 1)
        def _():
            fill_scores(g + 1, s_even)
            consume(g, s_odd)

        return carry

    g_last = n_groups - 1
    lax.fori_loop(0, g_last, key_step, 0)

    @pl.when(g_last % 2 == 0)
    def _():
        consume(g_last, s_even)

    @pl.when(g_last % 2 == 1)
    def _():
        consume(g_last, s_odd)

    d_slabs = []
    for k in range(2):
        halves = []
        for hl in range(2):
            h = 2 * k + hl
            acc = acc_s[h]
            r = acc * (1.0 / pltpu.roll(acc, 64, 1))
            halves.append(r[:L] - lam * r[L:])
        o = jnp.where(lo, halves[0], halves[1])
        sq = o * o
        s_lo = jnp.sum(jnp.where(lo, sq, 0.0), axis=1, keepdims=True)
        s_hi = jnp.sum(jnp.where(lo, 0.0, sq), axis=1, keepdims=True)
        ovar = jnp.where(lo, s_lo, s_hi) * (1.0 / DIFF_V_DIM)
        d_slabs.append(((o * lax.rsqrt(ovar + NORM_EPS)) * dnw_ref[:, 128 * k:128 * (k + 1)])
                       * (1.0 - lam_init))
    y_diff = jnp.concatenate(d_slabs, axis=1)

    y_ref[0] = jnp.concatenate([y_ssd, y_diff, y_ml], axis=1).astype(BF16)


def _mixer(px, pf, pb, lam, rel_bias, dt_bias, a_log, d_skip, ssd_norm_w,
           diff_norm_w, i_bias, f_bias, mlstm_norm_w, *, lam_init):
    bsz, s, _ = pf.shape
    L = CHUNK
    nb = s // L

    def tok(width, col0):
        return pl.BlockSpec((1, L, width), lambda b, c: (b, c, col0 // width))

    def seq(width, col0):
        return pl.BlockSpec((1, s, width), lambda b, c: (b, 0, col0 // width))

    def par(shape):
        return pl.BlockSpec(shape, lambda b, c: (0,) * len(shape))

    smem = pl.BlockSpec(memory_space=pltpu.SMEM)

    def lanes(*placed):
        row = jnp.zeros((128,), F32)
        for off, v in placed:
            row = row.at[off:off + v.shape[0]].set(v.astype(F32))
        return row.reshape(1, 128)

    params = [
        lanes((LANE_SSD, dt_bias), (LANE_ML, i_bias)), lanes((LANE_ML, f_bias)), lanes((LANE_SSD, a_log)),
        jnp.repeat(d_skip.astype(F32), SSD_HEAD_DIM).reshape(1, SSD_WIDTH),
        ssd_norm_w.reshape(1, SSD_WIDTH).astype(F32),
        jnp.tile(diff_norm_w.astype(F32), DIFF_HEADS).reshape(1, DIFF_WIDTH),
        mlstm_norm_w.reshape(1, MLSTM_WIDTH).astype(F32),
    ]
    in_specs = [smem, smem,
                tok(1024, 0), tok(512, F_Z), tok(256, F_MO), tok(128, F_SMALL_A), tok(128, F_SMALL_B),
                tok(256, B_DQ), seq(256, B_DK), seq(256, B_DV), tok(256, B_MQ), tok(256, B_MK), tok(256, B_MV),
                ] + [par(p.shape) for p in params]
    scratch = [
        pltpu.VMEM((SSD_GROUPS, SSD_STATE, 256), F32),
        pltpu.VMEM((2, 128, 128), F32),
        pltpu.VMEM((2, 128, 128), F32),
        pltpu.VMEM((8, 128), F32),
        pltpu.VMEM((DIFF_HEADS, s, 128), BF16),
        pltpu.VMEM((4, DIFF_HEADS, L, L), F32),
        pltpu.VMEM((DIFF_HEADS, 2 * L, 128), BF16),
        pltpu.VMEM((DIFF_HEADS, 2 * L, L), F32),
        pltpu.VMEM((DIFF_HEADS, 2 * L, 128), F32),
        pltpu.VMEM((DIFF_HEADS, 2 * L, KEY_GROUP * L), F32),
        pltpu.VMEM((DIFF_HEADS, 2 * L, KEY_GROUP * L), F32),
    ]
    return pl.pallas_call(
        functools.partial(_mixer_kernel, lam_init=lam_init, n_blocks=nb),
        grid=(bsz, nb),
        in_specs=in_specs,
        out_specs=pl.BlockSpec((1, L, MIX_WIDTH), lambda b, c: (b, c, 0)),
        out_shape=jax.ShapeDtypeStruct((bsz, s, MIX_WIDTH), BF16),
        scratch_shapes=scratch,
        compiler_params=pltpu.CompilerParams(dimension_semantics=("arbitrary", "arbitrary"),
                                             vmem_limit_bytes=V7X_VMEM_LIMIT),
        name="mixer",
    )(lam.reshape(1).astype(F32), rel_bias.astype(F32),
      px, *([pf] * 4), *([pb] * 6), *params)


def _dense_kernel(h_ref, y_ref, p_ref, wo_ref, n2_ref, wg_ref, wu_ref, wd_ref, pg_ref, pp_ref, fn_ref,
                  o_ref, *, final, h_chunk):
    h1 = h_ref[...] + _dot(y_ref[...], wo_ref[...])
    var = jnp.mean(h1 * h1, axis=-1, keepdims=True)
    u = ((h1 * lax.rsqrt(var + NORM_EPS)) * n2_ref[...]).astype(BF16)
    ffn = jnp.zeros(h1.shape, F32)
    for c0 in range(0, FFN_HIDDEN, h_chunk):
        g = _dot(u, wg_ref[:, c0:c0 + h_chunk])
        up = _dot(u, wu_ref[:, c0:c0 + h_chunk])
        a = ((g * _sigmoid(g)) * up).astype(BF16)
        ffn = ffn + _dot(a, wd_ref[c0:c0 + h_chunk, :])
    h2 = h1 + ffn
    gate = _sigmoid(_dot(h2.astype(BF16), pg_ref[...]))
    h3 = h2 + gate * _dot(p_ref[...].astype(BF16), pp_ref[...])
    if final:
        fvar = jnp.mean(h3 * h3, axis=-1, keepdims=True)
        h3 = (h3 * lax.rsqrt(fvar + NORM_EPS)) * fn_ref[...]
    o_ref[...] = h3


def _dense(h2d, y2d, p2d, w_out, norm2_w, w_gate, w_up, w_down, ple_gate_w, ple_proj_w, final_norm_w,
           *, final, tm):
    t = h2d.shape[0]
    row = lambda width: pl.BlockSpec((tm, width), lambda i: (i, 0))
    res = lambda a: pl.BlockSpec(a.shape, lambda i: (0, 0), pipeline_mode=pl.Buffered(1))
    weights = [w_out.astype(BF16), norm2_w.reshape(1, D_MODEL).astype(F32), w_gate.astype(BF16),
               w_up.astype(BF16), w_down.astype(BF16), ple_gate_w.astype(BF16), ple_proj_w.astype(BF16),
               final_norm_w.reshape(1, D_MODEL).astype(F32)]
    return pl.pallas_call(
        functools.partial(_dense_kernel, final=final, h_chunk=FFN_CHUNK),
        grid=(t // tm,),
        in_specs=[row(D_MODEL), row(MIX_WIDTH), row(PLE_DIM)] + [res(w) for w in weights],
        out_specs=row(D_MODEL),
        out_shape=jax.ShapeDtypeStruct((t, D_MODEL), F32),
        compiler_params=pltpu.CompilerParams(dimension_semantics=("arbitrary",),
                                             vmem_limit_bytes=V7X_VMEM_LIMIT),
        name="dense",
    )(h2d, y2d, p2d, *weights)


def _pack_w_in(w):
    z, xbc, dt, dq, dk, dv, mq, mk, mv, mo, mi, mf = jnp.split(w, np.cumsum(IN_SPLITS)[:-1], axis=1)
    d = w.shape[0]
    zeros = lambda n: jnp.zeros((d, n), w.dtype)
    small_a = jnp.concatenate([dt, mi, zeros(128 - LANE_ML - MLSTM_HEADS)], axis=1)
    small_b = jnp.concatenate([zeros(LANE_ML), mf, zeros(128 - LANE_ML - MLSTM_HEADS)], axis=1)
    return jnp.concatenate([xbc, z, mo, small_a, small_b, dq, dk, dv, mq, mk, mv], axis=1).astype(BF16)


def kernel(x, p, norm1_w, w_in, ssd_conv_w, ssd_conv_b, ssd_dt_bias, ssd_a_log, ssd_d, ssd_norm_w, diff_lq1, diff_lk1, diff_lq2, diff_lk2, diff_norm_w, rel_bias, mlstm_i_bias, mlstm_f_bias, mlstm_norm_w, w_out, norm2_w, w_ffn_gate, w_ffn_up, w_ffn_down, ple_gate_w, ple_proj_w, final_norm_w):
    bsz, s, d = x.shape
    depth = w_in.shape[0]
    assert d == D_MODEL and s % (KEY_GROUP * CHUNK) == 0
    t = bsz * s
    tm = ROW_TILE if s % ROW_TILE == 0 else CHUNK
    h = x.reshape(t, d).astype(F32)
    for i in range(depth):
        px, pf, pb = _in_proj(h, norm1_w[i], _pack_w_in(w_in[i]), ssd_conv_w[i], ssd_conv_b[i], tm=tm, seq_len=s)
        lam_init = 0.8 - 0.6 * math.exp(-0.3 * i)
        lam = (jnp.exp(jnp.sum(diff_lq1[i].astype(F32) * diff_lk1[i].astype(F32)))
               - jnp.exp(jnp.sum(diff_lq2[i].astype(F32) * diff_lk2[i].astype(F32))) + lam_init)
        y = _mixer(px.reshape(bsz, s, SSD_CONV_CH), pf.reshape(bsz, s, F_COLS), pb.reshape(bsz, s, B_COLS), lam,
                   rel_bias, ssd_dt_bias[i],
                   ssd_a_log[i], ssd_d[i], ssd_norm_w[i], diff_norm_w[i], mlstm_i_bias[i], mlstm_f_bias[i],
                   mlstm_norm_w[i], lam_init=lam_init)
        h = _dense(h, y.reshape(t, MIX_WIDTH), p[i].reshape(t, PLE_DIM).astype(F32), w_out[i], norm2_w[i],
                   w_ffn_gate[i], w_ffn_up[i], w_ffn_down[i], ple_gate_w[i], ple_proj_w[i], final_norm_w,
                   final=(i == depth - 1), tm=tm)
    return h.reshape(bsz, s, d).astype(x.dtype)
```

```python
import functools
import math

import numpy as np
import jax
import jax.numpy as jnp
from jax import lax
from jax.experimental import pallas as pl
from jax.experimental.pallas import tpu as pltpu

F32 = jnp.float32
BF16 = jnp.bfloat16

D_MODEL = 1024
PLE_DIM = 256
NORM_EPS = 1e-6
CHUNK = 128
KEY_GROUP = 4

SSD_HEADS = 8
SSD_HEAD_DIM = 64
SSD_WIDTH = 512
SSD_GROUPS = 2
SSD_STATE = 128
SSD_CONV = 4
SSD_CONV_CH = 1024
DIFF_HEADS = 4
DIFF_QK_DIM = 32
DIFF_V_DIM = 64
DIFF_WIDTH = 256
REL_BUCKETS = 32
REL_MAX_DIST = 128
MLSTM_HEADS = 4
MLSTM_HEAD_DIM = 64
MLSTM_WIDTH = 256
MIX_WIDTH = 1024
FFN_HIDDEN = 2816
IN_SPLITS = (512, 1024, 8, 256, 256, 256, 256, 256, 256, 256, 4, 4)

F_Z, F_MO, F_SMALL_A, F_SMALL_B, F_COLS = 0, 512, 768, 896, 1024
B_DQ, B_DK, B_DV, B_MQ, B_MK, B_MV, B_COLS = 0, 256, 512, 768, 1024, 1280, 1536
W_XBC, W_ZMO, W_SMALL, W_QKV = 0, 1024, 1792, 2048
PROJ_COLS = W_QKV + B_COLS
LANE_SSD, LANE_ML = 0, 8
ATTN_C1 = (DIFF_QK_DIM ** -0.5) * math.log2(math.e)

ROW_TILE = 512
FFN_CHUNK = 1408

V7X_VMEM_LIMIT = 56 * 1024 * 1024
MASKED = -1e30
LOG2E = math.log2(math.e)


def _t5_bucket_lower_bounds():
    max_exact = REL_BUCKETS // 2
    d = np.arange(0, 2 * CHUNK, dtype=np.int64)
    df = np.maximum(d, 1).astype(np.float32)
    large = max_exact + (np.log(df / np.float32(max_exact)) / np.float32(math.log(REL_MAX_DIST / max_exact))
                         * np.float32(REL_BUCKETS - max_exact)).astype(np.int32)
    large = np.minimum(large, REL_BUCKETS - 1)
    bucket = np.where(d < max_exact, d, large)
    assert np.all(np.diff(bucket) >= 0) and bucket[CHUNK] == REL_BUCKETS - 1
    return [int(np.argmax(bucket >= b)) for b in range(REL_BUCKETS)]


_BUCKET_LO = _t5_bucket_lower_bounds()


def _sigmoid(x):
    return 1.0 / (1.0 + jnp.exp(-x))


def _softplus(x):
    return jnp.maximum(x, 0.0) + jnp.log(1.0 + jnp.exp(-jnp.abs(x)))


def _dot(a, b):
    return jnp.dot(a, b, preferred_element_type=F32)


def _dot_nt(a, b):
    return lax.dot_general(a, b, (((1,), (1,)), ((), ())), preferred_element_type=F32)


def _dot_tn(a, b):
    return lax.dot_general(a, b, (((0,), (0,)), ((), ())), preferred_element_type=F32)


def _scan_rows(x, combine, identity):
    sub = lax.broadcasted_iota(jnp.int32, (8, x.shape[1]), 0)
    tiles, carry = [], None
    for i in range(x.shape[0] // 8):
        t = x[8 * i:8 * (i + 1)]
        for sh in (1, 2, 4):
            t = combine(t, jnp.where(sub >= sh, pltpu.roll(t, sh, 0), identity))
        if carry is not None:
            t = combine(t, carry)
        carry = t[7:8]
        tiles.append(t)
    return jnp.concatenate(tiles, axis=0)


def _in_proj_kernel(x_ref, nw_ref, w_ref, convw_ref, convb_ref, ox_ref, of_ref, ob_ref, xpad, *, tiles_per_seq):
    tm = x_ref.shape[0]
    i = pl.program_id(0)

    @pl.when(i == 0)
    def _():
        xpad[...] = jnp.zeros(xpad.shape, F32)

    @pl.when(lax.rem(i + tiles_per_seq - 1, tiles_per_seq) == 0)
    def _():
        xpad[0:8, :] = jnp.zeros((8, SSD_CONV_CH), F32)

    conv = convb_ref[...]
    for k in range(SSD_CONV):
        conv = conv + convw_ref[k:k + 1, :] * xpad[5 + k:5 + k + tm, :]
    ox_ref[...] = conv * _sigmoid(conv)
    xpad[0:8, :] = xpad[tm:tm + 8, :]

    x = x_ref[...]
    var = jnp.mean(x * x, axis=-1, keepdims=True)
    u = ((x * lax.rsqrt(var + NORM_EPS)) * nw_ref[...]).astype(BF16)
    xpad[8:8 + tm, :] = _dot(u, w_ref[:, W_XBC:W_ZMO])

    zmo = _dot(u, w_ref[:, W_ZMO:W_SMALL])
    z = zmo[:, :SSD_WIDTH]
    of_ref[:, F_Z:F_MO] = z * _sigmoid(z)
    of_ref[:, F_MO:F_SMALL_A] = _sigmoid(zmo[:, SSD_WIDTH:])
    of_ref[:, F_SMALL_A:F_COLS] = _dot(u, w_ref[:, W_SMALL:W_QKV])

    qkv = _dot(u, w_ref[:, W_QKV:])
    ob_ref[:, B_DQ:B_DK] = (qkv[:, B_DQ:B_DK] * ATTN_C1).astype(BF16)
    ob_ref[:, B_DK:] = qkv[:, B_DK:].astype(BF16)


def _in_proj(h2d, norm_w, w_packed, conv_w, conv_b, *, layer, tm, seq_len):
    t = h2d.shape[0]
    n = t // tm
    par = lambda a: pl.BlockSpec((None,) + a.shape[1:], lambda i: (layer, 0, 0))
    cur = lambda width: pl.BlockSpec((tm, width), lambda i: (jnp.minimum(i, n - 1), 0))
    return pl.pallas_call(
        functools.partial(_in_proj_kernel, tiles_per_seq=seq_len // tm),
        grid=(n + 1,),
        in_specs=[cur(D_MODEL), par(norm_w),
                  pl.BlockSpec((None, D_MODEL, PROJ_COLS), lambda i: (layer, 0, 0), pipeline_mode=pl.Buffered(1)),
                  par(conv_w), par(conv_b)],
        out_specs=[pl.BlockSpec((tm, SSD_CONV_CH), lambda i: (jnp.maximum(i - 1, 0), 0)),
                   cur(F_COLS), cur(B_COLS)],
        out_shape=[jax.ShapeDtypeStruct((t, SSD_CONV_CH), F32), jax.ShapeDtypeStruct((t, F_COLS), F32),
                   jax.ShapeDtypeStruct((t, B_COLS), BF16)],
        scratch_shapes=[pltpu.VMEM((tm + 8, SSD_CONV_CH), F32)],
        compiler_params=pltpu.CompilerParams(dimension_semantics=("arbitrary",),
                                             vmem_limit_bytes=V7X_VMEM_LIMIT),
        name="in_proj",
    )(h2d, norm_w, w_packed, conv_w, conv_b)


def _mixer_kernel(lam_ref, relb_ref,
                  xbc_ref, z_ref, mo_ref, sma_ref, smb_ref, dq_ref, dk_ref, dv_ref, mq_ref, mk_ref, mv_ref,
                  biasa_ref, biasb_ref, alog_ref, dfull_ref, ssdnw_ref, dnw_ref, mnw_ref,
                  y_ref,
                  prev_t, ct, nm, mst, vb, bias, qbd, rstat, acc_s, s_even, s_odd,
                  *, layer, lam_init, n_blocks):
    L = CHUNK
    b = pl.program_id(0)
    c = pl.program_id(1)

    lane128 = lax.broadcasted_iota(jnp.int32, (L, 128), 1)
    lo = lane128 < 64
    row_i = lax.broadcasted_iota(jnp.int32, (L, L), 0)
    col_i = lax.broadcasted_iota(jnp.int32, (L, L), 1)
    tril = row_i >= col_i
    row_lo = row_i < 64

    @pl.when(jnp.logical_and(b == 0, c == 0))
    def _():
        for h in range(DIFF_HEADS):
            bias[0, h] = jnp.full((L, L), MASKED, F32)
            bias[3, h] = jnp.zeros((L, L), F32)
            far = relb_ref[REL_BUCKETS - 1, h]
            for which, off in ((1, 0), (2, L)):
                dist = row_i - col_i + off
                v = jnp.full((L, L), far, F32)
                for bkt in range(REL_BUCKETS - 2, -1, -1):
                    v = jnp.where(dist < _BUCKET_LO[bkt + 1], relb_ref[bkt, h], v)
                bias[which, h] = jnp.where(dist >= 0, (v - far) * LOG2E, MASKED)

    @pl.when(c == 0)
    def _():
        prev_t[...] = jnp.zeros(prev_t.shape, F32)
        ct[...] = jnp.zeros(ct.shape, F32)
        nm[...] = jnp.zeros(nm.shape, F32)
        mst[...] = jnp.zeros(mst.shape, F32)

        def fill(t, carry):
            r0 = pl.multiple_of(t * L, L)
            v = dv_ref[0, pl.ds(r0, L), :]
            for h in range(DIFF_HEADS):
                own = lo if h % 2 == 0 else jnp.logical_not(lo)
                vb[h, pl.ds(r0, L), :] = jnp.where(own, v[:, 128 * (h // 2):128 * (h // 2 + 1)],
                                                   jnp.ones((), BF16))
            return carry

        lax.fori_loop(0, n_blocks, fill, 0)

    pre_a = sma_ref[0] + biasa_ref[...]
    log_f = -_softplus(-(smb_ref[0] + biasb_ref[...]))
    ssd_lane = lane128 < LANE_ML
    a_neg = jnp.where(ssd_lane[0:1], -jnp.exp(alog_ref[...]), 0.0)
    dt_c = _softplus(pre_a)
    csum = _scan_rows(jnp.where(ssd_lane, dt_c * a_neg, log_f), jnp.add, 0.0)
    last = csum[L - 1:L]
    u_c = pre_a - csum
    m_intra = csum + _scan_rows(u_c, jnp.maximum, -jnp.inf)
    w_end = last - csum + pre_a
    m_loc = jnp.max(w_end, axis=0, keepdims=True)
    e_end = jnp.exp(w_end - m_loc)
    m_prev = mst[0:1]
    inter_log = csum + m_prev
    m_t = jnp.maximum(inter_log, m_intra)
    w_inter = jnp.exp(inter_log - m_t)
    a_col = csum - m_t
    en = jnp.exp(-m_t)
    m_new = jnp.maximum(last + m_prev, m_loc)
    a_prev = jnp.exp(last + m_prev - m_new)
    a_loc = jnp.exp(m_loc - m_new)
    rows_t = jnp.where(ssd_lane, csum, u_c).T

    def bcast(arr, j):
        return jnp.broadcast_to(arr[:, j:j + 1], (arr.shape[0], 128))

    def pair(arr, j0, k):
        return jnp.where(lo[:arr.shape[0]], bcast(arr, j0 + 2 * k), bcast(arr, j0 + 2 * k + 1))

    xc = xbc_ref[0]
    xs = xc[:, :SSD_WIDTH]
    bmat = xc[:, SSD_WIDTH:SSD_WIDTH + 256]
    cmat = xc[:, SSD_WIDTH + 256:]
    dfull = dfull_ref[...]
    cs_b = [bcast(csum, LANE_SSD + h) for h in range(SSD_HEADS)]
    xdt, xds, ecs, cdec = [], [], [], []
    for k in range(4):
        cs_p = jnp.where(lo, cs_b[2 * k], cs_b[2 * k + 1])
        last_p = pair(last, LANE_SSD, k)
        xdt.append(xs[:, 128 * k:128 * (k + 1)] * pair(dt_c, LANE_SSD, k))
        xds.append((xdt[k] * jnp.exp(last_p - cs_p)).astype(BF16))
        ecs.append(jnp.exp(cs_p))
        cdec.append(jnp.exp(last_p))

    y_slabs = []
    for g in range(SSD_GROUPS):
        bm_g = bmat[:, 128 * g:128 * (g + 1)]
        cm_b = cmat[:, 128 * g:128 * (g + 1)].astype(BF16)
        cb = _dot_nt(cm_b, bm_g.astype(BF16))
        s_loc = _dot_tn(bm_g.astype(BF16), jnp.concatenate(xds[2 * g:2 * g + 2], axis=1))
        prev = prev_t[g]
        y_off = _dot(cm_b, prev.astype(BF16))
        for kk in range(2):
            k = 2 * g + kk
            slab = xdt[k].astype(BF16)
            halves = []
            for hl in range(2):
                h = 2 * k + hl
                diff = cs_b[h] - rows_t[LANE_SSD + h:LANE_SSD + h + 1, :]
                dec = jnp.exp(jnp.where(tril, diff, -jnp.inf))
                halves.append(_dot((cb * dec).astype(BF16), slab))
            y_diag = jnp.where(lo, halves[0], halves[1])
            y_slabs.append(y_diag + y_off[:, 128 * kk:128 * (kk + 1)] * ecs[k]
                           + xs[:, 128 * k:128 * (k + 1)] * dfull[:, 128 * k:128 * (k + 1)])
        prev_t[g] = prev * jnp.concatenate(cdec[2 * g:2 * g + 2], axis=1) + s_loc
    y = jnp.concatenate(y_slabs, axis=1)
    yz = y * z_ref[0]
    var = jnp.mean(yz * yz, axis=-1, keepdims=True)
    y_ssd = (yz * lax.rsqrt(var + NORM_EPS)) * ssdnw_ref[...]

    bd_mask = row_lo == (col_i < 64)
    inv_sqrt_d = MLSTM_HEAD_DIM ** -0.5
    ones_b = jnp.ones((L, 128), BF16)
    mq = mq_ref[0]
    mk = mk_ref[0]
    mv = mv_ref[0]
    mo = mo_ref[0]
    mnw = mnw_ref[...]
    m_slabs = []
    for k in range(2):
        sl = slice(128 * k, 128 * (k + 1))
        q_b = mq[:, sl]
        k_b = mk[:, sl]
        v_b = mv[:, sl]
        v1 = jnp.concatenate([v_b, ones_b], axis=1)
        intra = []
        for hl in range(2):
            h = 2 * k + hl
            qm = jnp.where(lo if hl == 0 else jnp.logical_not(lo), q_b, jnp.zeros((), BF16))
            qk = _dot_nt(qm, k_b)
            arg = bcast(a_col, LANE_ML + h) + rows_t[LANE_ML + h:LANE_ML + h + 1, :]
            w_intra = jnp.exp(jnp.where(tril, arg, -jnp.inf))
            intra.append(_dot(((qk * inv_sqrt_d) * w_intra).astype(BF16), v1))
        num_intra = jnp.where(lo, intra[0][:, :128], intra[1][:, :128])
        den_intra = jnp.where(lo, intra[0][:, 128:], intra[1][:, 128:])
        ct_prev = ct[k]
        nm_prev = nm[k]
        wi = pair(w_inter, LANE_ML, k)
        num = num_intra + wi * _dot(q_b, ct_prev.astype(BF16))
        den = den_intra + wi * _dot(q_b, nm_prev.astype(BF16))
        denom = jnp.maximum(jnp.abs(den), pair(en, LANE_ML, k))
        hs = num / denom
        sq = hs * hs
        s_lo = jnp.sum(jnp.where(lo, sq, 0.0), axis=1, keepdims=True)
        s_hi = jnp.sum(jnp.where(lo, 0.0, sq), axis=1, keepdims=True)
        hvar = jnp.where(lo, s_lo, s_hi) * (1.0 / MLSTM_HEAD_DIM)
        hn = (hs * lax.rsqrt(hvar + NORM_EPS)) * mnw[:, sl]
        m_slabs.append(mo[:, sl] * hn)
        e_p = pair(e_end, LANE_ML, k)
        loc = _dot_tn(k_b, jnp.concatenate([(v_b.astype(F32) * e_p).astype(BF16), e_p.astype(BF16)], axis=1)
                      ) * inv_sqrt_d
        ap = pair(a_prev, LANE_ML, k)
        al = pair(a_loc, LANE_ML, k)
        ct[k] = ct_prev * ap + jnp.where(bd_mask, loc[:, :128], 0.0) * al
        nm[k] = nm_prev * ap + jnp.where(bd_mask, loc[:, 128:], 0.0) * al
    mst[...] = jnp.broadcast_to(m_new, mst.shape)
    y_ml = jnp.concatenate(m_slabs, axis=1)

    lam = lam_ref[layer]
    q = dq_ref[0]
    zero_b = jnp.zeros((), BF16)
    for h in range(DIFF_HEADS):
        qs = q[:, 128 * (h // 2):128 * (h // 2 + 1)]
        base = 64 * (h % 2)
        q0 = jnp.where((lane128 >= base) & (lane128 < base + 32), qs, zero_b)
        q1 = jnp.where((lane128 >= base + 32) & (lane128 < base + 64), qs, zero_b)
        qbd[h] = jnp.concatenate([q0, q1], axis=0)
    n_groups = c // KEY_GROUP + 1
    gw = KEY_GROUP * L

    def fill_scores(g, dst):
        r0 = pl.multiple_of(g * gw, gw)
        for h in range(DIFF_HEADS):
            keys = dk_ref[0, pl.ds(r0, gw), 128 * (h // 2):128 * (h // 2 + 1)]
            add = jnp.concatenate(
                [bias[jnp.clip(c - KEY_GROUP * g - j, -1, 2) + 1, h] for j in range(KEY_GROUP)], axis=1)
            dst[h] = _dot_nt(qbd[h], keys) + jnp.concatenate([add, add], axis=0)

    def consume(g, src):
        r0 = pl.multiple_of(g * gw, gw)
        for h in range(DIFF_HEADS):
            s = src[h]
            t = s[:, :L]
            for j in range(1, KEY_GROUP):
                t = jnp.maximum(t, s[:, L * j:L * (j + 1)])
            m_old = rstat[h]
            m_new = jnp.maximum(m_old, jnp.max(t, axis=1, keepdims=True))
            p = jnp.concatenate([jnp.exp2(s[:, L * j:L * (j + 1)] - m_new) for j in range(KEY_GROUP)], axis=1)
            acc_s[h] = acc_s[h] * jnp.exp2(m_old - m_new) + _dot(p.astype(BF16), vb[h, pl.ds(r0, gw), :])
            rstat[h] = m_new

    rstat[...] = jnp.full(rstat.shape, MASKED, F32)
    acc_s[...] = jnp.zeros(acc_s.shape, F32)
    fill_scores(0, s_even)

    def key_step(g, carry):
        @pl.when(g % 2 == 0)
        def _():
            fill_scores(g + 1, s_odd)
            consume(g, s_even)

        @pl.when(g % 2 == 1)
        def _():
            fill_scores(g + 1, s_even)
            consume(g, s_odd)

        return carry

    g_last = n_groups - 1
    lax.fori_loop(0, g_last, key_step, 0)

    @pl.when(g_last % 2 == 0)
    def _():
        consume(g_last, s_even)

    @pl.when(g_last % 2 == 1)
    def _():
        consume(g_last, s_odd)

    d_slabs = []
    for k in range(2):
        a0, a1 = acc_s[2 * k], acc_s[2 * k + 1]
        lo2 = jnp.concatenate([lo, lo], axis=0)
        r = jnp.where(lo2, a0, a1) * (1.0 / pltpu.roll(jnp.where(lo2, a1, a0), 64, 1))
        o = r[:L] - lam * r[L:]
        sq = o * o
        s_lo = jnp.sum(jnp.where(lo, sq, 0.0), axis=1, keepdims=True)
        s_hi = jnp.sum(jnp.where(lo, 0.0, sq), axis=1, keepdims=True)
        ovar = jnp.where(lo, s_lo, s_hi) * (1.0 / DIFF_V_DIM)
        d_slabs.append(((o * lax.rsqrt(ovar + NORM_EPS)) * dnw_ref[:, 128 * k:128 * (k + 1)])
                       * (1.0 - lam_init))
    y_diff = jnp.concatenate(d_slabs, axis=1)

    y_ref[0] = jnp.concatenate([y_ssd, y_diff, y_ml], axis=1).astype(BF16)


def _mixer(px, pf, pb, lam, rel_bias, params, *, layer, lam_init):
    bsz, s, _ = pf.shape
    L = CHUNK
    nb = s // L

    def tok(width, col0):
        return pl.BlockSpec((1, L, width), lambda b, c: (b, c, col0 // width))

    def seq(width, col0):
        return pl.BlockSpec((1, s, width), lambda b, c: (b, 0, col0 // width))

    def par(a):
        return pl.BlockSpec((None,) + a.shape[1:], lambda b, c: (layer, 0, 0))

    smem = pl.BlockSpec(memory_space=pltpu.SMEM)
    in_specs = [smem, smem,
                tok(1024, 0), tok(512, F_Z), tok(256, F_MO), tok(128, F_SMALL_A), tok(128, F_SMALL_B),
                tok(256, B_DQ), seq(256, B_DK), seq(256, B_DV), tok(256, B_MQ), tok(256, B_MK), tok(256, B_MV),
                ] + [par(p) for p in params]
    scratch = [
        pltpu.VMEM((SSD_GROUPS, SSD_STATE, 256), F32),
        pltpu.VMEM((2, 128, 128), F32),
        pltpu.VMEM((2, 128, 128), F32),
        pltpu.VMEM((8, 128), F32),
        pltpu.VMEM((DIFF_HEADS, s, 128), BF16),
        pltpu.VMEM((4, DIFF_HEADS, L, L), F32),
        pltpu.VMEM((DIFF_HEADS, 2 * L, 128), BF16),
        pltpu.VMEM((DIFF_HEADS, 2 * L, L), F32),
        pltpu.VMEM((DIFF_HEADS, 2 * L, 128), F32),
        pltpu.VMEM((DIFF_HEADS, 2 * L, KEY_GROUP * L), F32),
        pltpu.VMEM((DIFF_HEADS, 2 * L, KEY_GROUP * L), F32),
    ]
    return pl.pallas_call(
        functools.partial(_mixer_kernel, layer=layer, lam_init=lam_init, n_blocks=nb),
        grid=(bsz, nb),
        in_specs=in_specs,
        out_specs=pl.BlockSpec((1, L, MIX_WIDTH), lambda b, c: (b, c, 0)),
        out_shape=jax.ShapeDtypeStruct((bsz, s, MIX_WIDTH), BF16),
        scratch_shapes=scratch,
        compiler_params=pltpu.CompilerParams(dimension_semantics=("arbitrary", "arbitrary"),
                                             vmem_limit_bytes=V7X_VMEM_LIMIT),
        name="mixer",
    )(lam, rel_bias.astype(F32), px, *([pf] * 4), *([pb] * 6), *params)


def _dense_kernel(h_ref, y_ref, p_ref, wo_ref, n2_ref, wg_ref, wu_ref, wd_ref, pg_ref, pp_ref, fn_ref,
                  o_ref, *, final, h_chunk):
    h1 = h_ref[...] + _dot(y_ref[...], wo_ref[...])
    var = jnp.mean(h1 * h1, axis=-1, keepdims=True)
    u = ((h1 * lax.rsqrt(var + NORM_EPS)) * n2_ref[...]).astype(BF16)
    ffn = jnp.zeros(h1.shape, F32)
    for c0 in range(0, FFN_HIDDEN, h_chunk):
        g = _dot(u, wg_ref[:, c0:c0 + h_chunk])
        up = _dot(u, wu_ref[:, c0:c0 + h_chunk])
        a = ((g * _sigmoid(g)) * up).astype(BF16)
        ffn = ffn + _dot(a, wd_ref[c0:c0 + h_chunk, :])
    h2 = h1 + ffn
    gate = _sigmoid(_dot(h2.astype(BF16), pg_ref[...]))
    h3 = h2 + gate * _dot(p_ref[...].astype(BF16), pp_ref[...])
    if final:
        fvar = jnp.mean(h3 * h3, axis=-1, keepdims=True)
        h3 = (h3 * lax.rsqrt(fvar + NORM_EPS)) * fn_ref[...]
    o_ref[...] = h3


def _dense(h2d, y2d, p_all, weights, final_norm_w, *, layer, final, tm):
    t = h2d.shape[0]
    row = lambda width: pl.BlockSpec((tm, width), lambda i: (i, 0))
    res = lambda a: pl.BlockSpec((None,) + a.shape[1:], lambda i: (layer, 0, 0), pipeline_mode=pl.Buffered(1))
    return pl.pallas_call(
        functools.partial(_dense_kernel, final=final, h_chunk=FFN_CHUNK),
        grid=(t // tm,),
        in_specs=[row(D_MODEL), row(MIX_WIDTH), pl.BlockSpec((None, tm, PLE_DIM), lambda i: (layer, i, 0))]
        + [res(w) for w in weights] + [pl.BlockSpec((1, D_MODEL), lambda i: (0, 0))],
        out_specs=row(D_MODEL),
        out_shape=jax.ShapeDtypeStruct((t, D_MODEL), F32),
        compiler_params=pltpu.CompilerParams(dimension_semantics=("arbitrary",),
                                             vmem_limit_bytes=V7X_VMEM_LIMIT),
        name="dense",
    )(h2d, y2d, p_all, *weights, final_norm_w.reshape(1, D_MODEL).astype(F32))


def _pack_w_in(w):
    z, xbc, dt, dq, dk, dv, mq, mk, mv, mo, mi, mf = jnp.split(w, np.cumsum(IN_SPLITS)[:-1], axis=-1)
    zeros = lambda n: jnp.zeros(w.shape[:-1] + (n,), w.dtype)
    small_a = jnp.concatenate([dt, mi, zeros(128 - LANE_ML - MLSTM_HEADS)], axis=-1)
    small_b = jnp.concatenate([zeros(LANE_ML), mf, zeros(128 - LANE_ML - MLSTM_HEADS)], axis=-1)
    return jnp.concatenate([xbc, z, mo, small_a, small_b, dq, dk, dv, mq, mk, mv], axis=-1).astype(BF16)


def _lanes(depth, *placed):
    row = jnp.zeros((depth, 128), F32)
    for off, v in placed:
        row = row.at[:, off:off + v.shape[1]].set(v.astype(F32))
    return row[:, None, :]


def kernel(x, p, norm1_w, w_in, ssd_conv_w, ssd_conv_b, ssd_dt_bias, ssd_a_log, ssd_d, ssd_norm_w, diff_lq1, diff_lk1, diff_lq2, diff_lk2, diff_norm_w, rel_bias, mlstm_i_bias, mlstm_f_bias, mlstm_norm_w, w_out, norm2_w, w_ffn_gate, w_ffn_up, w_ffn_down, ple_gate_w, ple_proj_w, final_norm_w):
    bsz, s, d = x.shape
    depth = w_in.shape[0]
    assert d == D_MODEL and s % (KEY_GROUP * CHUNK) == 0
    t = bsz * s
    tm = ROW_TILE if s % ROW_TILE == 0 else CHUNK
    row3 = lambda a: a.astype(F32)[:, None, :]

    lam_inits = [0.8 - 0.6 * math.exp(-0.3 * i) for i in range(depth)]
    lam = (jnp.exp(jnp.sum(diff_lq1.astype(F32) * diff_lk1.astype(F32), axis=-1))
           - jnp.exp(jnp.sum(diff_lq2.astype(F32) * diff_lk2.astype(F32), axis=-1))
           + jnp.asarray(lam_inits, F32))
    proj_args = (row3(norm1_w), _pack_w_in(w_in), ssd_conv_w.astype(F32), row3(ssd_conv_b))
    mixer_params = [
        _lanes(depth, (LANE_SSD, ssd_dt_bias), (LANE_ML, mlstm_i_bias)), _lanes(depth, (LANE_ML, mlstm_f_bias)),
        _lanes(depth, (LANE_SSD, ssd_a_log)),
        row3(jnp.repeat(ssd_d, SSD_HEAD_DIM, axis=1)), row3(ssd_norm_w),
        row3(jnp.tile(diff_norm_w, (1, DIFF_HEADS))), row3(mlstm_norm_w),
    ]
    dense_weights = [w_out.astype(BF16), row3(norm2_w), w_ffn_gate.astype(BF16), w_ffn_up.astype(BF16),
                     w_ffn_down.astype(BF16), ple_gate_w.astype(BF16), ple_proj_w.astype(BF16)]
    p_all = p.reshape(depth, t, PLE_DIM).astype(F32)

    h = x.reshape(t, d).astype(F32)
    for i in range(depth):
        px, pf, pb = _in_proj(h, *proj_args, layer=i, tm=tm, seq_len=s)
        y = _mixer(px.reshape(bsz, s, SSD_CONV_CH), pf.reshape(bsz, s, F_COLS), pb.reshape(bsz, s, B_COLS), lam,
                   rel_bias, mixer_params, layer=i, lam_init=lam_inits[i])
        h = _dense(h, y.reshape(t, MIX_WIDTH), p_all, dense_weights, final_norm_w, layer=i,
                   final=(i == depth - 1), tm=tm)
    return h.reshape(bsz, s, d).astype(x.dtype)
```

```python
import functools
import math

import numpy as np
import jax
import jax.numpy as jnp
from jax import lax
from jax.experimental import pallas as pl
from jax.experimental.pallas import tpu as pltpu

F32 = jnp.float32
BF16 = jnp.bfloat16

D_MODEL = 1024
PLE_DIM = 256
NORM_EPS = 1e-6
CHUNK = 128
KEY_GROUP = 4

SSD_HEADS = 8
SSD_HEAD_DIM = 64
SSD_WIDTH = 512
SSD_GROUPS = 2
SSD_STATE = 128
SSD_CONV = 4
SSD_CONV_CH = 1024
DIFF_HEADS = 4
DIFF_QK_DIM = 32
DIFF_V_DIM = 64
DIFF_WIDTH = 256
REL_BUCKETS = 32
REL_MAX_DIST = 128
MLSTM_HEADS = 4
MLSTM_HEAD_DIM = 64
MLSTM_WIDTH = 256
MIX_WIDTH = 1024
FFN_HIDDEN = 2816
IN_SPLITS = (512, 1024, 8, 256, 256, 256, 256, 256, 256, 256, 4, 4)

W_XBC, W_ZMO, W_SMALL, W_QKV = 0, 1024, 1792, 2048
PROJ_COLS = W_QKV + 6 * 256
ST_Z, ST_MO, ST_SMALL_A, ST_SMALL_B, ST_F_COLS = 0, 512, 768, 896, 1024
ST_Q, ST_MQ, ST_MK, ST_MV, ST_B_COLS = 0, 256, 512, 768, 1024
LANE_SSD, LANE_ML = 0, 8
ATTN_C1 = (DIFF_QK_DIM ** -0.5) * math.log2(math.e)

ROW_TILE = 512
FFN_CHUNK = 1408

V7X_VMEM_LIMIT = 56 * 1024 * 1024
MASKED = -1e30
LOG2E = math.log2(math.e)


def _t5_bucket_lower_bounds():
    max_exact = REL_BUCKETS // 2
    d = np.arange(0, 2 * CHUNK, dtype=np.int64)
    df = np.maximum(d, 1).astype(np.float32)
    large = max_exact + (np.log(df / np.float32(max_exact)) / np.float32(math.log(REL_MAX_DIST / max_exact))
                         * np.float32(REL_BUCKETS - max_exact)).astype(np.int32)
    large = np.minimum(large, REL_BUCKETS - 1)
    bucket = np.where(d < max_exact, d, large)
    assert np.all(np.diff(bucket) >= 0) and bucket[CHUNK] == REL_BUCKETS - 1
    return [int(np.argmax(bucket >= b)) for b in range(REL_BUCKETS)]


_BUCKET_LO = _t5_bucket_lower_bounds()


def _sigmoid(x):
    return 0.5 * jnp.tanh(0.5 * x) + 0.5


def _softplus(x):
    return jnp.maximum(x, 0.0) + jnp.log(1.0 + jnp.exp(-jnp.abs(x)))


def _dot(a, b):
    return jnp.dot(a, b, preferred_element_type=F32)


def _dot_nt(a, b):
    return lax.dot_general(a, b, (((1,), (1,)), ((), ())), preferred_element_type=F32)


def _dot_tn(a, b):
    return lax.dot_general(a, b, (((0,), (0,)), ((), ())), preferred_element_type=F32)


def _scan_rows(x, combine, identity):
    sub = lax.broadcasted_iota(jnp.int32, (8, x.shape[1]), 0)
    tiles, carry = [], None
    for i in range(x.shape[0] // 8):
        t = x[8 * i:8 * (i + 1)]
        for sh in (1, 2, 4):
            t = combine(t, jnp.where(sub >= sh, pltpu.roll(t, sh, 0), identity))
        if carry is not None:
            t = combine(t, carry)
        carry = t[7:8]
        tiles.append(t)
    return jnp.concatenate(tiles, axis=0)


def _mixer_kernel(lam_ref, relb_ref, hc_ref, hn_ref, n1_ref, w_ref, convw_ref, convb_ref,
                  biasa_ref, biasb_ref, alog_ref, dfull_ref, ssdnw_ref, dnw_ref, mnw_ref,
                  y_ref,
                  prev_t, ct, nm, mst, kb, vb, bias, qbd, rstat, acc_s, s_even, s_odd,
                  xpad_a, stf_a, stb_a, xpad_b, stf_b, stb_b,
                  *, layer, lam_init, n_blocks):
    L = CHUNK
    b = pl.program_id(0)
    c = pl.program_id(1)

    lane128 = lax.broadcasted_iota(jnp.int32, (L, 128), 1)
    lo = lane128 < 64
    row_i = lax.broadcasted_iota(jnp.int32, (L, L), 0)
    col_i = lax.broadcasted_iota(jnp.int32, (L, L), 1)
    tril = row_i >= col_i
    row_lo = row_i < 64

    @pl.when(jnp.logical_and(b == 0, c == 0))
    def _():
        for h in range(DIFF_HEADS):
            bias[0, h] = jnp.full((L, L), MASKED, F32)
            bias[3, h] = jnp.zeros((L, L), F32)
            far = relb_ref[REL_BUCKETS - 1, h]
            for which, off in ((1, 0), (2, L)):
                dist = row_i - col_i + off
                v = jnp.full((L, L), far, F32)
                for bkt in range(REL_BUCKETS - 2, -1, -1):
                    v = jnp.where(dist < _BUCKET_LO[bkt + 1], relb_ref[bkt, h], v)
                bias[which, h] = jnp.where(dist >= 0, (v - far) * LOG2E, MASKED)

    @pl.when(c == 0)
    def _():
        prev_t[...] = jnp.zeros(prev_t.shape, F32)
        ct[...] = jnp.zeros(ct.shape, F32)
        nm[...] = jnp.zeros(nm.shape, F32)
        mst[...] = jnp.zeros(mst.shape, F32)

    def normed(h_ref):
        x = h_ref[0]
        var = jnp.mean(x * x, axis=-1, keepdims=True)
        return ((x * lax.rsqrt(var + NORM_EPS)) * n1_ref[...]).astype(BF16)

    def project_a(u, xpad, stf):
        xpad[8:8 + L, :] = _dot(u, w_ref[:, W_XBC:W_ZMO])
        zmo = _dot(u, w_ref[:, W_ZMO:W_SMALL])
        z = zmo[:, :SSD_WIDTH]
        stf[:, ST_Z:ST_MO] = z * _sigmoid(z)
        stf[:, ST_MO:ST_SMALL_A] = _sigmoid(zmo[:, SSD_WIDTH:])

    def project_b(u, stf, stb, slot, row0):
        stf[:, ST_SMALL_A:] = _dot(u, w_ref[:, W_SMALL:W_QKV])
        qkv = _dot(u, w_ref[:, W_QKV:])
        stb[:, ST_Q:ST_MQ] = (qkv[:, 0:256] * ATTN_C1).astype(BF16)
        stb[:, ST_MQ:] = qkv[:, 768:].astype(BF16)
        kb[slot, pl.ds(row0, L), :] = qkv[:, 256:512].astype(BF16)
        v = qkv[:, 512:768].astype(BF16)
        for h in range(DIFF_HEADS):
            own = lo if h % 2 == 0 else jnp.logical_not(lo)
            vb[slot, h, pl.ds(row0, L), :] = jnp.where(own, v[:, 128 * (h // 2):128 * (h // 2 + 1)],
                                                       jnp.ones((), BF16))

    g = b * n_blocks + c
    c_next = lax.rem(c + 1, n_blocks)
    slot = lax.rem(b, 2)
    slot_next = lax.rem(b + (c + 1) // n_blocks, 2)

    @pl.when(g == 0)
    def _():
        xpad_a[...] = jnp.zeros(xpad_a.shape, F32)
        xpad_b[...] = jnp.zeros(xpad_b.shape, F32)
        kb[...] = jnp.zeros(kb.shape, BF16)
        vb[...] = jnp.zeros(vb.shape, BF16)
        u0 = normed(hc_ref)
        project_a(u0, xpad_a, stf_a)
        project_b(u0, stf_a, stb_a, 0, 0)

    def chunk_block(xpad, stf, stb, xpad_other, stf_other, stb_other):
        xpad[0:8, :] = jnp.where(c > 0, xpad_other[L:L + 8, :], 0.0)
        u_next = normed(hn_ref)
        project_a(u_next, xpad_other, stf_other)

        pre_a = stf[:, ST_SMALL_A:ST_SMALL_B] + biasa_ref[...]
        log_f = -_softplus(-(stf[:, ST_SMALL_B:] + biasb_ref[...]))
        ssd_lane = lane128 < LANE_ML
        a_neg = jnp.where(ssd_lane[0:1], -jnp.exp(alog_ref[...]), 0.0)
        dt_c = _softplus(pre_a)
        csum = _scan_rows(jnp.where(ssd_lane, dt_c * a_neg, log_f), jnp.add, 0.0)
        last = csum[L - 1:L]
        u_c = pre_a - csum
        m_intra = csum + _scan_rows(u_c, jnp.maximum, -jnp.inf)
        w_end = last - csum + pre_a
        m_loc = jnp.max(w_end, axis=0, keepdims=True)
        e_end = jnp.exp(w_end - m_loc)
        m_prev = mst[0:1]
        inter_log = csum + m_prev
        m_t = jnp.maximum(inter_log, m_intra)
        w_inter = jnp.exp(inter_log - m_t)
        a_col = csum - m_t
        en = jnp.exp(-m_t)
        m_new = jnp.maximum(last + m_prev, m_loc)
        a_prev = jnp.exp(last + m_prev - m_new)
        a_loc = jnp.exp(m_loc - m_new)
        rows_t = jnp.where(ssd_lane, csum, u_c).T

        def bcast(arr, j):
            return jnp.broadcast_to(arr[:, j:j + 1], (arr.shape[0], 128))

        def pair(arr, j0, k):
            return jnp.where(lo[:arr.shape[0]], bcast(arr, j0 + 2 * k), bcast(arr, j0 + 2 * k + 1))

        conv = convb_ref[...]
        for k in range(SSD_CONV):
            conv = conv + convw_ref[k:k + 1, :] * xpad[5 + k:5 + k + L, :]
        xc = conv * _sigmoid(conv)
        xs = xc[:, :SSD_WIDTH]
        bmat = xc[:, SSD_WIDTH:SSD_WIDTH + 256]
        cmat = xc[:, SSD_WIDTH + 256:]
        dfull = dfull_ref[...]
        cs_b = [bcast(csum, LANE_SSD + h) for h in range(SSD_HEADS)]
        xdt, xds, ecs, cdec = [], [], [], []
        for k in range(4):
            cs_p = jnp.where(lo, cs_b[2 * k], cs_b[2 * k + 1])
            last_p = pair(last, LANE_SSD, k)
            xdt.append(xs[:, 128 * k:128 * (k + 1)] * pair(dt_c, LANE_SSD, k))
            xds.append((xdt[k] * jnp.exp(last_p - cs_p)).astype(BF16))
            ecs.append(jnp.exp(cs_p))
            cdec.append(jnp.exp(last_p))

        y_slabs = []
        for g_ in range(SSD_GROUPS):
            bm_g = bmat[:, 128 * g_:128 * (g_ + 1)]
            cm_b = cmat[:, 128 * g_:128 * (g_ + 1)].astype(BF16)
            cb = _dot_nt(cm_b, bm_g.astype(BF16))
            s_loc = _dot_tn(bm_g.astype(BF16), jnp.concatenate(xds[2 * g_:2 * g_ + 2], axis=1))
            prev = prev_t[g_]
            y_off = _dot(cm_b, prev.astype(BF16))
            for kk in range(2):
                k = 2 * g_ + kk
                slab = xdt[k].astype(BF16)
                halves = []
                for hl in range(2):
                    h = 2 * k + hl
                    diff = cs_b[h] - rows_t[LANE_SSD + h:LANE_SSD + h + 1, :]
                    dec = jnp.exp(jnp.where(tril, diff, -jnp.inf))
                    halves.append(_dot((cb * dec).astype(BF16), slab))
                y_diag = jnp.where(lo, halves[0], halves[1])
                y_slabs.append(y_diag + y_off[:, 128 * kk:128 * (kk + 1)] * ecs[k]
                               + xs[:, 128 * k:128 * (k + 1)] * dfull[:, 128 * k:128 * (k + 1)])
            prev_t[g_] = prev * jnp.concatenate(cdec[2 * g_:2 * g_ + 2], axis=1) + s_loc
        y = jnp.concatenate(y_slabs, axis=1)
        yz = y * stf[:, ST_Z:ST_MO]
        var = jnp.mean(yz * yz, axis=-1, keepdims=True)
        y_ref[0, :, 0:SSD_WIDTH] = ((yz * lax.rsqrt(var + NORM_EPS)) * ssdnw_ref[...]).astype(BF16)

        project_b(u_next, stf_other, stb_other, slot_next, pl.multiple_of(c_next * L, L))

        bd_mask = row_lo == (col_i < 64)
        inv_sqrt_d = MLSTM_HEAD_DIM ** -0.5
        ones_b = jnp.ones((L, 128), BF16)
        mq = stb[:, ST_MQ:ST_MK]
        mk = stb[:, ST_MK:ST_MV]
        mv = stb[:, ST_MV:]
        mo = stf[:, ST_MO:ST_SMALL_A]
        mnw = mnw_ref[...]
        m_slabs = []
        for k in range(2):
            sl = slice(128 * k, 128 * (k + 1))
            q_b = mq[:, sl]
            k_b = mk[:, sl]
            v_b = mv[:, sl]
            v1 = jnp.concatenate([v_b, ones_b], axis=1)
            intra = []
            for hl in range(2):
                h = 2 * k + hl
                qm = jnp.where(lo if hl == 0 else jnp.logical_not(lo), q_b, jnp.zeros((), BF16))
                qk = _dot_nt(qm, k_b)
                arg = bcast(a_col, LANE_ML + h) + rows_t[LANE_ML + h:LANE_ML + h + 1, :]
                w_intra = jnp.exp(jnp.where(tril, arg, -jnp.inf))
                intra.append(_dot(((qk * inv_sqrt_d) * w_intra).astype(BF16), v1))
            num_intra = jnp.where(lo, intra[0][:, :128], intra[1][:, :128])
            den_intra = jnp.where(lo, intra[0][:, 128:], intra[1][:, 128:])
            ct_prev = ct[k]
            nm_prev = nm[k]
            wi = pair(w_inter, LANE_ML, k)
            num = num_intra + wi * _dot(q_b, ct_prev.astype(BF16))
            den = den_intra + wi * _dot(q_b, nm_prev.astype(BF16))
            denom = jnp.maximum(jnp.abs(den), pair(en, LANE_ML, k))
            hs = num / denom
            sq = hs * hs
            s_lo = jnp.sum(jnp.where(lo, sq, 0.0), axis=1, keepdims=True)
            s_hi = jnp.sum(jnp.where(lo, 0.0, sq), axis=1, keepdims=True)
            hvar = jnp.where(lo, s_lo, s_hi) * (1.0 / MLSTM_HEAD_DIM)
            hn = (hs * lax.rsqrt(hvar + NORM_EPS)) * mnw[:, sl]
            m_slabs.append(mo[:, sl] * hn)
            e_p = pair(e_end, LANE_ML, k)
            loc = _dot_tn(k_b, jnp.concatenate([(v_b.astype(F32) * e_p).astype(BF16), e_p.astype(BF16)], axis=1)
                          ) * inv_sqrt_d
            ap = pair(a_prev, LANE_ML, k)
            al = pair(a_loc, LANE_ML, k)
            ct[k] = ct_prev * ap + jnp.where(bd_mask, loc[:, :128], 0.0) * al
            nm[k] = nm_prev * ap + jnp.where(bd_mask, loc[:, 128:], 0.0) * al
        mst[...] = jnp.broadcast_to(m_new, mst.shape)
        y_ref[0, :, SSD_WIDTH + DIFF_WIDTH:] = jnp.concatenate(m_slabs, axis=1).astype(BF16)

        q = stb[:, ST_Q:ST_MQ]
        zero_b = jnp.zeros((), BF16)
        for h in range(DIFF_HEADS):
            qs = q[:, 128 * (h // 2):128 * (h // 2 + 1)]
            base = 64 * (h % 2)
            q0 = jnp.where((lane128 >= base) & (lane128 < base + 32), qs, zero_b)
            q1 = jnp.where((lane128 >= base + 32) & (lane128 < base + 64), qs, zero_b)
            qbd[h] = jnp.concatenate([q0, q1], axis=0)

    @pl.when(g % 2 == 0)
    def _():
        chunk_block(xpad_a, stf_a, stb_a, xpad_b, stf_b, stb_b)

    @pl.when(g % 2 == 1)
    def _():
        chunk_block(xpad_b, stf_b, stb_b, xpad_a, stf_a, stb_a)

    lam = lam_ref[layer]
    n_groups = c // KEY_GROUP + 1
    gw = KEY_GROUP * L

    def fill_scores(kg, dst):
        r0 = pl.multiple_of(kg * gw, gw)
        for h in range(DIFF_HEADS):
            keys = kb[slot, pl.ds(r0, gw), 128 * (h // 2):128 * (h // 2 + 1)]
            add = jnp.concatenate(
                [bias[jnp.clip(c - KEY_GROUP * kg - j, -1, 2) + 1, h] for j in range(KEY_GROUP)], axis=1)
            dst[h] = _dot_nt(qbd[h], keys) + jnp.concatenate([add, add], axis=0)

    def consume(kg, src):
        r0 = pl.multiple_of(kg * gw, gw)
        for h in range(DIFF_HEADS):
            s = src[h]
            t = s[:, :L]
            for j in range(1, KEY_GROUP):
                t = jnp.maximum(t, s[:, L * j:L * (j + 1)])
            m_old = rstat[h]
            m_new = jnp.maximum(m_old, jnp.max(t, axis=1, keepdims=True))
            p = jnp.concatenate([jnp.exp2(s[:, L * j:L * (j + 1)] - m_new) for j in range(KEY_GROUP)], axis=1)
            acc_s[h] = acc_s[h] * jnp.exp2(m_old - m_new) + _dot(p.astype(BF16), vb[slot, h, pl.ds(r0, gw), :])
            rstat[h] = m_new

    rstat[...] = jnp.full(rstat.shape, MASKED, F32)
    acc_s[...] = jnp.zeros(acc_s.shape, F32)
    fill_scores(0, s_even)

    def key_step(kg, carry):
        @pl.when(kg % 2 == 0)
        def _():
            fill_scores(kg + 1, s_odd)
            consume(kg, s_even)

        @pl.when(kg % 2 == 1)
        def _():
            fill_scores(kg + 1, s_even)
            consume(kg, s_odd)

        return carry

    g_last = n_groups - 1
    lax.fori_loop(0, g_last, key_step, 0)

    @pl.when(g_last % 2 == 0)
    def _():
        consume(g_last, s_even)

    @pl.when(g_last % 2 == 1)
    def _():
        consume(g_last, s_odd)

    d_slabs = []
    for k in range(2):
        a0, a1 = acc_s[2 * k], acc_s[2 * k + 1]
        lo2 = jnp.concatenate([lo, lo], axis=0)
        r = jnp.where(lo2, a0, a1) * (1.0 / pltpu.roll(jnp.where(lo2, a1, a0), 64, 1))
        o = r[:L] - lam * r[L:]
        sq = o * o
        s_lo = jnp.sum(jnp.where(lo, sq, 0.0), axis=1, keepdims=True)
        s_hi = jnp.sum(jnp.where(lo, 0.0, sq), axis=1, keepdims=True)
        ovar = jnp.where(lo, s_lo, s_hi) * (1.0 / DIFF_V_DIM)
        d_slabs.append(((o * lax.rsqrt(ovar + NORM_EPS)) * dnw_ref[:, 128 * k:128 * (k + 1)])
                       * (1.0 - lam_init))
    y_ref[0, :, SSD_WIDTH:SSD_WIDTH + DIFF_WIDTH] = jnp.concatenate(d_slabs, axis=1).astype(BF16)


def _mixer(h3d, lam, rel_bias, params, *, layer, lam_init):
    bsz, s, _ = h3d.shape
    L = CHUNK
    nb = s // L

    def par(a):
        return pl.BlockSpec((None,) + a.shape[1:], lambda b, c: (layer, 0, 0))

    smem = pl.BlockSpec(memory_space=pltpu.SMEM)
    cur = pl.BlockSpec((1, L, D_MODEL), lambda b, c: (b, c, 0))
    nxt = pl.BlockSpec((1, L, D_MODEL),
                       lambda b, c: (jnp.minimum(b + (c + 1) // nb, bsz - 1), lax.rem(c + 1, nb), 0))
    w_spec = pl.BlockSpec((None, D_MODEL, PROJ_COLS), lambda b, c: (layer, 0, 0), pipeline_mode=pl.Buffered(1))
    in_specs = [smem, smem, cur, nxt, par(params[0]), w_spec] + [par(p) for p in params[2:]]
    stage = [
        pltpu.VMEM((L + 8, SSD_CONV_CH), F32),
        pltpu.VMEM((L, ST_F_COLS), F32),
        pltpu.VMEM((L, ST_B_COLS), BF16),
    ]
    scratch = [
        pltpu.VMEM((SSD_GROUPS, SSD_STATE, 256), F32),
        pltpu.VMEM((2, 128, 128), F32),
        pltpu.VMEM((2, 128, 128), F32),
        pltpu.VMEM((8, 128), F32),
        pltpu.VMEM((2, s, 256), BF16),
        pltpu.VMEM((2, DIFF_HEADS, s, 128), BF16),
        pltpu.VMEM((4, DIFF_HEADS, L, L), F32),
        pltpu.VMEM((DIFF_HEADS, 2 * L, 128), BF16),
        pltpu.VMEM((DIFF_HEADS, 2 * L, L), F32),
        pltpu.VMEM((DIFF_HEADS, 2 * L, 128), F32),
        pltpu.VMEM((DIFF_HEADS, 2 * L, KEY_GROUP * L), F32),
        pltpu.VMEM((DIFF_HEADS, 2 * L, KEY_GROUP * L), F32),
    ] + stage + stage
    return pl.pallas_call(
        functools.partial(_mixer_kernel, layer=layer, lam_init=lam_init, n_blocks=nb),
        grid=(bsz, nb),
        in_specs=in_specs,
        out_specs=pl.BlockSpec((1, L, MIX_WIDTH), lambda b, c: (b, c, 0)),
        out_shape=jax.ShapeDtypeStruct((bsz, s, MIX_WIDTH), BF16),
        scratch_shapes=scratch,
        compiler_params=pltpu.CompilerParams(dimension_semantics=("arbitrary", "arbitrary"),
                                             vmem_limit_bytes=V7X_VMEM_LIMIT),
        name="mixer",
    )(lam, rel_bias.astype(F32), h3d, h3d, *params)


def _dense_kernel(h_ref, y_ref, p_ref, wo_ref, n2_ref, wg_ref, wu_ref, wd_ref, pg_ref, pp_ref, fn_ref,
                  o_ref, *, final, h_chunk):
    h1 = h_ref[...] + _dot(y_ref[...], wo_ref[...])
    var = jnp.mean(h1 * h1, axis=-1, keepdims=True)
    u = ((h1 * lax.rsqrt(var + NORM_EPS)) * n2_ref[...]).astype(BF16)
    ffn = jnp.zeros(h1.shape, F32)
    for c0 in range(0, FFN_HIDDEN, h_chunk):
        g = _dot(u, wg_ref[:, c0:c0 + h_chunk])
        up = _dot(u, wu_ref[:, c0:c0 + h_chunk])
        a = ((g * _sigmoid(g)) * up).astype(BF16)
        ffn = ffn + _dot(a, wd_ref[c0:c0 + h_chunk, :])
    h2 = h1 + ffn
    gate = _sigmoid(_dot(h2.astype(BF16), pg_ref[...]))
    h3 = h2 + gate * _dot(p_ref[...].astype(BF16), pp_ref[...])
    if final:
        fvar = jnp.mean(h3 * h3, axis=-1, keepdims=True)
        h3 = (h3 * lax.rsqrt(fvar + NORM_EPS)) * fn_ref[...]
    o_ref[...] = h3


def _dense(h2d, y2d, p_all, weights, final_norm_w, *, layer, final, tm):
    t = h2d.shape[0]
    row = lambda width: pl.BlockSpec((tm, width), lambda i: (i, 0))
    res = lambda a: pl.BlockSpec((None,) + a.shape[1:], lambda i: (layer, 0, 0), pipeline_mode=pl.Buffered(1))
    return pl.pallas_call(
        functools.partial(_dense_kernel, final=final, h_chunk=FFN_CHUNK),
        grid=(t // tm,),
        in_specs=[row(D_MODEL), row(MIX_WIDTH), pl.BlockSpec((None, tm, PLE_DIM), lambda i: (layer, i, 0))]
        + [res(w) for w in weights] + [pl.BlockSpec((1, D_MODEL), lambda i: (0, 0))],
        out_specs=row(D_MODEL),
        out_shape=jax.ShapeDtypeStruct((t, D_MODEL), F32),
        compiler_params=pltpu.CompilerParams(dimension_semantics=("arbitrary",),
                                             vmem_limit_bytes=V7X_VMEM_LIMIT),
        name="dense",
    )(h2d, y2d, p_all, *weights, final_norm_w.reshape(1, D_MODEL).astype(F32))


def _pack_w_in(w):
    z, xbc, dt, dq, dk, dv, mq, mk, mv, mo, mi, mf = jnp.split(w, np.cumsum(IN_SPLITS)[:-1], axis=-1)
    zeros = lambda n: jnp.zeros(w.shape[:-1] + (n,), w.dtype)
    small_a = jnp.concatenate([dt, mi, zeros(128 - LANE_ML - MLSTM_HEADS)], axis=-1)
    small_b = jnp.concatenate([zeros(LANE_ML), mf, zeros(128 - LANE_ML - MLSTM_HEADS)], axis=-1)
    return jnp.concatenate([xbc, z, mo, small_a, small_b, dq, dk, dv, mq, mk, mv], axis=-1).astype(BF16)


def _lanes(depth, *placed):
    row = jnp.zeros((depth, 128), F32)
    for off, v in placed:
        row = row.at[:, off:off + v.shape[1]].set(v.astype(F32))
    return row[:, None, :]


def kernel(x, p, norm1_w, w_in, ssd_conv_w, ssd_conv_b, ssd_dt_bias, ssd_a_log, ssd_d, ssd_norm_w, diff_lq1, diff_lk1, diff_lq2, diff_lk2, diff_norm_w, rel_bias, mlstm_i_bias, mlstm_f_bias, mlstm_norm_w, w_out, norm2_w, w_ffn_gate, w_ffn_up, w_ffn_down, ple_gate_w, ple_proj_w, final_norm_w):
    bsz, s, d = x.shape
    depth = w_in.shape[0]
    assert d == D_MODEL and s % (KEY_GROUP * CHUNK) == 0
    t = bsz * s
    tm = ROW_TILE if s % ROW_TILE == 0 else CHUNK
    row3 = lambda a: a.astype(F32)[:, None, :]

    lam_inits = [0.8 - 0.6 * math.exp(-0.3 * i) for i in range(depth)]
    lam = (jnp.exp(jnp.sum(diff_lq1.astype(F32) * diff_lk1.astype(F32), axis=-1))
           - jnp.exp(jnp.sum(diff_lq2.astype(F32) * diff_lk2.astype(F32), axis=-1))
           + jnp.asarray(lam_inits, F32))
    mixer_params = [
        row3(norm1_w), _pack_w_in(w_in), ssd_conv_w.astype(F32), row3(ssd_conv_b),
        _lanes(depth, (LANE_SSD, ssd_dt_bias), (LANE_ML, mlstm_i_bias)), _lanes(depth, (LANE_ML, mlstm_f_bias)),
        _lanes(depth, (LANE_SSD, ssd_a_log)),
        row3(jnp.repeat(ssd_d, SSD_HEAD_DIM, axis=1)), row3(ssd_norm_w),
        row3(jnp.tile(diff_norm_w, (1, DIFF_HEADS))), row3(mlstm_norm_w),
    ]
    dense_weights = [w_out.astype(BF16), row3(norm2_w), w_ffn_gate.astype(BF16), w_ffn_up.astype(BF16),
                     w_ffn_down.astype(BF16), ple_gate_w.astype(BF16), ple_proj_w.astype(BF16)]
    p_all = p.reshape(depth, t, PLE_DIM).astype(F32)

    h = x.reshape(t, d).astype(F32)
    for i in range(depth):
        y = _mixer(h.reshape(bsz, s, d), lam, rel_bias, mixer_params, layer=i, lam_init=lam_inits[i])
        h = _dense(h, y.reshape(t, MIX_WIDTH), p_all, dense_weights, final_norm_w, layer=i,
                   final=(i == depth - 1), tm=tm)
    return h.reshape(bsz, s, d).astype(x.dtype)
```

```python
import functools
import math

import numpy as np
import jax
import jax.numpy as jnp
from jax import lax
from jax.experimental import pallas as pl
from jax.experimental.pallas import tpu as pltpu

F32 = jnp.float32
BF16 = jnp.bfloat16

D_MODEL = 1024
PLE_DIM = 256
NORM_EPS = 1e-6
CHUNK = 128
KEY_GROUP = 4

SSD_HEADS = 8
SSD_HEAD_DIM = 64
SSD_WIDTH = 512
SSD_GROUPS = 2
SSD_STATE = 128
SSD_CONV = 4
SSD_CONV_CH = 1024
DIFF_HEADS = 4
DIFF_QK_DIM = 32
DIFF_V_DIM = 64
DIFF_WIDTH = 256
REL_BUCKETS = 32
REL_MAX_DIST = 128
MLSTM_HEADS = 4
MLSTM_HEAD_DIM = 64
MLSTM_WIDTH = 256
MIX_WIDTH = 1024
FFN_HIDDEN = 2816
IN_SPLITS = (512, 1024, 8, 256, 256, 256, 256, 256, 256, 256, 4, 4)

W_XBC, W_ZMO, W_SMALL, W_QKV = 0, 1024, 1792, 2048
PROJ_COLS = W_QKV + 6 * 256
ST_Z, ST_MO, ST_SMALL_A, ST_SMALL_B, ST_F_COLS = 0, 512, 768, 896, 1024
ST_Q, ST_MQ, ST_MK, ST_MV, ST_B_COLS = 0, 256, 512, 768, 1024
LANE_SSD, LANE_ML = 0, 8
ATTN_C1 = (DIFF_QK_DIM ** -0.5) * math.log2(math.e)

ROW_TILE = 512
FFN_CHUNK = 1408

V7X_VMEM_LIMIT = 56 * 1024 * 1024
MASKED = -1e30
LOG2E = math.log2(math.e)


def _t5_bucket_lower_bounds():
    max_exact = REL_BUCKETS // 2
    d = np.arange(0, 2 * CHUNK, dtype=np.int64)
    df = np.maximum(d, 1).astype(np.float32)
    large = max_exact + (np.log(df / np.float32(max_exact)) / np.float32(math.log(REL_MAX_DIST / max_exact))
                         * np.float32(REL_BUCKETS - max_exact)).astype(np.int32)
    large = np.minimum(large, REL_BUCKETS - 1)
    bucket = np.where(d < max_exact, d, large)
    assert np.all(np.diff(bucket) >= 0) and bucket[CHUNK] == REL_BUCKETS - 1
    return [int(np.argmax(bucket >= b)) for b in range(REL_BUCKETS)]


_BUCKET_LO = _t5_bucket_lower_bounds()


def _sigmoid(x):
    return 0.5 * jnp.tanh(0.5 * x) + 0.5


def _softplus(x):
    return jnp.maximum(x, 0.0) + jnp.log(1.0 + jnp.exp(-jnp.abs(x)))


def _dot(a, b):
    return jnp.dot(a, b, preferred_element_type=F32)


def _dot_nt(a, b):
    return lax.dot_general(a, b, (((1,), (1,)), ((), ())), preferred_element_type=F32)


def _dot_tn(a, b):
    return lax.dot_general(a, b, (((0,), (0,)), ((), ())), preferred_element_type=F32)


def _scan_rows(x, combine, identity):
    sub = lax.broadcasted_iota(jnp.int32, (8, x.shape[1]), 0)
    tiles, carry = [], None
    for i in range(x.shape[0] // 8):
        t = x[8 * i:8 * (i + 1)]
        for sh in (1, 2, 4):
            t = combine(t, jnp.where(sub >= sh, pltpu.roll(t, sh, 0), identity))
        if carry is not None:
            t = combine(t, carry)
        carry = t[7:8]
        tiles.append(t)
    return jnp.concatenate(tiles, axis=0)


def _mixer_kernel(lam_ref, relb_ref, hc_ref, hn_ref, n1_ref, w_ref, convw_ref, convb_ref,
                  biasa_ref, biasb_ref, alog_ref, dfull_ref, ssdnw_ref, dnw_ref, mnw_ref,
                  y_ref,
                  prev_t, ct, nm, mst, kb, vb, bias, qbd, rstat, acc_s, s_even, s_odd,
                  xpad_a, stf_a, stb_a, xpad_b, stf_b, stb_b,
                  *, layer, lam_init, n_blocks):
    L = CHUNK
    b = pl.program_id(0)
    cp = pl.program_id(1)

    lane128 = lax.broadcasted_iota(jnp.int32, (L, 128), 1)
    lo = lane128 < 64
    row_i = lax.broadcasted_iota(jnp.int32, (L, L), 0)
    col_i = lax.broadcasted_iota(jnp.int32, (L, L), 1)
    tril = row_i >= col_i
    row_lo = row_i < 64

    first_step = jnp.logical_and(b == 0, cp == 0)

    @pl.when(first_step)
    def _():
        for h in range(DIFF_HEADS):
            bias[0, h] = jnp.full((L, L), MASKED, F32)
            bias[3, h] = jnp.zeros((L, L), F32)
            far = relb_ref[REL_BUCKETS - 1, h]
            for which, off in ((1, 0), (2, L)):
                dist = row_i - col_i + off
                v = jnp.full((L, L), far, F32)
                for bkt in range(REL_BUCKETS - 2, -1, -1):
                    v = jnp.where(dist < _BUCKET_LO[bkt + 1], relb_ref[bkt, h], v)
                bias[which, h] = jnp.where(dist >= 0, (v - far) * LOG2E, MASKED)

    @pl.when(cp == 0)
    def _():
        prev_t[...] = jnp.zeros(prev_t.shape, F32)
        ct[...] = jnp.zeros(ct.shape, F32)
        nm[...] = jnp.zeros(nm.shape, F32)
        mst[...] = jnp.zeros(mst.shape, F32)

    def normed(x):
        var = jnp.mean(x * x, axis=-1, keepdims=True)
        return ((x * lax.rsqrt(var + NORM_EPS)) * n1_ref[...]).astype(BF16)

    def project_a(u, xpad, stf):
        xpad[8:8 + L, :] = _dot(u, w_ref[:, W_XBC:W_ZMO])
        zmo = _dot(u, w_ref[:, W_ZMO:W_SMALL])
        z = zmo[:, :SSD_WIDTH]
        stf[:, ST_Z:ST_MO] = z * _sigmoid(z)
        stf[:, ST_MO:ST_SMALL_A] = _sigmoid(zmo[:, SSD_WIDTH:])

    def project_b(u, stf, stb, slot, row0):
        stf[:, ST_SMALL_A:] = _dot(u, w_ref[:, W_SMALL:W_QKV])
        qkv = _dot(u, w_ref[:, W_QKV:])
        stb[:, ST_Q:ST_MQ] = (qkv[:, 0:256] * ATTN_C1).astype(BF16)
        stb[:, ST_MQ:] = qkv[:, 768:].astype(BF16)
        kb[slot, pl.ds(row0, L), :] = qkv[:, 256:512].astype(BF16)
        v = qkv[:, 512:768].astype(BF16)
        for h in range(DIFF_HEADS):
            own = lo if h % 2 == 0 else jnp.logical_not(lo)
            vb[slot, h, pl.ds(row0, L), :] = jnp.where(own, v[:, 128 * (h // 2):128 * (h // 2 + 1)],
                                                       jnp.ones((), BF16))

    slot = lax.rem(b, 2)

    @pl.when(first_step)
    def _():
        xpad_a[...] = jnp.zeros(xpad_a.shape, F32)
        xpad_b[...] = jnp.zeros(xpad_b.shape, F32)
        kb[...] = jnp.zeros(kb.shape, BF16)
        vb[...] = jnp.zeros(vb.shape, BF16)
        u0 = normed(hc_ref[0, 0:L, :])
        project_a(u0, xpad_a, stf_a)
        project_b(u0, stf_a, stb_a, 0, 0)

    def chunk_block(c, y_rows, xpad, stf, stb, xpad_other, stf_other, stb_other, x_next, slot_next, c_next):
        xpad[0:8, :] = jnp.where(c > 0, xpad_other[L:L + 8, :], 0.0)
        u_next = normed(x_next)
        project_a(u_next, xpad_other, stf_other)

        pre_a = stf[:, ST_SMALL_A:ST_SMALL_B] + biasa_ref[...]
        log_f = -_softplus(-(stf[:, ST_SMALL_B:] + biasb_ref[...]))
        ssd_lane = lane128 < LANE_ML
        a_neg = jnp.where(ssd_lane[0:1], -jnp.exp(alog_ref[...]), 0.0)
        dt_c = _softplus(pre_a)
        csum = _scan_rows(jnp.where(ssd_lane, dt_c * a_neg, log_f), jnp.add, 0.0)
        last = csum[L - 1:L]
        u_c = pre_a - csum
        m_intra = csum + _scan_rows(u_c, jnp.maximum, -jnp.inf)
        w_end = last - csum + pre_a
        m_loc = jnp.max(w_end, axis=0, keepdims=True)
        e_end = jnp.exp(w_end - m_loc)
        m_prev = mst[0:1]
        inter_log = csum + m_prev
        m_t = jnp.maximum(inter_log, m_intra)
        w_inter = jnp.exp(inter_log - m_t)
        a_col = csum - m_t
        en = jnp.exp(-m_t)
        m_new = jnp.maximum(last + m_prev, m_loc)
        a_prev = jnp.exp(last + m_prev - m_new)
        a_loc = jnp.exp(m_loc - m_new)
        rows_t = jnp.where(ssd_lane, csum, u_c).T

        def bcast(arr, j):
            return jnp.broadcast_to(arr[:, j:j + 1], (arr.shape[0], 128))

        def pair(arr, j0, k):
            return jnp.where(lo[:arr.shape[0]], bcast(arr, j0 + 2 * k), bcast(arr, j0 + 2 * k + 1))

        conv = convb_ref[...]
        for k in range(SSD_CONV):
            conv = conv + convw_ref[k:k + 1, :] * xpad[5 + k:5 + k + L, :]
        xc = conv * _sigmoid(conv)
        xs = xc[:, :SSD_WIDTH]
        bmat = xc[:, SSD_WIDTH:SSD_WIDTH + 256]
        cmat = xc[:, SSD_WIDTH + 256:]
        dfull = dfull_ref[...]
        cs_b = [bcast(csum, LANE_SSD + h) for h in range(SSD_HEADS)]
        xdt, xds, ecs, cdec = [], [], [], []
        for k in range(4):
            cs_p = jnp.where(lo, cs_b[2 * k], cs_b[2 * k + 1])
            last_p = pair(last, LANE_SSD, k)
            xdt.append(xs[:, 128 * k:128 * (k + 1)] * pair(dt_c, LANE_SSD, k))
            xds.append((xdt[k] * jnp.exp(last_p - cs_p)).astype(BF16))
            ecs.append(jnp.exp(cs_p))
            cdec.append(jnp.exp(last_p))

        y_slabs = []
        for g_ in range(SSD_GROUPS):
            bm_g = bmat[:, 128 * g_:128 * (g_ + 1)]
            cm_b = cmat[:, 128 * g_:128 * (g_ + 1)].astype(BF16)
            cb = _dot_nt(cm_b, bm_g.astype(BF16))
            s_loc = _dot_tn(bm_g.astype(BF16), jnp.concatenate(xds[2 * g_:2 * g_ + 2], axis=1))
            prev = prev_t[g_]
            y_off = _dot(cm_b, prev.astype(BF16))
            for kk in range(2):
                k = 2 * g_ + kk
                slab = xdt[k].astype(BF16)
                halves = []
                for hl in range(2):
                    h = 2 * k + hl
                    diff = cs_b[h] - rows_t[LANE_SSD + h:LANE_SSD + h + 1, :]
                    dec = jnp.exp(jnp.where(tril, diff, -jnp.inf))
                    halves.append(_dot((cb * dec).astype(BF16), slab))
                y_diag = jnp.where(lo, halves[0], halves[1])
                y_slabs.append(y_diag + y_off[:, 128 * kk:128 * (kk + 1)] * ecs[k]
                               + xs[:, 128 * k:128 * (k + 1)] * dfull[:, 128 * k:128 * (k + 1)])
            prev_t[g_] = prev * jnp.concatenate(cdec[2 * g_:2 * g_ + 2], axis=1) + s_loc
        y = jnp.concatenate(y_slabs, axis=1)
        yz = y * stf[:, ST_Z:ST_MO]
        var = jnp.mean(yz * yz, axis=-1, keepdims=True)
        y_ref[0, y_rows, 0:SSD_WIDTH] = ((yz * lax.rsqrt(var + NORM_EPS)) * ssdnw_ref[...]).astype(BF16)

        project_b(u_next, stf_other, stb_other, slot_next, pl.multiple_of(c_next * L, L))

        bd_mask = row_lo == (col_i < 64)
        inv_sqrt_d = MLSTM_HEAD_DIM ** -0.5
        ones_b = jnp.ones((L, 128), BF16)
        mq = stb[:, ST_MQ:ST_MK]
        mk = stb[:, ST_MK:ST_MV]
        mv = stb[:, ST_MV:]
        mo = stf[:, ST_MO:ST_SMALL_A]
        mnw = mnw_ref[...]
        m_slabs = []
        for k in range(2):
            sl = slice(128 * k, 128 * (k + 1))
            q_b = mq[:, sl]
            k_b = mk[:, sl]
            v_b = mv[:, sl]
            v1 = jnp.concatenate([v_b, ones_b], axis=1)
            intra = []
            for hl in range(2):
                h = 2 * k + hl
                qm = jnp.where(lo if hl == 0 else jnp.logical_not(lo), q_b, jnp.zeros((), BF16))
                qk = _dot_nt(qm, k_b)
                arg = bcast(a_col, LANE_ML + h) + rows_t[LANE_ML + h:LANE_ML + h + 1, :]
                w_intra = jnp.exp(jnp.where(tril, arg, -jnp.inf))
                intra.append(_dot(((qk * inv_sqrt_d) * w_intra).astype(BF16), v1))
            num_intra = jnp.where(lo, intra[0][:, :128], intra[1][:, :128])
            den_intra = jnp.where(lo, intra[0][:, 128:], intra[1][:, 128:])
            ct_prev = ct[k]
            nm_prev = nm[k]
            wi = pair(w_inter, LANE_ML, k)
            num = num_intra + wi * _dot(q_b, ct_prev.astype(BF16))
            den = den_intra + wi * _dot(q_b, nm_prev.astype(BF16))
            denom = jnp.maximum(jnp.abs(den), pair(en, LANE_ML, k))
            hs = num / denom
            sq = hs * hs
            s_lo = jnp.sum(jnp.where(lo, sq, 0.0), axis=1, keepdims=True)
            s_hi = jnp.sum(jnp.where(lo, 0.0, sq), axis=1, keepdims=True)
            hvar = jnp.where(lo, s_lo, s_hi) * (1.0 / MLSTM_HEAD_DIM)
            hn = (hs * lax.rsqrt(hvar + NORM_EPS)) * mnw[:, sl]
            m_slabs.append(mo[:, sl] * hn)
            e_p = pair(e_end, LANE_ML, k)
            loc = _dot_tn(k_b, jnp.concatenate([(v_b.astype(F32) * e_p).astype(BF16), e_p.astype(BF16)], axis=1)
                          ) * inv_sqrt_d
            ap = pair(a_prev, LANE_ML, k)
            al = pair(a_loc, LANE_ML, k)
            ct[k] = ct_prev * ap + jnp.where(bd_mask, loc[:, :128], 0.0) * al
            nm[k] = nm_prev * ap + jnp.where(bd_mask, loc[:, 128:], 0.0) * al
        mst[...] = jnp.broadcast_to(m_new, mst.shape)
        y_ref[0, y_rows, SSD_WIDTH + DIFF_WIDTH:] = jnp.concatenate(m_slabs, axis=1).astype(BF16)

        q = stb[:, ST_Q:ST_MQ]
        zero_b = jnp.zeros((), BF16)
        for h in range(DIFF_HEADS):
            qs = q[:, 128 * (h // 2):128 * (h // 2 + 1)]
            base = 64 * (h % 2)
            q0 = jnp.where((lane128 >= base) & (lane128 < base + 32), qs, zero_b)
            q1 = jnp.where((lane128 >= base + 32) & (lane128 < base + 64), qs, zero_b)
            qbd[h] = jnp.concatenate([q0, q1], axis=0)

    lam = lam_ref[layer]

    def attention(c, y_rows):
        n_groups = c // KEY_GROUP + 1
        gw = KEY_GROUP * L

        def fill_scores(kg, dst):
            r0 = pl.multiple_of(kg * gw, gw)
            for h in range(DIFF_HEADS):
                keys = kb[slot, pl.ds(r0, gw), 128 * (h // 2):128 * (h // 2 + 1)]
                add = jnp.concatenate(
                    [bias[jnp.clip(c - KEY_GROUP * kg - j, -1, 2) + 1, h] for j in range(KEY_GROUP)], axis=1)
                dst[h] = _dot_nt(qbd[h], keys) + jnp.concatenate([add, add], axis=0)

        def consume(kg, src):
            r0 = pl.multiple_of(kg * gw, gw)
            for h in range(DIFF_HEADS):
                s = src[h]
                t = s[:, :L]
                for j in range(1, KEY_GROUP):
                    t = jnp.maximum(t, s[:, L * j:L * (j + 1)])
                m_old = rstat[h]
                m_new = jnp.maximum(m_old, jnp.max(t, axis=1, keepdims=True))
                p = jnp.concatenate([jnp.exp2(s[:, L * j:L * (j + 1)] - m_new) for j in range(KEY_GROUP)], axis=1)
                acc_s[h] = acc_s[h] * jnp.exp2(m_old - m_new) + _dot(p.astype(BF16), vb[slot, h, pl.ds(r0, gw), :])
                rstat[h] = m_new

        rstat[...] = jnp.full(rstat.shape, MASKED, F32)
        acc_s[...] = jnp.zeros(acc_s.shape, F32)
        fill_scores(0, s_even)

        def key_step(kg, carry):
            @pl.when(kg % 2 == 0)
            def _():
                fill_scores(kg + 1, s_odd)
                consume(kg, s_even)

            @pl.when(kg % 2 == 1)
            def _():
                fill_scores(kg + 1, s_even)
                consume(kg, s_odd)

            return carry

        g_last = n_groups - 1
        lax.fori_loop(0, g_last, key_step, 0)

        @pl.when(g_last % 2 == 0)
        def _():
            consume(g_last, s_even)

        @pl.when(g_last % 2 == 1)
        def _():
            consume(g_last, s_odd)

        d_slabs = []
        for k in range(2):
            a0, a1 = acc_s[2 * k], acc_s[2 * k + 1]
            lo2 = jnp.concatenate([lo, lo], axis=0)
            r = jnp.where(lo2, a0, a1) * (1.0 / pltpu.roll(jnp.where(lo2, a1, a0), 64, 1))
            o = r[:L] - lam * r[L:]
            sq = o * o
            s_lo = jnp.sum(jnp.where(lo, sq, 0.0), axis=1, keepdims=True)
            s_hi = jnp.sum(jnp.where(lo, 0.0, sq), axis=1, keepdims=True)
            ovar = jnp.where(lo, s_lo, s_hi) * (1.0 / DIFF_V_DIM)
            d_slabs.append(((o * lax.rsqrt(ovar + NORM_EPS)) * dnw_ref[:, 128 * k:128 * (k + 1)])
                           * (1.0 - lam_init))
        y_ref[0, y_rows, SSD_WIDTH:SSD_WIDTH + DIFF_WIDTH] = jnp.concatenate(d_slabs, axis=1).astype(BF16)

    nb = n_blocks
    c0 = 2 * cp
    chunk_block(c0, slice(0, L), xpad_a, stf_a, stb_a, xpad_b, stf_b, stb_b,
                hc_ref[0, L:2 * L, :], slot, c0 + 1)
    attention(c0, slice(0, L))
    chunk_block(c0 + 1, slice(L, 2 * L), xpad_b, stf_b, stb_b, xpad_a, stf_a, stb_a,
                hn_ref[0], lax.rem(b + (c0 + 2) // nb, 2), lax.rem(c0 + 2, nb))
    attention(c0 + 1, slice(L, 2 * L))


def _mixer(h3d, lam, rel_bias, params, *, layer, lam_init):
    bsz, s, _ = h3d.shape
    L = CHUNK
    nb = s // L

    def par(a):
        return pl.BlockSpec((None,) + a.shape[1:], lambda b, c: (layer, 0, 0))

    smem = pl.BlockSpec(memory_space=pltpu.SMEM)
    cur = pl.BlockSpec((1, 2 * L, D_MODEL), lambda b, c: (b, c, 0))
    nxt = pl.BlockSpec((1, L, D_MODEL),
                       lambda b, c: (jnp.minimum(b + (2 * c + 2) // nb, bsz - 1), lax.rem(2 * c + 2, nb), 0))
    w_spec = pl.BlockSpec((None, D_MODEL, PROJ_COLS), lambda b, c: (layer, 0, 0), pipeline_mode=pl.Buffered(1))
    in_specs = [smem, smem, cur, nxt, par(params[0]), w_spec] + [par(p) for p in params[2:]]
    stage = [
        pltpu.VMEM((L + 8, SSD_CONV_CH), F32),
        pltpu.VMEM((L, ST_F_COLS), F32),
        pltpu.VMEM((L, ST_B_COLS), BF16),
    ]
    scratch = [
        pltpu.VMEM((SSD_GROUPS, SSD_STATE, 256), F32),
        pltpu.VMEM((2, 128, 128), F32),
        pltpu.VMEM((2, 128, 128), F32),
        pltpu.VMEM((8, 128), F32),
        pltpu.VMEM((2, s, 256), BF16),
        pltpu.VMEM((2, DIFF_HEADS, s, 128), BF16),
        pltpu.VMEM((4, DIFF_HEADS, L, L), F32),
        pltpu.VMEM((DIFF_HEADS, 2 * L, 128), BF16),
        pltpu.VMEM((DIFF_HEADS, 2 * L, L), F32),
        pltpu.VMEM((DIFF_HEADS, 2 * L, 128), F32),
        pltpu.VMEM((DIFF_HEADS, 2 * L, KEY_GROUP * L), F32),
        pltpu.VMEM((DIFF_HEADS, 2 * L, KEY_GROUP * L), F32),
    ] + stage + stage
    return pl.pallas_call(
        functools.partial(_mixer_kernel, layer=layer, lam_init=lam_init, n_blocks=nb),
        grid=(bsz, nb // 2),
        in_specs=in_specs,
        out_specs=pl.BlockSpec((1, 2 * L, MIX_WIDTH), lambda b, c: (b, c, 0)),
        out_shape=jax.ShapeDtypeStruct((bsz, s, MIX_WIDTH), BF16),
        scratch_shapes=scratch,
        compiler_params=pltpu.CompilerParams(dimension_semantics=("arbitrary", "arbitrary"),
                                             vmem_limit_bytes=V7X_VMEM_LIMIT),
        name="mixer",
    )(lam, rel_bias.astype(F32), h3d, h3d, *params)


def _dense_kernel(h_ref, y_ref, p_ref, wo_ref, n2_ref, wg_ref, wu_ref, wd_ref, pg_ref, pp_ref, fn_ref,
                  o_ref, *, final, h_chunk):
    h1 = h_ref[...] + _dot(y_ref[...], wo_ref[...])
    var = jnp.mean(h1 * h1, axis=-1, keepdims=True)
    u = ((h1 * lax.rsqrt(var + NORM_EPS)) * n2_ref[...]).astype(BF16)
    ffn = jnp.zeros(h1.shape, F32)
    for c0 in range(0, FFN_HIDDEN, h_chunk):
        g = _dot(u, wg_ref[:, c0:c0 + h_chunk])
        up = _dot(u, wu_ref[:, c0:c0 + h_chunk])
        a = ((g * _sigmoid(g)) * up).astype(BF16)
        ffn = ffn + _dot(a, wd_ref[c0:c0 + h_chunk, :])
    h2 = h1 + ffn
    gate = _sigmoid(_dot(h2.astype(BF16), pg_ref[...]))
    h3 = h2 + gate * _dot(p_ref[...].astype(BF16), pp_ref[...])
    if final:
        fvar = jnp.mean(h3 * h3, axis=-1, keepdims=True)
        h3 = (h3 * lax.rsqrt(fvar + NORM_EPS)) * fn_ref[...]
    o_ref[...] = h3


def _dense(h2d, y2d, p_all, weights, final_norm_w, *, layer, final, tm):
    t = h2d.shape[0]
    row = lambda width: pl.BlockSpec((tm, width), lambda i: (i, 0))
    res = lambda a: pl.BlockSpec((None,) + a.shape[1:], lambda i: (layer, 0, 0), pipeline_mode=pl.Buffered(1))
    return pl.pallas_call(
        functools.partial(_dense_kernel, final=final, h_chunk=FFN_CHUNK),
        grid=(t // tm,),
        in_specs=[row(D_MODEL), row(MIX_WIDTH), pl.BlockSpec((None, tm, PLE_DIM), lambda i: (layer, i, 0))]
        + [res(w) for w in weights] + [pl.BlockSpec((1, D_MODEL), lambda i: (0, 0))],
        out_specs=row(D_MODEL),
        out_shape=jax.ShapeDtypeStruct((t, D_MODEL), F32),
        compiler_params=pltpu.CompilerParams(dimension_semantics=("arbitrary",),
                                             vmem_limit_bytes=V7X_VMEM_LIMIT),
        name="dense",
    )(h2d, y2d, p_all, *weights, final_norm_w.reshape(1, D_MODEL).astype(F32))


def _pack_w_in(w):
    z, xbc, dt, dq, dk, dv, mq, mk, mv, mo, mi, mf = jnp.split(w, np.cumsum(IN_SPLITS)[:-1], axis=-1)
    zeros = lambda n: jnp.zeros(w.shape[:-1] + (n,), w.dtype)
    small_a = jnp.concatenate([dt, mi, zeros(128 - LANE_ML - MLSTM_HEADS)], axis=-1)
    small_b = jnp.concatenate([zeros(LANE_ML), mf, zeros(128 - LANE_ML - MLSTM_HEADS)], axis=-1)
    return jnp.concatenate([xbc, z, mo, small_a, small_b, dq, dk, dv, mq, mk, mv], axis=-1).astype(BF16)


def _lanes(depth, *placed):
    row = jnp.zeros((depth, 128), F32)
    for off, v in placed:
        row = row.at[:, off:off + v.shape[1]].set(v.astype(F32))
    return row[:, None, :]


def kernel(x, p, norm1_w, w_in, ssd_conv_w, ssd_conv_b, ssd_dt_bias, ssd_a_log, ssd_d, ssd_norm_w, diff_lq1, diff_lk1, diff_lq2, diff_lk2, diff_norm_w, rel_bias, mlstm_i_bias, mlstm_f_bias, mlstm_norm_w, w_out, norm2_w, w_ffn_gate, w_ffn_up, w_ffn_down, ple_gate_w, ple_proj_w, final_norm_w):
    bsz, s, d = x.shape
    depth = w_in.shape[0]
    assert d == D_MODEL and s % (KEY_GROUP * CHUNK) == 0
    t = bsz * s
    tm = ROW_TILE if s % ROW_TILE == 0 else CHUNK
    row3 = lambda a: a.astype(F32)[:, None, :]

    lam_inits = [0.8 - 0.6 * math.exp(-0.3 * i) for i in range(depth)]
    lam = (jnp.exp(jnp.sum(diff_lq1.astype(F32) * diff_lk1.astype(F32), axis=-1))
           - jnp.exp(jnp.sum(diff_lq2.astype(F32) * diff_lk2.astype(F32), axis=-1))
           + jnp.asarray(lam_inits, F32))
    mixer_params = [
        row3(norm1_w), _pack_w_in(w_in), ssd_conv_w.astype(F32), row3(ssd_conv_b),
        _lanes(depth, (LANE_SSD, ssd_dt_bias), (LANE_ML, mlstm_i_bias)), _lanes(depth, (LANE_ML, mlstm_f_bias)),
        _lanes(depth, (LANE_SSD, ssd_a_log)),
        row3(jnp.repeat(ssd_d, SSD_HEAD_DIM, axis=1)), row3(ssd_norm_w),
        row3(jnp.tile(diff_norm_w, (1, DIFF_HEADS))), row3(mlstm_norm_w),
    ]
    dense_weights = [w_out.astype(BF16), row3(norm2_w), w_ffn_gate.astype(BF16), w_ffn_up.astype(BF16),
                     w_ffn_down.astype(BF16), ple_gate_w.astype(BF16), ple_proj_w.astype(BF16)]
    p_all = p.reshape(depth, t, PLE_DIM).astype(F32)

    h = x.reshape(t, d).astype(F32)
    for i in range(depth):
        y = _mixer(h.reshape(bsz, s, d), lam, rel_bias, mixer_params, layer=i, lam_init=lam_inits[i])
        h = _dense(h, y.reshape(t, MIX_WIDTH), p_all, dense_weights, final_norm_w, layer=i,
                   final=(i == depth - 1), tm=tm)
    return h.reshape(bsz, s, d).astype(x.dtype)
```

```python
import functools
import math

import numpy as np
import jax
import jax.numpy as jnp
from jax import lax
from jax.experimental import pallas as pl
from jax.experimental.pallas import tpu as pltpu

F32 = jnp.float32
BF16 = jnp.bfloat16

D_MODEL = 1024
PLE_DIM = 256
NORM_EPS = 1e-6
CHUNK = 128
KEY_GROUP = 4
CHUNKS_PER_STEP = 4

SSD_HEADS = 8
SSD_HEAD_DIM = 64
SSD_WIDTH = 512
SSD_GROUPS = 2
SSD_STATE = 128
SSD_CONV = 4
SSD_CONV_CH = 1024
DIFF_HEADS = 4
DIFF_QK_DIM = 32
DIFF_V_DIM = 64
DIFF_WIDTH = 256
REL_BUCKETS = 32
REL_MAX_DIST = 128
MLSTM_HEADS = 4
MLSTM_HEAD_DIM = 64
MLSTM_WIDTH = 256
MIX_WIDTH = 1024
FFN_HIDDEN = 2816
IN_SPLITS = (512, 1024, 8, 256, 256, 256, 256, 256, 256, 256, 4, 4)

W_XBC, W_ZMO, W_SMALL, W_QKV = 0, 1024, 1792, 2048
PROJ_COLS = W_QKV + 6 * 256
ST_Z, ST_MO, ST_SMALL_A, ST_SMALL_B, ST_F_COLS = 0, 512, 768, 896, 1024
ST_Q, ST_MQ, ST_MK, ST_MV, ST_B_COLS = 0, 256, 512, 768, 1024
LANE_SSD, LANE_ML = 0, 8
ATTN_C1 = (DIFF_QK_DIM ** -0.5) * math.log2(math.e)

ROW_TILE = 512
FFN_CHUNK = 1408

V7X_VMEM_LIMIT = 56 * 1024 * 1024
MASKED = -1e30
LOG2E = math.log2(math.e)


def _t5_bucket_lower_bounds():
    max_exact = REL_BUCKETS // 2
    d = np.arange(0, 2 * CHUNK, dtype=np.int64)
    df = np.maximum(d, 1).astype(np.float32)
    large = max_exact + (np.log(df / np.float32(max_exact)) / np.float32(math.log(REL_MAX_DIST / max_exact))
                         * np.float32(REL_BUCKETS - max_exact)).astype(np.int32)
    large = np.minimum(large, REL_BUCKETS - 1)
    bucket = np.where(d < max_exact, d, large)
    assert np.all(np.diff(bucket) >= 0) and bucket[CHUNK] == REL_BUCKETS - 1
    return [int(np.argmax(bucket >= b)) for b in range(REL_BUCKETS)]


_BUCKET_LO = _t5_bucket_lower_bounds()


def _sigmoid(x):
    return 0.5 * jnp.tanh(0.5 * x) + 0.5


def _softplus(x):
    return jnp.maximum(x, 0.0) + jnp.log(1.0 + jnp.exp(-jnp.abs(x)))


def _dot(a, b):
    return jnp.dot(a, b, preferred_element_type=F32)


def _dot_nt(a, b):
    return lax.dot_general(a, b, (((1,), (1,)), ((), ())), preferred_element_type=F32)


def _dot_tn(a, b):
    return lax.dot_general(a, b, (((0,), (0,)), ((), ())), preferred_element_type=F32)


def _scan_rows(x, combine, identity):
    sub = lax.broadcasted_iota(jnp.int32, (8, x.shape[1]), 0)
    tiles, carry = [], None
    for i in range(x.shape[0] // 8):
        t = x[8 * i:8 * (i + 1)]
        for sh in (1, 2, 4):
            t = combine(t, jnp.where(sub >= sh, pltpu.roll(t, sh, 0), identity))
        if carry is not None:
            t = combine(t, carry)
        carry = t[7:8]
        tiles.append(t)
    return jnp.concatenate(tiles, axis=0)


def _mixer_kernel(lam_ref, relb_ref, hc_ref, hn_ref, n1_ref, w_ref, convw_ref, convb_ref,
                  biasa_ref, biasb_ref, alog_ref, dfull_ref, ssdnw_ref, dnw_ref, mnw_ref,
                  y_ref,
                  prev_t, ct, nm, mst, kb, vb, bias, qbd, rstat, acc_s, s_even, s_odd,
                  xpad_a, stf_a, stb_a, xpad_b, stf_b, stb_b,
                  *, layer, lam_init, n_blocks):
    L = CHUNK
    b = pl.program_id(0)
    cp = pl.program_id(1)

    lane128 = lax.broadcasted_iota(jnp.int32, (L, 128), 1)
    lo = lane128 < 64
    row_i = lax.broadcasted_iota(jnp.int32, (L, L), 0)
    col_i = lax.broadcasted_iota(jnp.int32, (L, L), 1)
    tril = row_i >= col_i
    row_lo = row_i < 64

    first_step = jnp.logical_and(b == 0, cp == 0)

    @pl.when(first_step)
    def _():
        for h in range(DIFF_HEADS):
            bias[0, h] = jnp.full((L, L), MASKED, F32)
            bias[3, h] = jnp.zeros((L, L), F32)
            far = relb_ref[REL_BUCKETS - 1, h]
            for which, off in ((1, 0), (2, L)):
                dist = row_i - col_i + off
                v = jnp.full((L, L), far, F32)
                for bkt in range(REL_BUCKETS - 2, -1, -1):
                    v = jnp.where(dist < _BUCKET_LO[bkt + 1], relb_ref[bkt, h], v)
                bias[which, h] = jnp.where(dist >= 0, (v - far) * LOG2E, MASKED)

    @pl.when(cp == 0)
    def _():
        prev_t[...] = jnp.zeros(prev_t.shape, F32)
        ct[...] = jnp.zeros(ct.shape, F32)
        nm[...] = jnp.zeros(nm.shape, F32)
        mst[...] = jnp.zeros(mst.shape, F32)

    def normed(x):
        var = jnp.mean(x * x, axis=-1, keepdims=True)
        return ((x * lax.rsqrt(var + NORM_EPS)) * n1_ref[...]).astype(BF16)

    def project_a(u, xpad, stf):
        xpad[8:8 + L, :] = _dot(u, w_ref[:, W_XBC:W_ZMO])
        zmo = _dot(u, w_ref[:, W_ZMO:W_SMALL])
        z = zmo[:, :SSD_WIDTH]
        stf[:, ST_Z:ST_MO] = z * _sigmoid(z)
        stf[:, ST_MO:ST_SMALL_A] = _sigmoid(zmo[:, SSD_WIDTH:])

    def project_b(u, stf, stb, slot, row0):
        stf[:, ST_SMALL_A:] = _dot(u, w_ref[:, W_SMALL:W_QKV])
        qkv = _dot(u, w_ref[:, W_QKV:])
        stb[:, ST_Q:ST_MQ] = (qkv[:, 0:256] * ATTN_C1).astype(BF16)
        stb[:, ST_MQ:] = qkv[:, 768:].astype(BF16)
        kb[slot, pl.ds(row0, L), :] = qkv[:, 256:512].astype(BF16)
        v = qkv[:, 512:768].astype(BF16)
        for h in range(DIFF_HEADS):
            own = lo if h % 2 == 0 else jnp.logical_not(lo)
            vb[slot, h, pl.ds(row0, L), :] = jnp.where(own, v[:, 128 * (h // 2):128 * (h // 2 + 1)],
                                                       jnp.ones((), BF16))

    slot = lax.rem(b, 2)

    @pl.when(first_step)
    def _():
        xpad_a[...] = jnp.zeros(xpad_a.shape, F32)
        xpad_b[...] = jnp.zeros(xpad_b.shape, F32)
        kb[...] = jnp.zeros(kb.shape, BF16)
        vb[...] = jnp.zeros(vb.shape, BF16)
        u0 = normed(hc_ref[0, 0:L, :])
        project_a(u0, xpad_a, stf_a)
        project_b(u0, stf_a, stb_a, 0, 0)

    def chunk_block(c, y_rows, xpad, stf, stb, xpad_other, stf_other, stb_other, x_next, slot_next, c_next):
        xpad[0:8, :] = jnp.where(c > 0, xpad_other[L:L + 8, :], 0.0)
        u_next = normed(x_next)
        project_a(u_next, xpad_other, stf_other)

        pre_a = stf[:, ST_SMALL_A:ST_SMALL_B] + biasa_ref[...]
        log_f = -_softplus(-(stf[:, ST_SMALL_B:] + biasb_ref[...]))
        ssd_lane = lane128 < LANE_ML
        a_neg = jnp.where(ssd_lane[0:1], -jnp.exp(alog_ref[...]), 0.0)
        dt_c = _softplus(pre_a)
        csum = _scan_rows(jnp.where(ssd_lane, dt_c * a_neg, log_f), jnp.add, 0.0)
        last = csum[L - 1:L]
        u_c = pre_a - csum
        m_intra = csum + _scan_rows(u_c, jnp.maximum, -jnp.inf)
        w_end = last - csum + pre_a
        m_loc = jnp.max(w_end, axis=0, keepdims=True)
        e_end = jnp.exp(w_end - m_loc)
        m_prev = mst[0:1]
        inter_log = csum + m_prev
        m_t = jnp.maximum(inter_log, m_intra)
        w_inter = jnp.exp(inter_log - m_t)
        a_col = csum - m_t
        en = jnp.exp(-m_t)
        m_new = jnp.maximum(last + m_prev, m_loc)
        a_prev = jnp.exp(last + m_prev - m_new)
        a_loc = jnp.exp(m_loc - m_new)
        rows_t = jnp.where(ssd_lane, csum, u_c).T

        def bcast(arr, j):
            return jnp.broadcast_to(arr[:, j:j + 1], (arr.shape[0], 128))

        def pair(arr, j0, k):
            return jnp.where(lo[:arr.shape[0]], bcast(arr, j0 + 2 * k), bcast(arr, j0 + 2 * k + 1))

        conv = convb_ref[...]
        for k in range(SSD_CONV):
            conv = conv + convw_ref[k:k + 1, :] * xpad[5 + k:5 + k + L, :]
        xc = conv * _sigmoid(conv)
        xs = xc[:, :SSD_WIDTH]
        bmat = xc[:, SSD_WIDTH:SSD_WIDTH + 256]
        cmat = xc[:, SSD_WIDTH + 256:]
        dfull = dfull_ref[...]
        cs_b = [bcast(csum, LANE_SSD + h) for h in range(SSD_HEADS)]
        xdt, xds, ecs, cdec = [], [], [], []
        for k in range(4):
            cs_p = jnp.where(lo, cs_b[2 * k], cs_b[2 * k + 1])
            last_p = pair(last, LANE_SSD, k)
            xdt.append(xs[:, 128 * k:128 * (k + 1)] * pair(dt_c, LANE_SSD, k))
            xds.append((xdt[k] * jnp.exp(last_p - cs_p)).astype(BF16))
            ecs.append(jnp.exp(cs_p))
            cdec.append(jnp.exp(last_p))

        y_slabs = []
        for g_ in range(SSD_GROUPS):
            bm_g = bmat[:, 128 * g_:128 * (g_ + 1)]
            cm_b = cmat[:, 128 * g_:128 * (g_ + 1)].astype(BF16)
            cb = _dot_nt(cm_b, bm_g.astype(BF16))
            s_loc = _dot_tn(bm_g.astype(BF16), jnp.concatenate(xds[2 * g_:2 * g_ + 2], axis=1))
            prev = prev_t[g_]
            y_off = _dot(cm_b, prev.astype(BF16))
            for kk in range(2):
                k = 2 * g_ + kk
                slab = xdt[k].astype(BF16)
                halves = []
                for hl in range(2):
                    h = 2 * k + hl
                    diff = cs_b[h] - rows_t[LANE_SSD + h:LANE_SSD + h + 1, :]
                    dec = jnp.exp(jnp.where(tril, diff, -jnp.inf))
                    halves.append(_dot((cb * dec).astype(BF16), slab))
                y_diag = jnp.where(lo, halves[0], halves[1])
                y_slabs.append(y_diag + y_off[:, 128 * kk:128 * (kk + 1)] * ecs[k]
                               + xs[:, 128 * k:128 * (k + 1)] * dfull[:, 128 * k:128 * (k + 1)])
            prev_t[g_] = prev * jnp.concatenate(cdec[2 * g_:2 * g_ + 2], axis=1) + s_loc
        y = jnp.concatenate(y_slabs, axis=1)
        yz = y * stf[:, ST_Z:ST_MO]
        var = jnp.mean(yz * yz, axis=-1, keepdims=True)
        y_ref[0, y_rows, 0:SSD_WIDTH] = ((yz * lax.rsqrt(var + NORM_EPS)) * ssdnw_ref[...]).astype(BF16)

        project_b(u_next, stf_other, stb_other, slot_next, pl.multiple_of(c_next * L, L))

        bd_mask = row_lo == (col_i < 64)
        inv_sqrt_d = MLSTM_HEAD_DIM ** -0.5
        ones_b = jnp.ones((L, 128), BF16)
        mq = stb[:, ST_MQ:ST_MK]
        mk = stb[:, ST_MK:ST_MV]
        mv = stb[:, ST_MV:]
        mo = stf[:, ST_MO:ST_SMALL_A]
        mnw = mnw_ref[...]
        m_slabs = []
        for k in range(2):
            sl = slice(128 * k, 128 * (k + 1))
            q_b = mq[:, sl]
            k_b = mk[:, sl]
            v_b = mv[:, sl]
            v1 = jnp.concatenate([v_b, ones_b], axis=1)
            intra = []
            for hl in range(2):
                h = 2 * k + hl
                qm = jnp.where(lo if hl == 0 else jnp.logical_not(lo), q_b, jnp.zeros((), BF16))
                qk = _dot_nt(qm, k_b)
                arg = bcast(a_col, LANE_ML + h) + rows_t[LANE_ML + h:LANE_ML + h + 1, :]
                w_intra = jnp.exp(jnp.where(tril, arg, -jnp.inf))
                intra.append(_dot(((qk * inv_sqrt_d) * w_intra).astype(BF16), v1))
            num_intra = jnp.where(lo, intra[0][:, :128], intra[1][:, :128])
            den_intra = jnp.where(lo, intra[0][:, 128:], intra[1][:, 128:])
            ct_prev = ct[k]
            nm_prev = nm[k]
            wi = pair(w_inter, LANE_ML, k)
            num = num_intra + wi * _dot(q_b, ct_prev.astype(BF16))
            den = den_intra + wi * _dot(q_b, nm_prev.astype(BF16))
            denom = jnp.maximum(jnp.abs(den), pair(en, LANE_ML, k))
            hs = num / denom
            sq = hs * hs
            s_lo = jnp.sum(jnp.where(lo, sq, 0.0), axis=1, keepdims=True)
            s_hi = jnp.sum(jnp.where(lo, 0.0, sq), axis=1, keepdims=True)
            hvar = jnp.where(lo, s_lo, s_hi) * (1.0 / MLSTM_HEAD_DIM)
            hn = (hs * lax.rsqrt(hvar + NORM_EPS)) * mnw[:, sl]
            m_slabs.append(mo[:, sl] * hn)
            e_p = pair(e_end, LANE_ML, k)
            loc = _dot_tn(k_b, jnp.concatenate([(v_b.astype(F32) * e_p).astype(BF16), e_p.astype(BF16)], axis=1)
                          ) * inv_sqrt_d
            ap = pair(a_prev, LANE_ML, k)
            al = pair(a_loc, LANE_ML, k)
            ct[k] = ct_prev * ap + jnp.where(bd_mask, loc[:, :128], 0.0) * al
            nm[k] = nm_prev * ap + jnp.where(bd_mask, loc[:, 128:], 0.0) * al
        mst[...] = jnp.broadcast_to(m_new, mst.shape)
        y_ref[0, y_rows, SSD_WIDTH + DIFF_WIDTH:] = jnp.concatenate(m_slabs, axis=1).astype(BF16)

        q = stb[:, ST_Q:ST_MQ]
        zero_b = jnp.zeros((), BF16)
        for h in range(DIFF_HEADS):
            qs = q[:, 128 * (h // 2):128 * (h // 2 + 1)]
            base = 64 * (h % 2)
            q0 = jnp.where((lane128 >= base) & (lane128 < base + 32), qs, zero_b)
            q1 = jnp.where((lane128 >= base + 32) & (lane128 < base + 64), qs, zero_b)
            qbd[h] = jnp.concatenate([q0, q1], axis=0)

    lam = lam_ref[layer]

    def attention(c, y_rows):
        n_groups = c // KEY_GROUP + 1
        gw = KEY_GROUP * L

        def fill_scores(kg, dst):
            r0 = pl.multiple_of(kg * gw, gw)
            for h in range(DIFF_HEADS):
                keys = kb[slot, pl.ds(r0, gw), 128 * (h // 2):128 * (h // 2 + 1)]
                add = jnp.concatenate(
                    [bias[jnp.clip(c - KEY_GROUP * kg - j, -1, 2) + 1, h] for j in range(KEY_GROUP)], axis=1)
                dst[h] = _dot_nt(qbd[h], keys) + jnp.concatenate([add, add], axis=0)

        def consume(kg, src):
            r0 = pl.multiple_of(kg * gw, gw)
            for h in range(DIFF_HEADS):
                s = src[h]
                t = s[:, :L]
                for j in range(1, KEY_GROUP):
                    t = jnp.maximum(t, s[:, L * j:L * (j + 1)])
                m_old = rstat[h]
                m_new = jnp.maximum(m_old, jnp.max(t, axis=1, keepdims=True))
                p = jnp.concatenate([jnp.exp2(s[:, L * j:L * (j + 1)] - m_new) for j in range(KEY_GROUP)], axis=1)
                acc_s[h] = acc_s[h] * jnp.exp2(m_old - m_new) + _dot(p.astype(BF16), vb[slot, h, pl.ds(r0, gw), :])
                rstat[h] = m_new

        rstat[...] = jnp.full(rstat.shape, MASKED, F32)
        acc_s[...] = jnp.zeros(acc_s.shape, F32)
        fill_scores(0, s_even)

        def key_step(kg, carry):
            @pl.when(kg % 2 == 0)
            def _():
                fill_scores(kg + 1, s_odd)
                consume(kg, s_even)

            @pl.when(kg % 2 == 1)
            def _():
                fill_scores(kg + 1, s_even)
                consume(kg, s_odd)

            return carry

        g_last = n_groups - 1
        lax.fori_loop(0, g_last, key_step, 0)

        @pl.when(g_last % 2 == 0)
        def _():
            consume(g_last, s_even)

        @pl.when(g_last % 2 == 1)
        def _():
            consume(g_last, s_odd)

        d_slabs = []
        for k in range(2):
            a0, a1 = acc_s[2 * k], acc_s[2 * k + 1]
            lo2 = jnp.concatenate([lo, lo], axis=0)
            r = jnp.where(lo2, a0, a1) * (1.0 / pltpu.roll(jnp.where(lo2, a1, a0), 64, 1))
            o = r[:L] - lam * r[L:]
            sq = o * o
            s_lo = jnp.sum(jnp.where(lo, sq, 0.0), axis=1, keepdims=True)
            s_hi = jnp.sum(jnp.where(lo, 0.0, sq), axis=1, keepdims=True)
            ovar = jnp.where(lo, s_lo, s_hi) * (1.0 / DIFF_V_DIM)
            d_slabs.append(((o * lax.rsqrt(ovar + NORM_EPS)) * dnw_ref[:, 128 * k:128 * (k + 1)])
                           * (1.0 - lam_init))
        y_ref[0, y_rows, SSD_WIDTH:SSD_WIDTH + DIFF_WIDTH] = jnp.concatenate(d_slabs, axis=1).astype(BF16)

    nb = n_blocks
    sets = ((xpad_a, stf_a, stb_a), (xpad_b, stf_b, stb_b))
    for j in range(CHUNKS_PER_STEP):
        c = CHUNKS_PER_STEP * cp + j
        rows = slice(j * L, (j + 1) * L)
        if j + 1 < CHUNKS_PER_STEP:
            nxt = (hc_ref[0, (j + 1) * L:(j + 2) * L, :], slot, c + 1)
        else:
            nxt = (hn_ref[0], lax.rem(b + (c + 1) // nb, 2), lax.rem(c + 1, nb))
        chunk_block(c, rows, *sets[j % 2], *sets[(j + 1) % 2], *nxt)
        attention(c, rows)


def _mixer(h3d, lam, rel_bias, params, *, layer, lam_init):
    bsz, s, _ = h3d.shape
    L = CHUNK
    nb = s // L

    def par(a):
        return pl.BlockSpec((None,) + a.shape[1:], lambda b, c: (layer, 0, 0))

    smem = pl.BlockSpec(memory_space=pltpu.SMEM)
    cps = CHUNKS_PER_STEP
    cur = pl.BlockSpec((1, cps * L, D_MODEL), lambda b, c: (b, c, 0))
    nxt = pl.BlockSpec((1, L, D_MODEL),
                       lambda b, c: (jnp.minimum(b + (cps * c + cps) // nb, bsz - 1), lax.rem(cps * c + cps, nb), 0))
    w_spec = pl.BlockSpec((None, D_MODEL, PROJ_COLS), lambda b, c: (layer, 0, 0), pipeline_mode=pl.Buffered(1))
    in_specs = [smem, smem, cur, nxt, par(params[0]), w_spec] + [par(p) for p in params[2:]]
    stage = [
        pltpu.VMEM((L + 8, SSD_CONV_CH), F32),
        pltpu.VMEM((L, ST_F_COLS), F32),
        pltpu.VMEM((L, ST_B_COLS), BF16),
    ]
    scratch = [
        pltpu.VMEM((SSD_GROUPS, SSD_STATE, 256), F32),
        pltpu.VMEM((2, 128, 128), F32),
        pltpu.VMEM((2, 128, 128), F32),
        pltpu.VMEM((8, 128), F32),
        pltpu.VMEM((2, s, 256), BF16),
        pltpu.VMEM((2, DIFF_HEADS, s, 128), BF16),
        pltpu.VMEM((4, DIFF_HEADS, L, L), F32),
        pltpu.VMEM((DIFF_HEADS, 2 * L, 128), BF16),
        pltpu.VMEM((DIFF_HEADS, 2 * L, L), F32),
        pltpu.VMEM((DIFF_HEADS, 2 * L, 128), F32),
        pltpu.VMEM((DIFF_HEADS, 2 * L, KEY_GROUP * L), F32),
        pltpu.VMEM((DIFF_HEADS, 2 * L, KEY_GROUP * L), F32),
    ] + stage + stage
    return pl.pallas_call(
        functools.partial(_mixer_kernel, layer=layer, lam_init=lam_init, n_blocks=nb),
        grid=(bsz, nb // cps),
        in_specs=in_specs,
        out_specs=pl.BlockSpec((1, cps * L, MIX_WIDTH), lambda b, c: (b, c, 0)),
        out_shape=jax.ShapeDtypeStruct((bsz, s, MIX_WIDTH), BF16),
        scratch_shapes=scratch,
        compiler_params=pltpu.CompilerParams(dimension_semantics=("arbitrary", "arbitrary"),
                                             vmem_limit_bytes=V7X_VMEM_LIMIT),
        name="mixer",
    )(lam, rel_bias.astype(F32), h3d, h3d, *params)


def _dense_kernel(h_ref, y_ref, p_ref, wo_ref, n2_ref, wg_ref, wu_ref, wd_ref, pg_ref, pp_ref, fn_ref,
                  o_ref, *, final, h_chunk):
    h1 = h_ref[...] + _dot(y_ref[...], wo_ref[...])
    var = jnp.mean(h1 * h1, axis=-1, keepdims=True)
    u = ((h1 * lax.rsqrt(var + NORM_EPS)) * n2_ref[...]).astype(BF16)
    ffn = jnp.zeros(h1.shape, F32)
    for c0 in range(0, FFN_HIDDEN, h_chunk):
        g = _dot(u, wg_ref[:, c0:c0 + h_chunk])
        up = _dot(u, wu_ref[:, c0:c0 + h_chunk])
        a = ((g * _sigmoid(g)) * up).astype(BF16)
        ffn = ffn + _dot(a, wd_ref[c0:c0 + h_chunk, :])
    h2 = h1 + ffn
    gate = _sigmoid(_dot(h2.astype(BF16), pg_ref[...]))
    h3 = h2 + gate * _dot(p_ref[...].astype(BF16), pp_ref[...])
    if final:
        fvar = jnp.mean(h3 * h3, axis=-1, keepdims=True)
        h3 = (h3 * lax.rsqrt(fvar + NORM_EPS)) * fn_ref[...]
    o_ref[...] = h3


def _dense(h2d, y2d, p_all, weights, final_norm_w, *, layer, final, tm):
    t = h2d.shape[0]
    row = lambda width: pl.BlockSpec((tm, width), lambda i: (i, 0))
    res = lambda a: pl.BlockSpec((None,) + a.shape[1:], lambda i: (layer, 0, 0), pipeline_mode=pl.Buffered(1))
    return pl.pallas_call(
        functools.partial(_dense_kernel, final=final, h_chunk=FFN_CHUNK),
        grid=(t // tm,),
        in_specs=[row(D_MODEL), row(MIX_WIDTH), pl.BlockSpec((None, tm, PLE_DIM), lambda i: (layer, i, 0))]
        + [res(w) for w in weights] + [pl.BlockSpec((1, D_MODEL), lambda i: (0, 0))],
        out_specs=row(D_MODEL),
        out_shape=jax.ShapeDtypeStruct((t, D_MODEL), F32),
        compiler_params=pltpu.CompilerParams(dimension_semantics=("arbitrary",),
                                             vmem_limit_bytes=V7X_VMEM_LIMIT),
        name="dense",
    )(h2d, y2d, p_all, *weights, final_norm_w.reshape(1, D_MODEL).astype(F32))


def _pack_w_in(w):
    z, xbc, dt, dq, dk, dv, mq, mk, mv, mo, mi, mf = jnp.split(w, np.cumsum(IN_SPLITS)[:-1], axis=-1)
    zeros = lambda n: jnp.zeros(w.shape[:-1] + (n,), w.dtype)
    small_a = jnp.concatenate([dt, mi, zeros(128 - LANE_ML - MLSTM_HEADS)], axis=-1)
    small_b = jnp.concatenate([zeros(LANE_ML), mf, zeros(128 - LANE_ML - MLSTM_HEADS)], axis=-1)
    return jnp.concatenate([xbc, z, mo, small_a, small_b, dq, dk, dv, mq, mk, mv], axis=-1).astype(BF16)


def _lanes(depth, *placed):
    row = jnp.zeros((depth, 128), F32)
    for off, v in placed:
        row = row.at[:, off:off + v.shape[1]].set(v.astype(F32))
    return row[:, None, :]


def kernel(x, p, norm1_w, w_in, ssd_conv_w, ssd_conv_b, ssd_dt_bias, ssd_a_log, ssd_d, ssd_norm_w, diff_lq1, diff_lk1, diff_lq2, diff_lk2, diff_norm_w, rel_bias, mlstm_i_bias, mlstm_f_bias, mlstm_norm_w, w_out, norm2_w, w_ffn_gate, w_ffn_up, w_ffn_down, ple_gate_w, ple_proj_w, final_norm_w):
    bsz, s, d = x.shape
    depth = w_in.shape[0]
    assert d == D_MODEL and s % (KEY_GROUP * CHUNK) == 0 and s % (CHUNKS_PER_STEP * CHUNK) == 0
    t = bsz * s
    tm = ROW_TILE if s % ROW_TILE == 0 else CHUNK
    row3 = lambda a: a.astype(F32)[:, None, :]

    lam_inits = [0.8 - 0.6 * math.exp(-0.3 * i) for i in range(depth)]
    lam = (jnp.exp(jnp.sum(diff_lq1.astype(F32) * diff_lk1.astype(F32), axis=-1))
           - jnp.exp(jnp.sum(diff_lq2.astype(F32) * diff_lk2.astype(F32), axis=-1))
           + jnp.asarray(lam_inits, F32))
    mixer_params = [
        row3(norm1_w), _pack_w_in(w_in), ssd_conv_w.astype(F32), row3(ssd_conv_b),
        _lanes(depth, (LANE_SSD, ssd_dt_bias), (LANE_ML, mlstm_i_bias)), _lanes(depth, (LANE_ML, mlstm_f_bias)),
        _lanes(depth, (LANE_SSD, ssd_a_log)),
        row3(jnp.repeat(ssd_d, SSD_HEAD_DIM, axis=1)), row3(ssd_norm_w),
        row3(jnp.tile(diff_norm_w, (1, DIFF_HEADS))), row3(mlstm_norm_w),
    ]
    dense_weights = [w_out.astype(BF16), row3(norm2_w), w_ffn_gate.astype(BF16), w_ffn_up.astype(BF16),
                     w_ffn_down.astype(BF16), ple_gate_w.astype(BF16), ple_proj_w.astype(BF16)]
    p_all = p.reshape(depth, t, PLE_DIM).astype(F32)

    h = x.reshape(t, d).astype(F32)
    for i in range(depth):
        y = _mixer(h.reshape(bsz, s, d), lam, rel_bias, mixer_params, layer=i, lam_init=lam_inits[i])
        h = _dense(h, y.reshape(t, MIX_WIDTH), p_all, dense_weights, final_norm_w, layer=i,
                   final=(i == depth - 1), tm=tm)
    return h.reshape(bsz, s, d).astype(x.dtype)
```

```python
import functools
import math

import numpy as np
import jax
import jax.numpy as jnp
from jax import lax
from jax.experimental import pallas as pl
from jax.experimental.pallas import tpu as pltpu

F32 = jnp.float32
BF16 = jnp.bfloat16

D_MODEL = 1024
PLE_DIM = 256
NORM_EPS = 1e-6
CHUNK = 128
KEY_GROUP = 4
CHUNKS_PER_STEP = 4

SSD_HEADS = 8
SSD_HEAD_DIM = 64
SSD_WIDTH = 512
SSD_GROUPS = 2
SSD_STATE = 128
SSD_CONV = 4
SSD_CONV_CH = 1024
DIFF_HEADS = 4
DIFF_QK_DIM = 32
DIFF_V_DIM = 64
DIFF_WIDTH = 256
REL_BUCKETS = 32
REL_MAX_DIST = 128
MLSTM_HEADS = 4
MLSTM_HEAD_DIM = 64
MLSTM_WIDTH = 256
MIX_WIDTH = 1024
FFN_HIDDEN = 2816
IN_SPLITS = (512, 1024, 8, 256, 256, 256, 256, 256, 256, 256, 4, 4)

W_XBC, W_ZMO, W_SMALL, W_QKV = 0, 1024, 1792, 2048
PROJ_COLS = W_QKV + 6 * 256
ST_Z, ST_MO, ST_SMALL_A, ST_SMALL_B, ST_F_COLS = 0, 512, 768, 896, 1024
ST_Q, ST_MQ, ST_MK, ST_MV, ST_B_COLS = 0, 256, 512, 768, 1024
LANE_SSD, LANE_ML = 0, 8
ATTN_C1 = (DIFF_QK_DIM ** -0.5) * math.log2(math.e)

ROW_TILE = 512
FFN_CHUNK = 1408

V7X_VMEM_LIMIT = 56 * 1024 * 1024
MASKED = -1e30
LOG2E = math.log2(math.e)


def _t5_bucket_lower_bounds():
    max_exact = REL_BUCKETS // 2
    d = np.arange(0, 2 * CHUNK, dtype=np.int64)
    df = np.maximum(d, 1).astype(np.float32)
    large = max_exact + (np.log(df / np.float32(max_exact)) / np.float32(math.log(REL_MAX_DIST / max_exact))
                         * np.float32(REL_BUCKETS - max_exact)).astype(np.int32)
    large = np.minimum(large, REL_BUCKETS - 1)
    bucket = np.where(d < max_exact, d, large)
    assert np.all(np.diff(bucket) >= 0) and bucket[CHUNK] == REL_BUCKETS - 1
    return [int(np.argmax(bucket >= b)) for b in range(REL_BUCKETS)]


_BUCKET_LO = _t5_bucket_lower_bounds()


def _sigmoid(x):
    return 0.5 * jnp.tanh(0.5 * x) + 0.5


def _softplus(x):
    return jnp.maximum(x, 0.0) + jnp.log(1.0 + jnp.exp(-jnp.abs(x)))


def _dot(a, b):
    return jnp.dot(a, b, preferred_element_type=F32)


def _dot_nt(a, b):
    return lax.dot_general(a, b, (((1,), (1,)), ((), ())), preferred_element_type=F32)


def _dot_tn(a, b):
    return lax.dot_general(a, b, (((0,), (0,)), ((), ())), preferred_element_type=F32)


def _scan_rows(x, combine, identity):
    sub = lax.broadcasted_iota(jnp.int32, (8, x.shape[1]), 0)
    tiles, carry = [], None
    for i in range(x.shape[0] // 8):
        t = x[8 * i:8 * (i + 1)]
        for sh in (1, 2, 4):
            t = combine(t, jnp.where(sub >= sh, pltpu.roll(t, sh, 0), identity))
        if carry is not None:
            t = combine(t, carry)
        carry = t[7:8]
        tiles.append(t)
    return jnp.concatenate(tiles, axis=0)


def _mixer_kernel(lam_ref, relb_ref, hc_ref, hn_ref, n1_ref, w_ref, convw_ref, convb_ref,
                  biasa_ref, biasb_ref, alog_ref, dfull_ref, ssdnw_ref, dnw_ref, mnw_ref,
                  y_ref,
                  prev_t, ct, nm, mst, kb, vb, bias, qbd, rstat, acc_s, s_even, s_odd,
                  xpad_a, stf_a, stb_a, xpad_b, stf_b, stb_b,
                  *, layer, lam_init, n_blocks):
    L = CHUNK
    b = pl.program_id(0)
    cp = pl.program_id(1)

    lane128 = lax.broadcasted_iota(jnp.int32, (L, 128), 1)
    lo = lane128 < 64
    row_i = lax.broadcasted_iota(jnp.int32, (L, L), 0)
    col_i = lax.broadcasted_iota(jnp.int32, (L, L), 1)
    tril = row_i >= col_i
    row_lo = row_i < 64

    first_step = jnp.logical_and(b == 0, cp == 0)

    @pl.when(first_step)
    def _():
        for h in range(DIFF_HEADS):
            bias[0, h] = jnp.full((L, L), MASKED, F32)
            bias[3, h] = jnp.zeros((L, L), F32)
            far = relb_ref[REL_BUCKETS - 1, h]
            for which, off in ((1, 0), (2, L)):
                dist = row_i - col_i + off
                v = jnp.full((L, L), far, F32)
                for bkt in range(REL_BUCKETS - 2, -1, -1):
                    v = jnp.where(dist < _BUCKET_LO[bkt + 1], relb_ref[bkt, h], v)
                bias[which, h] = jnp.where(dist >= 0, (v - far) * LOG2E, MASKED)

    @pl.when(cp == 0)
    def _():
        prev_t[...] = jnp.zeros(prev_t.shape, F32)
        ct[...] = jnp.zeros(ct.shape, F32)
        nm[...] = jnp.zeros(nm.shape, F32)
        mst[...] = jnp.zeros(mst.shape, F32)

    def normed(x):
        var = jnp.mean(x * x, axis=-1, keepdims=True)
        return ((x * lax.rsqrt(var + NORM_EPS)) * n1_ref[...]).astype(BF16)

    def project_a(u, xpad, stf):
        xpad[8:8 + L, :] = _dot(u, w_ref[:, W_XBC:W_ZMO])
        zmo = _dot(u, w_ref[:, W_ZMO:W_SMALL])
        z = zmo[:, :SSD_WIDTH]
        stf[:, ST_Z:ST_MO] = z * _sigmoid(z)
        stf[:, ST_MO:ST_SMALL_A] = _sigmoid(zmo[:, SSD_WIDTH:])

    def project_b(u, stf, stb, slot, row0):
        stf[:, ST_SMALL_A:] = _dot(u, w_ref[:, W_SMALL:W_QKV])
        qkv = _dot(u, w_ref[:, W_QKV:])
        stb[:, ST_Q:ST_MQ] = (qkv[:, 0:256] * ATTN_C1).astype(BF16)
        stb[:, ST_MQ:] = qkv[:, 768:].astype(BF16)
        kb[slot, pl.ds(row0, L), :] = qkv[:, 256:512].astype(BF16)
        v = qkv[:, 512:768].astype(BF16)
        for h in range(DIFF_HEADS):
            own = lo if h % 2 == 0 else jnp.logical_not(lo)
            vb[slot, h, pl.ds(row0, L), :] = jnp.where(own, v[:, 128 * (h // 2):128 * (h // 2 + 1)],
                                                       jnp.ones((), BF16))

    slot = lax.rem(b, 2)

    @pl.when(first_step)
    def _():
        xpad_a[...] = jnp.zeros(xpad_a.shape, F32)
        xpad_b[...] = jnp.zeros(xpad_b.shape, F32)
        kb[...] = jnp.zeros(kb.shape, BF16)
        vb[...] = jnp.zeros(vb.shape, BF16)
        u0 = normed(hc_ref[0, 0:L, :])
        project_a(u0, xpad_a, stf_a)
        project_b(u0, stf_a, stb_a, 0, 0)

    def chunk_block(c, y_rows, xpad, stf, stb, xpad_other):
        xpad[0:8, :] = jnp.where(c > 0, xpad_other[L:L + 8, :], 0.0)

        pre_a = stf[:, ST_SMALL_A:ST_SMALL_B] + biasa_ref[...]
        log_f = -_softplus(-(stf[:, ST_SMALL_B:] + biasb_ref[...]))
        ssd_lane = lane128 < LANE_ML
        a_neg = jnp.where(ssd_lane[0:1], -jnp.exp(alog_ref[...]), 0.0)
        dt_c = _softplus(pre_a)
        csum = _scan_rows(jnp.where(ssd_lane, dt_c * a_neg, log_f), jnp.add, 0.0)
        last = csum[L - 1:L]
        u_c = pre_a - csum
        m_intra = csum + _scan_rows(u_c, jnp.maximum, -jnp.inf)
        w_end = last - csum + pre_a
        m_loc = jnp.max(w_end, axis=0, keepdims=True)
        e_end = jnp.exp(w_end - m_loc)
        m_prev = mst[0:1]
        inter_log = csum + m_prev
        m_t = jnp.maximum(inter_log, m_intra)
        w_inter = jnp.exp(inter_log - m_t)
        a_col = csum - m_t
        en = jnp.exp(-m_t)
        m_new = jnp.maximum(last + m_prev, m_loc)
        a_prev = jnp.exp(last + m_prev - m_new)
        a_loc = jnp.exp(m_loc - m_new)
        rows_t = jnp.where(ssd_lane, csum, u_c).T

        def bcast(arr, j):
            return jnp.broadcast_to(arr[:, j:j + 1], (arr.shape[0], 128))

        def pair(arr, j0, k):
            return jnp.where(lo[:arr.shape[0]], bcast(arr, j0 + 2 * k), bcast(arr, j0 + 2 * k + 1))

        conv = convb_ref[...]
        for k in range(SSD_CONV):
            conv = conv + convw_ref[k:k + 1, :] * xpad[5 + k:5 + k + L, :]
        xc = conv * _sigmoid(conv)
        xs = xc[:, :SSD_WIDTH]
        bmat = xc[:, SSD_WIDTH:SSD_WIDTH + 256]
        cmat = xc[:, SSD_WIDTH + 256:]
        dfull = dfull_ref[...]
        cs_b = [bcast(csum, LANE_SSD + h) for h in range(SSD_HEADS)]
        xdt, xds, ecs, cdec = [], [], [], []
        for k in range(4):
            cs_p = jnp.where(lo, cs_b[2 * k], cs_b[2 * k + 1])
            last_p = pair(last, LANE_SSD, k)
            xdt.append(xs[:, 128 * k:128 * (k + 1)] * pair(dt_c, LANE_SSD, k))
            xds.append((xdt[k] * jnp.exp(last_p - cs_p)).astype(BF16))
            ecs.append(jnp.exp(cs_p))
            cdec.append(jnp.exp(last_p))

        y_slabs = []
        for g_ in range(SSD_GROUPS):
            bm_g = bmat[:, 128 * g_:128 * (g_ + 1)]
            cm_b = cmat[:, 128 * g_:128 * (g_ + 1)].astype(BF16)
            cb = _dot_nt(cm_b, bm_g.astype(BF16))
            s_loc = _dot_tn(bm_g.astype(BF16), jnp.concatenate(xds[2 * g_:2 * g_ + 2], axis=1))
            prev = prev_t[g_]
            y_off = _dot(cm_b, prev.astype(BF16))
            for kk in range(2):
                k = 2 * g_ + kk
                slab = xdt[k].astype(BF16)
                halves = []
                for hl in range(2):
                    h = 2 * k + hl
                    diff = cs_b[h] - rows_t[LANE_SSD + h:LANE_SSD + h + 1, :]
                    dec = jnp.exp(jnp.where(tril, diff, -jnp.inf))
                    halves.append(_dot((cb * dec).astype(BF16), slab))
                y_diag = jnp.where(lo, halves[0], halves[1])
                y_slabs.append(y_diag + y_off[:, 128 * kk:128 * (kk + 1)] * ecs[k]
                               + xs[:, 128 * k:128 * (k + 1)] * dfull[:, 128 * k:128 * (k + 1)])
            prev_t[g_] = prev * jnp.concatenate(cdec[2 * g_:2 * g_ + 2], axis=1) + s_loc
        y = jnp.concatenate(y_slabs, axis=1)
        yz = y * stf[:, ST_Z:ST_MO]
        var = jnp.mean(yz * yz, axis=-1, keepdims=True)
        y_ref[0, y_rows, 0:SSD_WIDTH] = ((yz * lax.rsqrt(var + NORM_EPS)) * ssdnw_ref[...]).astype(BF16)

        bd_mask = row_lo == (col_i < 64)
        inv_sqrt_d = MLSTM_HEAD_DIM ** -0.5
        ones_b = jnp.ones((L, 128), BF16)
        mq = stb[:, ST_MQ:ST_MK]
        mk = stb[:, ST_MK:ST_MV]
        mv = stb[:, ST_MV:]
        mo = stf[:, ST_MO:ST_SMALL_A]
        mnw = mnw_ref[...]
        m_slabs = []
        for k in range(2):
            sl = slice(128 * k, 128 * (k + 1))
            q_b = mq[:, sl]
            k_b = mk[:, sl]
            v_b = mv[:, sl]
            v1 = jnp.concatenate([v_b, ones_b], axis=1)
            intra = []
            for hl in range(2):
                h = 2 * k + hl
                qm = jnp.where(lo if hl == 0 else jnp.logical_not(lo), q_b, jnp.zeros((), BF16))
                qk = _dot_nt(qm, k_b)
                arg = bcast(a_col, LANE_ML + h) + rows_t[LANE_ML + h:LANE_ML + h + 1, :]
                w_intra = jnp.exp(jnp.where(tril, arg, -jnp.inf))
                intra.append(_dot(((qk * inv_sqrt_d) * w_intra).astype(BF16), v1))
            num_intra = jnp.where(lo, intra[0][:, :128], intra[1][:, :128])
            den_intra = jnp.where(lo, intra[0][:, 128:], intra[1][:, 128:])
            ct_prev = ct[k]
            nm_prev = nm[k]
            wi = pair(w_inter, LANE_ML, k)
            num = num_intra + wi * _dot(q_b, ct_prev.astype(BF16))
            den = den_intra + wi * _dot(q_b, nm_prev.astype(BF16))
            denom = jnp.maximum(jnp.abs(den), pair(en, LANE_ML, k))
            hs = num / denom
            sq = hs * hs
            s_lo = jnp.sum(jnp.where(lo, sq, 0.0), axis=1, keepdims=True)
            s_hi = jnp.sum(jnp.where(lo, 0.0, sq), axis=1, keepdims=True)
            hvar = jnp.where(lo, s_lo, s_hi) * (1.0 / MLSTM_HEAD_DIM)
            hn = (hs * lax.rsqrt(hvar + NORM_EPS)) * mnw[:, sl]
            m_slabs.append(mo[:, sl] * hn)
            e_p = pair(e_end, LANE_ML, k)
            loc = _dot_tn(k_b, jnp.concatenate([(v_b.astype(F32) * e_p).astype(BF16), e_p.astype(BF16)], axis=1)
                          ) * inv_sqrt_d
            ap = pair(a_prev, LANE_ML, k)
            al = pair(a_loc, LANE_ML, k)
            ct[k] = ct_prev * ap + jnp.where(bd_mask, loc[:, :128], 0.0) * al
            nm[k] = nm_prev * ap + jnp.where(bd_mask, loc[:, 128:], 0.0) * al
        mst[...] = jnp.broadcast_to(m_new, mst.shape)
        y_ref[0, y_rows, SSD_WIDTH + DIFF_WIDTH:] = jnp.concatenate(m_slabs, axis=1).astype(BF16)

        q = stb[:, ST_Q:ST_MQ]
        zero_b = jnp.zeros((), BF16)
        for h in range(DIFF_HEADS):
            qs = q[:, 128 * (h // 2):128 * (h // 2 + 1)]
            base = 64 * (h % 2)
            q0 = jnp.where((lane128 >= base) & (lane128 < base + 32), qs, zero_b)
            q1 = jnp.where((lane128 >= base + 32) & (lane128 < base + 64), qs, zero_b)
            qbd[h] = jnp.concatenate([q0, q1], axis=0)

    lam = lam_ref[layer]

    def attention(c, y_rows, xpad_other, stf_other, stb_other, x_next, slot_next, c_next):
        n_groups = c // KEY_GROUP + 1
        gw = KEY_GROUP * L

        def fill_scores(kg, dst):
            r0 = pl.multiple_of(kg * gw, gw)
            for h in range(DIFF_HEADS):
                keys = kb[slot, pl.ds(r0, gw), 128 * (h // 2):128 * (h // 2 + 1)]
                add = jnp.concatenate(
                    [bias[jnp.clip(c - KEY_GROUP * kg - j, -1, 2) + 1, h] for j in range(KEY_GROUP)], axis=1)
                dst[h] = _dot_nt(qbd[h], keys) + jnp.concatenate([add, add], axis=0)

        def consume(kg, src):
            r0 = pl.multiple_of(kg * gw, gw)
            for h in range(DIFF_HEADS):
                s = src[h]
                t = s[:, :L]
                for j in range(1, KEY_GROUP):
                    t = jnp.maximum(t, s[:, L * j:L * (j + 1)])
                m_old = rstat[h]
                m_new = jnp.maximum(m_old, jnp.max(t, axis=1, keepdims=True))
                p = jnp.concatenate([jnp.exp2(s[:, L * j:L * (j + 1)] - m_new) for j in range(KEY_GROUP)], axis=1)
                acc_s[h] = acc_s[h] * jnp.exp2(m_old - m_new) + _dot(p.astype(BF16), vb[slot, h, pl.ds(r0, gw), :])
                rstat[h] = m_new

        rstat[...] = jnp.full(rstat.shape, MASKED, F32)
        acc_s[...] = jnp.zeros(acc_s.shape, F32)
        fill_scores(0, s_even)

        def key_step(kg, carry):
            @pl.when(kg % 2 == 0)
            def _():
                fill_scores(kg + 1, s_odd)
                consume(kg, s_even)

            @pl.when(kg % 2 == 1)
            def _():
                fill_scores(kg + 1, s_even)
                consume(kg, s_odd)

            return carry

        g_last = n_groups - 1
        lax.fori_loop(0, g_last, key_step, 0)

        @pl.when(g_last % 2 == 0)
        def _():
            consume(g_last, s_even)

        @pl.when(g_last % 2 == 1)
        def _():
            consume(g_last, s_odd)

        d_slabs = []
        for k in range(2):
            a0, a1 = acc_s[2 * k], acc_s[2 * k + 1]
            lo2 = jnp.concatenate([lo, lo], axis=0)
            r = jnp.where(lo2, a0, a1) * (1.0 / pltpu.roll(jnp.where(lo2, a1, a0), 64, 1))
            o = r[:L] - lam * r[L:]
            sq = o * o
            s_lo = jnp.sum(jnp.where(lo, sq, 0.0), axis=1, keepdims=True)
            s_hi = jnp.sum(jnp.where(lo, 0.0, sq), axis=1, keepdims=True)
            ovar = jnp.where(lo, s_lo, s_hi) * (1.0 / DIFF_V_DIM)
            d_slabs.append(((o * lax.rsqrt(ovar + NORM_EPS)) * dnw_ref[:, 128 * k:128 * (k + 1)])
                           * (1.0 - lam_init))
        y_ref[0, y_rows, SSD_WIDTH:SSD_WIDTH + DIFF_WIDTH] = jnp.concatenate(d_slabs, axis=1).astype(BF16)

        u_next = normed(x_next)
        project_a(u_next, xpad_other, stf_other)
        project_b(u_next, stf_other, stb_other, slot_next, pl.multiple_of(c_next * L, L))

    nb = n_blocks
    sets = ((xpad_a, stf_a, stb_a), (xpad_b, stf_b, stb_b))
    for j in range(CHUNKS_PER_STEP):
        c = CHUNKS_PER_STEP * cp + j
        rows = slice(j * L, (j + 1) * L)
        if j + 1 < CHUNKS_PER_STEP:
            nxt = (hc_ref[0, (j + 1) * L:(j + 2) * L, :], slot, c + 1)
        else:
            nxt = (hn_ref[0], lax.rem(b + (c + 1) // nb, 2), lax.rem(c + 1, nb))
        chunk_block(c, rows, *sets[j % 2], sets[(j + 1) % 2][0])
        attention(c, rows, *sets[(j + 1) % 2], *nxt)


def _mixer(h3d, lam, rel_bias, params, *, layer, lam_init):
    bsz, s, _ = h3d.shape
    L = CHUNK
    nb = s // L

    def par(a):
        return pl.BlockSpec((None,) + a.shape[1:], lambda b, c: (layer, 0, 0))

    smem = pl.BlockSpec(memory_space=pltpu.SMEM)
    cps = CHUNKS_PER_STEP
    cur = pl.BlockSpec((1, cps * L, D_MODEL), lambda b, c: (b, c, 0))
    nxt = pl.BlockSpec((1, L, D_MODEL),
                       lambda b, c: (jnp.minimum(b + (cps * c + cps) // nb, bsz - 1), lax.rem(cps * c + cps, nb), 0))
    w_spec = pl.BlockSpec((None, D_MODEL, PROJ_COLS), lambda b, c: (layer, 0, 0), pipeline_mode=pl.Buffered(1))
    in_specs = [smem, smem, cur, nxt, par(params[0]), w_spec] + [par(p) for p in params[2:]]
    stage = [
        pltpu.VMEM((L + 8, SSD_CONV_CH), F32),
        pltpu.VMEM((L, ST_F_COLS), F32),
        pltpu.VMEM((L, ST_B_COLS), BF16),
    ]
    scratch = [
        pltpu.VMEM((SSD_GROUPS, SSD_STATE, 256), F32),
        pltpu.VMEM((2, 128, 128), F32),
        pltpu.VMEM((2, 128, 128), F32),
        pltpu.VMEM((8, 128), F32),
        pltpu.VMEM((2, s, 256), BF16),
        pltpu.VMEM((2, DIFF_HEADS, s, 128), BF16),
        pltpu.VMEM((4, DIFF_HEADS, L, L), F32),
        pltpu.VMEM((DIFF_HEADS, 2 * L, 128), BF16),
        pltpu.VMEM((DIFF_HEADS, 2 * L, L), F32),
        pltpu.VMEM((DIFF_HEADS, 2 * L, 128), F32),
        pltpu.VMEM((DIFF_HEADS, 2 * L, KEY_GROUP * L), F32),
        pltpu.VMEM((DIFF_HEADS, 2 * L, KEY_GROUP * L), F32),
    ] + stage + stage
    return pl.pallas_call(
        functools.partial(_mixer_kernel, layer=layer, lam_init=lam_init, n_blocks=nb),
        grid=(bsz, nb // cps),
        in_specs=in_specs,
        out_specs=pl.BlockSpec((1, cps * L, MIX_WIDTH), lambda b, c: (b, c, 0)),
        out_shape=jax.ShapeDtypeStruct((bsz, s, MIX_WIDTH), BF16),
        scratch_shapes=scratch,
        compiler_params=pltpu.CompilerParams(dimension_semantics=("arbitrary", "arbitrary"),
                                             vmem_limit_bytes=V7X_VMEM_LIMIT),
        name="mixer",
    )(lam, rel_bias.astype(F32), h3d, h3d, *params)


def _dense_kernel(h_ref, y_ref, p_ref, wo_ref, n2_ref, wg_ref, wu_ref, wd_ref, pg_ref, pp_ref, fn_ref,
                  o_ref, *, final, h_chunk):
    h1 = h_ref[...] + _dot(y_ref[...], wo_ref[...])
    var = jnp.mean(h1 * h1, axis=-1, keepdims=True)
    u = ((h1 * lax.rsqrt(var + NORM_EPS)) * n2_ref[...]).astype(BF16)
    ffn = jnp.zeros(h1.shape, F32)
    for c0 in range(0, FFN_HIDDEN, h_chunk):
        g = _dot(u, wg_ref[:, c0:c0 + h_chunk])
        up = _dot(u, wu_ref[:, c0:c0 + h_chunk])
        a = ((g * _sigmoid(g)) * up).astype(BF16)
        ffn = ffn + _dot(a, wd_ref[c0:c0 + h_chunk, :])
    h2 = h1 + ffn
    gate = _sigmoid(_dot(h2.astype(BF16), pg_ref[...]))
    h3 = h2 + gate * _dot(p_ref[...].astype(BF16), pp_ref[...])
    if final:
        fvar = jnp.mean(h3 * h3, axis=-1, keepdims=True)
        h3 = (h3 * lax.rsqrt(fvar + NORM_EPS)) * fn_ref[...]
    o_ref[...] = h3


def _dense(h2d, y2d, p_all, weights, final_norm_w, *, layer, final, tm):
    t = h2d.shape[0]
    row = lambda width: pl.BlockSpec((tm, width), lambda i: (i, 0))
    res = lambda a: pl.BlockSpec((None,) + a.shape[1:], lambda i: (layer, 0, 0), pipeline_mode=pl.Buffered(1))
    return pl.pallas_call(
        functools.partial(_dense_kernel, final=final, h_chunk=FFN_CHUNK),
        grid=(t // tm,),
        in_specs=[row(D_MODEL), row(MIX_WIDTH), pl.BlockSpec((None, tm, PLE_DIM), lambda i: (layer, i, 0))]
        + [res(w) for w in weights] + [pl.BlockSpec((1, D_MODEL), lambda i: (0, 0))],
        out_specs=row(D_MODEL),
        out_shape=jax.ShapeDtypeStruct((t, D_MODEL), F32),
        compiler_params=pltpu.CompilerParams(dimension_semantics=("arbitrary",),
                                             vmem_limit_bytes=V7X_VMEM_LIMIT),
        name="dense",
    )(h2d, y2d, p_all, *weights, final_norm_w.reshape(1, D_MODEL).astype(F32))


def _pack_w_in(w):
    z, xbc, dt, dq, dk, dv, mq, mk, mv, mo, mi, mf = jnp.split(w, np.cumsum(IN_SPLITS)[:-1], axis=-1)
    zeros = lambda n: jnp.zeros(w.shape[:-1] + (n,), w.dtype)
    small_a = jnp.concatenate([dt, mi, zeros(128 - LANE_ML - MLSTM_HEADS)], axis=-1)
    small_b = jnp.concatenate([zeros(LANE_ML), mf, zeros(128 - LANE_ML - MLSTM_HEADS)], axis=-1)
    return jnp.concatenate([xbc, z, mo, small_a, small_b, dq, dk, dv, mq, mk, mv], axis=-1).astype(BF16)


def _lanes(depth, *placed):
    row = jnp.zeros((depth, 128), F32)
    for off, v in placed:
        row = row.at[:, off:off + v.shape[1]].set(v.astype(F32))
    return row[:, None, :]


def kernel(x, p, norm1_w, w_in, ssd_conv_w, ssd_conv_b, ssd_dt_bias, ssd_a_log, ssd_d, ssd_norm_w, diff_lq1, diff_lk1, diff_lq2, diff_lk2, diff_norm_w, rel_bias, mlstm_i_bias, mlstm_f_bias, mlstm_norm_w, w_out, norm2_w, w_ffn_gate, w_ffn_up, w_ffn_down, ple_gate_w, ple_proj_w, final_norm_w):
    bsz, s, d = x.shape
    depth = w_in.shape[0]
    assert d == D_MODEL and s % (KEY_GROUP * CHUNK) == 0 and s % (CHUNKS_PER_STEP * CHUNK) == 0
    t = bsz * s
    tm = ROW_TILE if s % ROW_TILE == 0 else CHUNK
    row3 = lambda a: a.astype(F32)[:, None, :]

    lam_inits = [0.8 - 0.6 * math.exp(-0.3 * i) for i in range(depth)]
    lam = (jnp.exp(jnp.sum(diff_lq1.astype(F32) * diff_lk1.astype(F32), axis=-1))
           - jnp.exp(jnp.sum(diff_lq2.astype(F32) * diff_lk2.astype(F32), axis=-1))
           + jnp.asarray(lam_inits, F32))
    mixer_params = [
        row3(norm1_w), _pack_w_in(w_in), ssd_conv_w.astype(F32), row3(ssd_conv_b),
        _lanes(depth, (LANE_SSD, ssd_dt_bias), (LANE_ML, mlstm_i_bias)), _lanes(depth, (LANE_ML, mlstm_f_bias)),
        _lanes(depth, (LANE_SSD, ssd_a_log)),
        row3(jnp.repeat(ssd_d, SSD_HEAD_DIM, axis=1)), row3(ssd_norm_w),
        row3(jnp.tile(diff_norm_w, (1, DIFF_HEADS))), row3(mlstm_norm_w),
    ]
    dense_weights = [w_out.astype(BF16), row3(norm2_w), w_ffn_gate.astype(BF16), w_ffn_up.astype(BF16),
                     w_ffn_down.astype(BF16), ple_gate_w.astype(BF16), ple_proj_w.astype(BF16)]
    p_all = p.reshape(depth, t, PLE_DIM).astype(F32)

    h = x.reshape(t, d).astype(F32)
    for i in range(depth):
        y = _mixer(h.reshape(bsz, s, d), lam, rel_bias, mixer_params, layer=i, lam_init=lam_inits[i])
        h = _dense(h, y.reshape(t, MIX_WIDTH), p_all, dense_weights, final_norm_w, layer=i,
                   final=(i == depth - 1), tm=tm)
    return h.reshape(bsz, s, d).astype(x.dtype)
```

```python
import functools
import math

import numpy as np
import jax
import jax.numpy as jnp
from jax import lax
from jax.experimental import pallas as pl
from jax.experimental.pallas import tpu as pltpu

F32 = jnp.float32
BF16 = jnp.bfloat16

D_MODEL = 1024
PLE_DIM = 256
NORM_EPS = 1e-6
CHUNK = 128
KEY_GROUP = 4
CHUNKS_PER_STEP = 4

SSD_HEADS = 8
SSD_HEAD_DIM = 64
SSD_WIDTH = 512
SSD_GROUPS = 2
SSD_STATE = 128
SSD_CONV = 4
SSD_CONV_CH = 1024
DIFF_HEADS = 4
DIFF_QK_DIM = 32
DIFF_V_DIM = 64
DIFF_WIDTH = 256
REL_BUCKETS = 32
REL_MAX_DIST = 128
MLSTM_HEADS = 4
MLSTM_HEAD_DIM = 64
MLSTM_WIDTH = 256
MIX_WIDTH = 1024
FFN_HIDDEN = 2816
IN_SPLITS = (512, 1024, 8, 256, 256, 256, 256, 256, 256, 256, 4, 4)

W_XBC, W_ZMO, W_SMALL, W_QKV = 0, 1024, 1792, 2048
PROJ_COLS = W_QKV + 6 * 256
ST_Z, ST_MO, ST_SMALL_A, ST_SMALL_B, ST_F_COLS = 0, 512, 768, 896, 1024
ST_Q, ST_MQ, ST_MK, ST_MV, ST_B_COLS = 0, 256, 512, 768, 1024
LANE_SSD, LANE_ML = 0, 8
ATTN_C1 = (DIFF_QK_DIM ** -0.5) * math.log2(math.e)

ROW_TILE = 512
FFN_CHUNK = 1408

V7X_VMEM_LIMIT = 56 * 1024 * 1024
MASKED = -1e30
LOG2E = math.log2(math.e)


def _t5_bucket_lower_bounds():
    max_exact = REL_BUCKETS // 2
    d = np.arange(0, 2 * CHUNK, dtype=np.int64)
    df = np.maximum(d, 1).astype(np.float32)
    large = max_exact + (np.log(df / np.float32(max_exact)) / np.float32(math.log(REL_MAX_DIST / max_exact))
                         * np.float32(REL_BUCKETS - max_exact)).astype(np.int32)
    large = np.minimum(large, REL_BUCKETS - 1)
    bucket = np.where(d < max_exact, d, large)
    assert np.all(np.diff(bucket) >= 0) and bucket[CHUNK] == REL_BUCKETS - 1
    return [int(np.argmax(bucket >= b)) for b in range(REL_BUCKETS)]


_BUCKET_LO = _t5_bucket_lower_bounds()


def _sigmoid(x):
    return 0.5 * jnp.tanh(0.5 * x) + 0.5


def _softplus(x):
    return jnp.maximum(x, 0.0) + jnp.log(1.0 + jnp.exp(-jnp.abs(x)))


def _dot(a, b):
    return jnp.dot(a, b, preferred_element_type=F32)


def _dot_nt(a, b):
    return lax.dot_general(a, b, (((1,), (1,)), ((), ())), preferred_element_type=F32)


def _dot_tn(a, b):
    return lax.dot_general(a, b, (((0,), (0,)), ((), ())), preferred_element_type=F32)


def _scan_rows(x, combine, identity):
    sub = lax.broadcasted_iota(jnp.int32, (8, x.shape[1]), 0)
    tiles, carry = [], None
    for i in range(x.shape[0] // 8):
        t = x[8 * i:8 * (i + 1)]
        for sh in (1, 2, 4):
            t = combine(t, jnp.where(sub >= sh, pltpu.roll(t, sh, 0), identity))
        if carry is not None:
            t = combine(t, carry)
        carry = t[7:8]
        tiles.append(t)
    return jnp.concatenate(tiles, axis=0)


def _mixer_kernel(lam_ref, relb_ref, hc_ref, hn_ref, n1_ref, w_ref, convw_ref, convb_ref,
                  biasa_ref, biasb_ref, alog_ref, dfull_ref, ssdnw_ref, dnw_ref, mnw_ref,
                  y_ref,
                  prev_t, ct, nm, mst, kb, vb, bias, qbd, rstat, acc_s, s_even, s_odd,
                  xpad_a, stf_a, stb_a, xpad_b, stf_b, stb_b,
                  *, layer, lam_init, n_blocks):
    L = CHUNK
    b = pl.program_id(0)
    cp = pl.program_id(1)

    lane128 = lax.broadcasted_iota(jnp.int32, (L, 128), 1)
    lo = lane128 < 64
    row_i = lax.broadcasted_iota(jnp.int32, (L, L), 0)
    col_i = lax.broadcasted_iota(jnp.int32, (L, L), 1)
    tril = row_i >= col_i
    row_lo = row_i < 64

    first_step = jnp.logical_and(b == 0, cp == 0)

    @pl.when(first_step)
    def _():
        for h in range(DIFF_HEADS):
            bias[0, h] = jnp.full((L, L), MASKED, F32)
            bias[3, h] = jnp.zeros((L, L), F32)
            far = relb_ref[REL_BUCKETS - 1, h]
            for which, off in ((1, 0), (2, L)):
                dist = row_i - col_i + off
                v = jnp.full((L, L), far, F32)
                for bkt in range(REL_BUCKETS - 2, -1, -1):
                    v = jnp.where(dist < _BUCKET_LO[bkt + 1], relb_ref[bkt, h], v)
                bias[which, h] = jnp.where(dist >= 0, (v - far) * LOG2E, MASKED)

    @pl.when(cp == 0)
    def _():
        prev_t[...] = jnp.zeros(prev_t.shape, F32)
        ct[...] = jnp.zeros(ct.shape, F32)
        nm[...] = jnp.zeros(nm.shape, F32)
        mst[...] = jnp.zeros(mst.shape, F32)

    def normed(x):
        var = jnp.mean(x * x, axis=-1, keepdims=True)
        return ((x * lax.rsqrt(var + NORM_EPS)) * n1_ref[...]).astype(BF16)

    def project_a(u, xpad, stf):
        xpad[8:8 + L, :] = _dot(u, w_ref[:, W_XBC:W_ZMO])
        zmo = _dot(u, w_ref[:, W_ZMO:W_SMALL])
        z = zmo[:, :SSD_WIDTH]
        stf[:, ST_Z:ST_MO] = z * _sigmoid(z)
        stf[:, ST_MO:ST_SMALL_A] = _sigmoid(zmo[:, SSD_WIDTH:])

    def project_b(u, stf, stb, slot, row0):
        stf[:, ST_SMALL_A:] = _dot(u, w_ref[:, W_SMALL:W_QKV])
        qkv = _dot(u, w_ref[:, W_QKV:])
        stb[:, ST_Q:ST_MQ] = (qkv[:, 0:256] * ATTN_C1).astype(BF16)
        stb[:, ST_MQ:] = qkv[:, 768:].astype(BF16)
        kb[slot, pl.ds(row0, L), :] = qkv[:, 256:512].astype(BF16)
        v = qkv[:, 512:768].astype(BF16)
        for h in range(DIFF_HEADS):
            own = lo if h % 2 == 0 else jnp.logical_not(lo)
            vb[slot, h, pl.ds(row0, L), :] = jnp.where(own, v[:, 128 * (h // 2):128 * (h // 2 + 1)],
                                                       jnp.ones((), BF16))

    slot = lax.rem(b, 2)

    @pl.when(first_step)
    def _():
        xpad_a[...] = jnp.zeros(xpad_a.shape, F32)
        xpad_b[...] = jnp.zeros(xpad_b.shape, F32)
        kb[...] = jnp.zeros(kb.shape, BF16)
        vb[...] = jnp.zeros(vb.shape, BF16)
        u0 = normed(hc_ref[0, 0:L, :])
        project_a(u0, xpad_a, stf_a)
        project_b(u0, stf_a, stb_a, 0, 0)

    def chunk_block(c, y_rows, xpad, stf, stb, xpad_other):
        xpad[0:8, :] = jnp.where(c > 0, xpad_other[L:L + 8, :], 0.0)

        pre_a = stf[:, ST_SMALL_A:ST_SMALL_B] + biasa_ref[...]
        log_f = -_softplus(-(stf[:, ST_SMALL_B:] + biasb_ref[...]))
        ssd_lane = lane128 < LANE_ML
        a_neg = jnp.where(ssd_lane[0:1], -jnp.exp(alog_ref[...]), 0.0)
        dt_c = _softplus(pre_a)
        csum = _scan_rows(jnp.where(ssd_lane, dt_c * a_neg, log_f), jnp.add, 0.0)
        last = csum[L - 1:L]
        u_c = pre_a - csum
        m_intra = csum + _scan_rows(u_c, jnp.maximum, -jnp.inf)
        w_end = last - csum + pre_a
        m_loc = jnp.max(w_end, axis=0, keepdims=True)
        e_end = jnp.exp(w_end - m_loc)
        m_prev = mst[0:1]
        inter_log = csum + m_prev
        m_t = jnp.maximum(inter_log, m_intra)
        w_inter = jnp.exp(inter_log - m_t)
        a_col = csum - m_t
        en = jnp.exp(-m_t)
        m_new = jnp.maximum(last + m_prev, m_loc)
        a_prev = jnp.exp(last + m_prev - m_new)
        a_loc = jnp.exp(m_loc - m_new)
        rows_t = jnp.where(ssd_lane, csum, u_c).T

        def bcast(arr, j):
            return jnp.broadcast_to(arr[:, j:j + 1], (arr.shape[0], 128))

        def pair(arr, j0, k):
            return jnp.where(lo[:arr.shape[0]], bcast(arr, j0 + 2 * k), bcast(arr, j0 + 2 * k + 1))

        conv = convb_ref[...]
        for k in range(SSD_CONV):
            conv = conv + convw_ref[k:k + 1, :] * xpad[5 + k:5 + k + L, :]
        xc = conv * _sigmoid(conv)
        xs = xc[:, :SSD_WIDTH]
        bmat = xc[:, SSD_WIDTH:SSD_WIDTH + 256]
        cmat = xc[:, SSD_WIDTH + 256:]
        dfull = dfull_ref[...]
        cs_b = [bcast(csum, LANE_SSD + h) for h in range(SSD_HEADS)]
        xdt, xds, ecs, cdec = [], [], [], []
        for k in range(4):
            cs_p = jnp.where(lo, cs_b[2 * k], cs_b[2 * k + 1])
            last_p = pair(last, LANE_SSD, k)
            xdt.append(xs[:, 128 * k:128 * (k + 1)] * pair(dt_c, LANE_SSD, k))
            xds.append((xdt[k] * jnp.exp(last_p - cs_p)).astype(BF16))
            ecs.append(jnp.exp(cs_p))
            cdec.append(jnp.exp(last_p))

        y_slabs = []
        for g_ in range(SSD_GROUPS):
            bm_g = bmat[:, 128 * g_:128 * (g_ + 1)]
            cm_b = cmat[:, 128 * g_:128 * (g_ + 1)].astype(BF16)
            cb = _dot_nt(cm_b, bm_g.astype(BF16))
            s_loc = _dot_tn(bm_g.astype(BF16), jnp.concatenate(xds[2 * g_:2 * g_ + 2], axis=1))
            prev = prev_t[g_]
            y_off = _dot(cm_b, prev.astype(BF16))
            for kk in range(2):
                k = 2 * g_ + kk
                slab = xdt[k].astype(BF16)
                halves = []
                for hl in range(2):
                    h = 2 * k + hl
                    diff = cs_b[h] - rows_t[LANE_SSD + h:LANE_SSD + h + 1, :]
                    dec = jnp.exp(jnp.where(tril, diff, -jnp.inf))
                    halves.append(_dot((cb * dec).astype(BF16), slab))
                y_diag = jnp.where(lo, halves[0], halves[1])
                y_slabs.append(y_diag + y_off[:, 128 * kk:128 * (kk + 1)] * ecs[k]
                               + xs[:, 128 * k:128 * (k + 1)] * dfull[:, 128 * k:128 * (k + 1)])
            prev_t[g_] = prev * jnp.concatenate(cdec[2 * g_:2 * g_ + 2], axis=1) + s_loc
        y = jnp.concatenate(y_slabs, axis=1)
        yz = y * stf[:, ST_Z:ST_MO]
        var = jnp.mean(yz * yz, axis=-1, keepdims=True)
        y_ref[0, y_rows, 0:SSD_WIDTH] = ((yz * lax.rsqrt(var + NORM_EPS)) * ssdnw_ref[...]).astype(BF16)

        bd_mask = row_lo == (col_i < 64)
        inv_sqrt_d = MLSTM_HEAD_DIM ** -0.5
        ones_b = jnp.ones((L, 128), BF16)
        mq = stb[:, ST_MQ:ST_MK]
        mk = stb[:, ST_MK:ST_MV]
        mv = stb[:, ST_MV:]
        mo = stf[:, ST_MO:ST_SMALL_A]
        mnw = mnw_ref[...]
        m_slabs = []
        for k in range(2):
            sl = slice(128 * k, 128 * (k + 1))
            q_b = mq[:, sl]
            k_b = mk[:, sl]
            v_b = mv[:, sl]
            v1 = jnp.concatenate([v_b, ones_b], axis=1)
            intra = []
            for hl in range(2):
                h = 2 * k + hl
                qm = jnp.where(lo if hl == 0 else jnp.logical_not(lo), q_b, jnp.zeros((), BF16))
                qk = _dot_nt(qm, k_b)
                arg = bcast(a_col, LANE_ML + h) + rows_t[LANE_ML + h:LANE_ML + h + 1, :]
                w_intra = jnp.exp(jnp.where(tril, arg, -jnp.inf))
                intra.append(_dot(((qk * inv_sqrt_d) * w_intra).astype(BF16), v1))
            num_intra = jnp.where(lo, intra[0][:, :128], intra[1][:, :128])
            den_intra = jnp.where(lo, intra[0][:, 128:], intra[1][:, 128:])
            ct_prev = ct[k]
            nm_prev = nm[k]
            wi = pair(w_inter, LANE_ML, k)
            num = num_intra + wi * _dot(q_b, ct_prev.astype(BF16))
            den = den_intra + wi * _dot(q_b, nm_prev.astype(BF16))
            denom = jnp.maximum(jnp.abs(den), pair(en, LANE_ML, k))
            hs = num / denom
            sq = hs * hs
            s_lo = jnp.sum(jnp.where(lo, sq, 0.0), axis=1, keepdims=True)
            s_hi = jnp.sum(jnp.where(lo, 0.0, sq), axis=1, keepdims=True)
            hvar = jnp.where(lo, s_lo, s_hi) * (1.0 / MLSTM_HEAD_DIM)
            hn = (hs * lax.rsqrt(hvar + NORM_EPS)) * mnw[:, sl]
            m_slabs.append(mo[:, sl] * hn)
            e_p = pair(e_end, LANE_ML, k)
            loc = _dot_tn(k_b, jnp.concatenate([(v_b.astype(F32) * e_p).astype(BF16), e_p.astype(BF16)], axis=1)
                          ) * inv_sqrt_d
            ap = pair(a_prev, LANE_ML, k)
            al = pair(a_loc, LANE_ML, k)
            ct[k] = ct_prev * ap + jnp.where(bd_mask, loc[:, :128], 0.0) * al
            nm[k] = nm_prev * ap + jnp.where(bd_mask, loc[:, 128:], 0.0) * al
        mst[...] = jnp.broadcast_to(m_new, mst.shape)
        y_ref[0, y_rows, SSD_WIDTH + DIFF_WIDTH:] = jnp.concatenate(m_slabs, axis=1).astype(BF16)

        q = stb[:, ST_Q:ST_MQ]
        zero_b = jnp.zeros((), BF16)
        for h in range(DIFF_HEADS):
            qs = q[:, 128 * (h // 2):128 * (h // 2 + 1)]
            base = 64 * (h % 2)
            q0 = jnp.where((lane128 >= base) & (lane128 < base + 32), qs, zero_b)
            q1 = jnp.where((lane128 >= base + 32) & (lane128 < base + 64), qs, zero_b)
            qbd[h] = jnp.concatenate([q0, q1], axis=0)

    lam = lam_ref[layer]

    def attention(c, y_rows, xpad_other, stf_other, stb_other, x_next, slot_next, c_next):
        n_groups = c // KEY_GROUP + 1
        gw = KEY_GROUP * L

        def fill_scores(kg, dst):
            r0 = pl.multiple_of(kg * gw, gw)
            for h in range(DIFF_HEADS):
                keys = kb[slot, pl.ds(r0, gw), 128 * (h // 2):128 * (h // 2 + 1)]
                add = jnp.concatenate(
                    [bias[jnp.clip(c - KEY_GROUP * kg - j, -1, 2) + 1, h] for j in range(KEY_GROUP)], axis=1)
                dst[h] = _dot_nt(qbd[h], keys) + jnp.concatenate([add, add], axis=0)

        def consume(kg, src, nblk=KEY_GROUP):
            r0 = pl.multiple_of(kg * gw, gw)
            for h in range(DIFF_HEADS):
                s = src[h, :, 0:nblk * L]
                t = s[:, :L]
                for j in range(1, nblk):
                    t = jnp.maximum(t, s[:, L * j:L * (j + 1)])
                m_old = rstat[h]
                m_new = jnp.maximum(m_old, jnp.max(t, axis=1, keepdims=True))
                p = jnp.concatenate([jnp.exp2(s[:, L * j:L * (j + 1)] - m_new) for j in range(nblk)], axis=1)
                acc_s[h] = (acc_s[h] * jnp.exp2(m_old - m_new)
                            + _dot(p.astype(BF16), vb[slot, h, pl.ds(r0, nblk * L), :]))
                rstat[h] = m_new

        rstat[...] = jnp.full(rstat.shape, MASKED, F32)
        acc_s[...] = jnp.zeros(acc_s.shape, F32)
        fill_scores(0, s_even)

        def key_step(kg, carry):
            @pl.when(kg % 2 == 0)
            def _():
                fill_scores(kg + 1, s_odd)
                consume(kg, s_even)

            @pl.when(kg % 2 == 1)
            def _():
                fill_scores(kg + 1, s_even)
                consume(kg, s_odd)

            return carry

        g_last = n_groups - 1
        lax.fori_loop(0, g_last, key_step, 0)

        half = lax.rem(c, KEY_GROUP) < KEY_GROUP // 2
        for parity, buf in ((0, s_even), (1, s_odd)):
            for is_half, nblk in ((True, KEY_GROUP // 2), (False, KEY_GROUP)):
                @pl.when(jnp.logical_and(g_last % 2 == parity, half == is_half))
                def _(buf=buf, nblk=nblk):
                    consume(g_last, buf, nblk)

        d_slabs = []
        for k in range(2):
            a0, a1 = acc_s[2 * k], acc_s[2 * k + 1]
            lo2 = jnp.concatenate([lo, lo], axis=0)
            r = jnp.where(lo2, a0, a1) * (1.0 / pltpu.roll(jnp.where(lo2, a1, a0), 64, 1))
            o = r[:L] - lam * r[L:]
            sq = o * o
            s_lo = jnp.sum(jnp.where(lo, sq, 0.0), axis=1, keepdims=True)
            s_hi = jnp.sum(jnp.where(lo, 0.0, sq), axis=1, keepdims=True)
            ovar = jnp.where(lo, s_lo, s_hi) * (1.0 / DIFF_V_DIM)
            d_slabs.append(((o * lax.rsqrt(ovar + NORM_EPS)) * dnw_ref[:, 128 * k:128 * (k + 1)])
                           * (1.0 - lam_init))
        y_ref[0, y_rows, SSD_WIDTH:SSD_WIDTH + DIFF_WIDTH] = jnp.concatenate(d_slabs, axis=1).astype(BF16)

        u_next = normed(x_next)
        project_a(u_next, xpad_other, stf_other)
        project_b(u_next, stf_other, stb_other, slot_next, pl.multiple_of(c_next * L, L))

    nb = n_blocks
    sets = ((xpad_a, stf_a, stb_a), (xpad_b, stf_b, stb_b))
    for j in range(CHUNKS_PER_STEP):
        c = CHUNKS_PER_STEP * cp + j
        rows = slice(j * L, (j + 1) * L)
        if j + 1 < CHUNKS_PER_STEP:
            nxt = (hc_ref[0, (j + 1) * L:(j + 2) * L, :], slot, c + 1)
        else:
            nxt = (hn_ref[0], lax.rem(b + (c + 1) // nb, 2), lax.rem(c + 1, nb))
        chunk_block(c, rows, *sets[j % 2], sets[(j + 1) % 2][0])
        attention(c, rows, *sets[(j + 1) % 2], *nxt)


def _mixer(h3d, lam, rel_bias, params, *, layer, lam_init):
    bsz, s, _ = h3d.shape
    L = CHUNK
    nb = s // L

    def par(a):
        return pl.BlockSpec((None,) + a.shape[1:], lambda b, c: (layer, 0, 0))

    smem = pl.BlockSpec(memory_space=pltpu.SMEM)
    cps = CHUNKS_PER_STEP
    cur = pl.BlockSpec((1, cps * L, D_MODEL), lambda b, c: (b, c, 0))
    nxt = pl.BlockSpec((1, L, D_MODEL),
                       lambda b, c: (jnp.minimum(b + (cps * c + cps) // nb, bsz - 1), lax.rem(cps * c + cps, nb), 0))
    w_spec = pl.BlockSpec((None, D_MODEL, PROJ_COLS), lambda b, c: (layer, 0, 0), pipeline_mode=pl.Buffered(1))
    in_specs = [smem, smem, cur, nxt, par(params[0]), w_spec] + [par(p) for p in params[2:]]
    stage = [
        pltpu.VMEM((L + 8, SSD_CONV_CH), F32),
        pltpu.VMEM((L, ST_F_COLS), F32),
        pltpu.VMEM((L, ST_B_COLS), BF16),
    ]
    scratch = [
        pltpu.VMEM((SSD_GROUPS, SSD_STATE, 256), F32),
        pltpu.VMEM((2, 128, 128), F32),
        pltpu.VMEM((2, 128, 128), F32),
        pltpu.VMEM((8, 128), F32),
        pltpu.VMEM((2, s, 256), BF16),
        pltpu.VMEM((2, DIFF_HEADS, s, 128), BF16),
        pltpu.VMEM((4, DIFF_HEADS, L, L), F32),
        pltpu.VMEM((DIFF_HEADS, 2 * L, 128), BF16),
        pltpu.VMEM((DIFF_HEADS, 2 * L, L), F32),
        pltpu.VMEM((DIFF_HEADS, 2 * L, 128), F32),
        pltpu.VMEM((DIFF_HEADS, 2 * L, KEY_GROUP * L), F32),
        pltpu.VMEM((DIFF_HEADS, 2 * L, KEY_GROUP * L), F32),
    ] + stage + stage
    return pl.pallas_call(
        functools.partial(_mixer_kernel, layer=layer, lam_init=lam_init, n_blocks=nb),
        grid=(bsz, nb // cps),
        in_specs=in_specs,
        out_specs=pl.BlockSpec((1, cps * L, MIX_WIDTH), lambda b, c: (b, c, 0)),
        out_shape=jax.ShapeDtypeStruct((bsz, s, MIX_WIDTH), BF16),
        scratch_shapes=scratch,
        compiler_params=pltpu.CompilerParams(dimension_semantics=("arbitrary", "arbitrary"),
                                             vmem_limit_bytes=V7X_VMEM_LIMIT),
        name="mixer",
    )(lam, rel_bias.astype(F32), h3d, h3d, *params)


def _dense_kernel(h_ref, y_ref, p_ref, wo_ref, n2_ref, wg_ref, wu_ref, wd_ref, pg_ref, pp_ref, fn_ref,
                  o_ref, *, final, h_chunk):
    h1 = h_ref[...] + _dot(y_ref[...], wo_ref[...])
    var = jnp.mean(h1 * h1, axis=-1, keepdims=True)
    u = ((h1 * lax.rsqrt(var + NORM_EPS)) * n2_ref[...]).astype(BF16)
    ffn = jnp.zeros(h1.shape, F32)
    for c0 in range(0, FFN_HIDDEN, h_chunk):
        g = _dot(u, wg_ref[:, c0:c0 + h_chunk])
        up = _dot(u, wu_ref[:, c0:c0 + h_chunk])
        a = ((g * _sigmoid(g)) * up).astype(BF16)
        ffn = ffn + _dot(a, wd_ref[c0:c0 + h_chunk, :])
    h2 = h1 + ffn
    gate = _sigmoid(_dot(h2.astype(BF16), pg_ref[...]))
    h3 = h2 + gate * _dot(p_ref[...].astype(BF16), pp_ref[...])
    if final:
        fvar = jnp.mean(h3 * h3, axis=-1, keepdims=True)
        h3 = (h3 * lax.rsqrt(fvar + NORM_EPS)) * fn_ref[...]
    o_ref[...] = h3


def _dense(h2d, y2d, p_all, weights, final_norm_w, *, layer, final, tm):
    t = h2d.shape[0]
    row = lambda width: pl.BlockSpec((tm, width), lambda i: (i, 0))
    res = lambda a: pl.BlockSpec((None,) + a.shape[1:], lambda i: (layer, 0, 0), pipeline_mode=pl.Buffered(1))
    return pl.pallas_call(
        functools.partial(_dense_kernel, final=final, h_chunk=FFN_CHUNK),
        grid=(t // tm,),
        in_specs=[row(D_MODEL), row(MIX_WIDTH), pl.BlockSpec((None, tm, PLE_DIM), lambda i: (layer, i, 0))]
        + [res(w) for w in weights] + [pl.BlockSpec((1, D_MODEL), lambda i: (0, 0))],
        out_specs=row(D_MODEL),
        out_shape=jax.ShapeDtypeStruct((t, D_MODEL), F32),
        compiler_params=pltpu.CompilerParams(dimension_semantics=("arbitrary",),
                                             vmem_limit_bytes=V7X_VMEM_LIMIT),
        name="dense",
    )(h2d, y2d, p_all, *weights, final_norm_w.reshape(1, D_MODEL).astype(F32))


def _pack_w_in(w):
    w = w.astype(BF16)
    z, xbc, dt, dq, dk, dv, mq, mk, mv, mo, mi, mf = jnp.split(w, np.cumsum(IN_SPLITS)[:-1], axis=-1)
    zeros = lambda n: jnp.zeros(w.shape[:-1] + (n,), w.dtype)
    small_a = jnp.concatenate([dt, mi, zeros(128 - LANE_ML - MLSTM_HEADS)], axis=-1)
    small_b = jnp.concatenate([zeros(LANE_ML), mf, zeros(128 - LANE_ML - MLSTM_HEADS)], axis=-1)
    return jnp.concatenate([xbc, z, mo, small_a, small_b, dq, dk, dv, mq, mk, mv], axis=-1)


def _lanes(depth, *placed):
    row = jnp.zeros((depth, 128), F32)
    for off, v in placed:
        row = row.at[:, off:off + v.shape[1]].set(v.astype(F32))
    return row[:, None, :]


def kernel(x, p, norm1_w, w_in, ssd_conv_w, ssd_conv_b, ssd_dt_bias, ssd_a_log, ssd_d, ssd_norm_w, diff_lq1, diff_lk1, diff_lq2, diff_lk2, diff_norm_w, rel_bias, mlstm_i_bias, mlstm_f_bias, mlstm_norm_w, w_out, norm2_w, w_ffn_gate, w_ffn_up, w_ffn_down, ple_gate_w, ple_proj_w, final_norm_w):
    bsz, s, d = x.shape
    depth = w_in.shape[0]
    assert d == D_MODEL and s % (KEY_GROUP * CHUNK) == 0 and s % (CHUNKS_PER_STEP * CHUNK) == 0
    t = bsz * s
    tm = ROW_TILE if s % ROW_TILE == 0 else CHUNK
    row3 = lambda a: a.astype(F32)[:, None, :]

    lam_inits = [0.8 - 0.6 * math.exp(-0.3 * i) for i in range(depth)]
    lam = (jnp.exp(jnp.sum(diff_lq1.astype(F32) * diff_lk1.astype(F32), axis=-1))
           - jnp.exp(jnp.sum(diff_lq2.astype(F32) * diff_lk2.astype(F32), axis=-1))
           + jnp.asarray(lam_inits, F32))
    mixer_params = [
        row3(norm1_w), _pack_w_in(w_in), ssd_conv_w.astype(F32), row3(ssd_conv_b),
        _lanes(depth, (LANE_SSD, ssd_dt_bias), (LANE_ML, mlstm_i_bias)), _lanes(depth, (LANE_ML, mlstm_f_bias)),
        _lanes(depth, (LANE_SSD, ssd_a_log)),
        row3(jnp.repeat(ssd_d, SSD_HEAD_DIM, axis=1)), row3(ssd_norm_w),
        row3(jnp.tile(diff_norm_w, (1, DIFF_HEADS))), row3(mlstm_norm_w),
    ]
    dense_weights = [w_out.astype(BF16), row3(norm2_w), w_ffn_gate.astype(BF16), w_ffn_up.astype(BF16),
                     w_ffn_down.astype(BF16), ple_gate_w.astype(BF16), ple_proj_w.astype(BF16)]
    p_all = p.reshape(depth, t, PLE_DIM).astype(F32)

    h = x.reshape(t, d).astype(F32)
    for i in range(depth):
        y = _mixer(h.reshape(bsz, s, d), lam, rel_bias, mixer_params, layer=i, lam_init=lam_inits[i])
        h = _dense(h, y.reshape(t, MIX_WIDTH), p_all, dense_weights, final_norm_w, layer=i,
                   final=(i == depth - 1), tm=tm)
    return h.reshape(bsz, s, d).astype(x.dtype)
```

```python
import functools
import math

import numpy as np
import jax
import jax.numpy as jnp
from jax import lax
from jax.experimental import pallas as pl
from jax.experimental.pallas import tpu as pltpu

F32 = jnp.float32
BF16 = jnp.bfloat16

D_MODEL = 1024
PLE_DIM = 256
NORM_EPS = 1e-6
CHUNK = 128
KEY_GROUP = 4
CHUNKS_PER_STEP = 4

SSD_HEADS = 8
SSD_HEAD_DIM = 64
SSD_WIDTH = 512
SSD_GROUPS = 2
SSD_STATE = 128
SSD_CONV = 4
SSD_CONV_CH = 1024
DIFF_HEADS = 4
DIFF_QK_DIM = 32
DIFF_V_DIM = 64
DIFF_WIDTH = 256
REL_BUCKETS = 32
REL_MAX_DIST = 128
MLSTM_HEADS = 4
MLSTM_HEAD_DIM = 64
MLSTM_WIDTH = 256
MIX_WIDTH = 1024
FFN_HIDDEN = 2816
IN_SPLITS = (512, 1024, 8, 256, 256, 256, 256, 256, 256, 256, 4, 4)

W_XBC, W_ZMO, W_SMALL, W_QKV = 0, 1024, 1792, 2048
PROJ_COLS = W_QKV + 6 * 256
ST_Z, ST_MO, ST_SMALL_A, ST_SMALL_B, ST_F_COLS = 0, 512, 768, 896, 1024
ST_Q, ST_MQ, ST_MK, ST_MV, ST_B_COLS = 0, 256, 512, 768, 1024
LANE_SSD, LANE_ML = 0, 8
ATTN_C1 = (DIFF_QK_DIM ** -0.5) * math.log2(math.e)

ROW_TILE = 512
FFN_CHUNK = 2816

V7X_VMEM_LIMIT = 56 * 1024 * 1024
MASKED = -1e30
LOG2E = math.log2(math.e)


def _t5_bucket_lower_bounds():
    max_exact = REL_BUCKETS // 2
    d = np.arange(0, 2 * CHUNK, dtype=np.int64)
    df = np.maximum(d, 1).astype(np.float32)
    large = max_exact + (np.log(df / np.float32(max_exact)) / np.float32(math.log(REL_MAX_DIST / max_exact))
                         * np.float32(REL_BUCKETS - max_exact)).astype(np.int32)
    large = np.minimum(large, REL_BUCKETS - 1)
    bucket = np.where(d < max_exact, d, large)
    assert np.all(np.diff(bucket) >= 0) and bucket[CHUNK] == REL_BUCKETS - 1
    return [int(np.argmax(bucket >= b)) for b in range(REL_BUCKETS)]


_BUCKET_LO = _t5_bucket_lower_bounds()


def _sigmoid(x):
    return 0.5 * jnp.tanh(0.5 * x) + 0.5


def _softplus(x):
    return jnp.maximum(x, 0.0) + jnp.log(1.0 + jnp.exp(-jnp.abs(x)))


def _dot(a, b):
    return jnp.dot(a, b, preferred_element_type=F32)


def _dot_nt(a, b):
    return lax.dot_general(a, b, (((1,), (1,)), ((), ())), preferred_element_type=F32)


def _dot_tn(a, b):
    return lax.dot_general(a, b, (((0,), (0,)), ((), ())), preferred_element_type=F32)


def _scan_rows(x, combine, identity):
    sub = lax.broadcasted_iota(jnp.int32, (8, x.shape[1]), 0)
    tiles, carry = [], None
    for i in range(x.shape[0] // 8):
        t = x[8 * i:8 * (i + 1)]
        for sh in (1, 2, 4):
            t = combine(t, jnp.where(sub >= sh, pltpu.roll(t, sh, 0), identity))
        if carry is not None:
            t = combine(t, carry)
        carry = t[7:8]
        tiles.append(t)
    return jnp.concatenate(tiles, axis=0)


def _mixer_kernel(lam_ref, relb_ref, hc_ref, hn_ref, n1_ref, w_ref, convw_ref, convb_ref,
                  biasa_ref, biasb_ref, alog_ref, dfull_ref, ssdnw_ref, dnw_ref, mnw_ref,
                  y_ref,
                  prev_t, ct, nm, mst, kb, vb, bias, qbd, rstat, acc_s, s_even, s_odd,
                  xpad_a, stf_a, stb_a, xpad_b, stf_b, stb_b,
                  *, layer, lam_init, n_blocks):
    L = CHUNK
    b = pl.program_id(0)
    cp = pl.program_id(1)

    lane128 = lax.broadcasted_iota(jnp.int32, (L, 128), 1)
    lo = lane128 < 64
    row_i = lax.broadcasted_iota(jnp.int32, (L, L), 0)
    col_i = lax.broadcasted_iota(jnp.int32, (L, L), 1)
    tril = row_i >= col_i
    row_lo = row_i < 64

    first_step = jnp.logical_and(b == 0, cp == 0)

    @pl.when(first_step)
    def _():
        for h in range(DIFF_HEADS):
            bias[0, h] = jnp.full((L, L), MASKED, F32)
            bias[3, h] = jnp.zeros((L, L), F32)
            far = relb_ref[REL_BUCKETS - 1, h]
            for which, off in ((1, 0), (2, L)):
                dist = row_i - col_i + off
                v = jnp.full((L, L), far, F32)
                for bkt in range(REL_BUCKETS - 2, -1, -1):
                    v = jnp.where(dist < _BUCKET_LO[bkt + 1], relb_ref[bkt, h], v)
                bias[which, h] = jnp.where(dist >= 0, (v - far) * LOG2E, MASKED)

    @pl.when(cp == 0)
    def _():
        prev_t[...] = jnp.zeros(prev_t.shape, F32)
        ct[...] = jnp.zeros(ct.shape, F32)
        nm[...] = jnp.zeros(nm.shape, F32)
        mst[...] = jnp.zeros(mst.shape, F32)

    def normed(x):
        var = jnp.mean(x * x, axis=-1, keepdims=True)
        return ((x * lax.rsqrt(var + NORM_EPS)) * n1_ref[...]).astype(BF16)

    def project_a(u, xpad, stf):
        xpad[8:8 + L, :] = _dot(u, w_ref[:, W_XBC:W_ZMO])
        zmo = _dot(u, w_ref[:, W_ZMO:W_SMALL])
        z = zmo[:, :SSD_WIDTH]
        stf[:, ST_Z:ST_MO] = z * _sigmoid(z)
        stf[:, ST_MO:ST_SMALL_A] = _sigmoid(zmo[:, SSD_WIDTH:])

    def project_b(u, stf, stb, slot, row0):
        stf[:, ST_SMALL_A:] = _dot(u, w_ref[:, W_SMALL:W_QKV])
        qkv = _dot(u, w_ref[:, W_QKV:])
        stb[:, ST_Q:ST_MQ] = (qkv[:, 0:256] * ATTN_C1).astype(BF16)
        stb[:, ST_MQ:] = qkv[:, 768:].astype(BF16)
        kb[slot, pl.ds(row0, L), :] = qkv[:, 256:512].astype(BF16)
        v = qkv[:, 512:768].astype(BF16)
        for h in range(DIFF_HEADS):
            own = lo if h % 2 == 0 else jnp.logical_not(lo)
            vb[slot, h, pl.ds(row0, L), :] = jnp.where(own, v[:, 128 * (h // 2):128 * (h // 2 + 1)],
                                                       jnp.ones((), BF16))

    slot = lax.rem(b, 2)

    @pl.when(first_step)
    def _():
        xpad_a[...] = jnp.zeros(xpad_a.shape, F32)
        xpad_b[...] = jnp.zeros(xpad_b.shape, F32)
        kb[...] = jnp.zeros(kb.shape, BF16)
        vb[...] = jnp.zeros(vb.shape, BF16)
        u0 = normed(hc_ref[0, 0:L, :])
        project_a(u0, xpad_a, stf_a)
        project_b(u0, stf_a, stb_a, 0, 0)

    def chunk_block(c, y_rows, xpad, stf, stb, xpad_other):
        xpad[0:8, :] = jnp.where(c > 0, xpad_other[L:L + 8, :], 0.0)

        pre_a = stf[:, ST_SMALL_A:ST_SMALL_B] + biasa_ref[...]
        log_f = -_softplus(-(stf[:, ST_SMALL_B:] + biasb_ref[...]))
        ssd_lane = lane128 < LANE_ML
        a_neg = jnp.where(ssd_lane[0:1], -jnp.exp(alog_ref[...]), 0.0)
        dt_c = _softplus(pre_a)
        csum = _scan_rows(jnp.where(ssd_lane, dt_c * a_neg, log_f), jnp.add, 0.0)
        last = csum[L - 1:L]
        u_c = pre_a - csum
        m_intra = csum + _scan_rows(u_c, jnp.maximum, -jnp.inf)
        w_end = last - csum + pre_a
        m_loc = jnp.max(w_end, axis=0, keepdims=True)
        e_end = jnp.exp(w_end - m_loc)
        m_prev = mst[0:1]
        inter_log = csum + m_prev
        m_t = jnp.maximum(inter_log, m_intra)
        w_inter = jnp.exp(inter_log - m_t)
        a_col = csum - m_t
        en = jnp.exp(-m_t)
        m_new = jnp.maximum(last + m_prev, m_loc)
        a_prev = jnp.exp(last + m_prev - m_new)
        a_loc = jnp.exp(m_loc - m_new)
        rows_t = jnp.where(ssd_lane, csum, u_c).T

        def bcast(arr, j):
            return jnp.broadcast_to(arr[:, j:j + 1], (arr.shape[0], 128))

        def pair(arr, j0, k):
            return jnp.where(lo[:arr.shape[0]], bcast(arr, j0 + 2 * k), bcast(arr, j0 + 2 * k + 1))

        conv = convb_ref[...]
        for k in range(SSD_CONV):
            conv = conv + convw_ref[k:k + 1, :] * xpad[5 + k:5 + k + L, :]
        xc = conv * _sigmoid(conv)
        xs = xc[:, :SSD_WIDTH]
        bmat = xc[:, SSD_WIDTH:SSD_WIDTH + 256]
        cmat = xc[:, SSD_WIDTH + 256:]
        dfull = dfull_ref[...]
        cs_b = [bcast(csum, LANE_SSD + h) for h in range(SSD_HEADS)]
        xdt, xds, ecs, cdec = [], [], [], []
        for k in range(4):
            cs_p = jnp.where(lo, cs_b[2 * k], cs_b[2 * k + 1])
            last_p = pair(last, LANE_SSD, k)
            xdt.append(xs[:, 128 * k:128 * (k + 1)] * pair(dt_c, LANE_SSD, k))
            xds.append((xdt[k] * jnp.exp(last_p - cs_p)).astype(BF16))
            ecs.append(jnp.exp(cs_p))
            cdec.append(jnp.exp(last_p))

        y_slabs = []
        for g_ in range(SSD_GROUPS):
            bm_g = bmat[:, 128 * g_:128 * (g_ + 1)]
            cm_b = cmat[:, 128 * g_:128 * (g_ + 1)].astype(BF16)
            cb = _dot_nt(cm_b, bm_g.astype(BF16))
            s_loc = _dot_tn(bm_g.astype(BF16), jnp.concatenate(xds[2 * g_:2 * g_ + 2], axis=1))
            prev = prev_t[g_]
            y_off = _dot(cm_b, prev.astype(BF16))
            for kk in range(2):
                k = 2 * g_ + kk
                slab = xdt[k].astype(BF16)
                halves = []
                for hl in range(2):
                    h = 2 * k + hl
                    diff = cs_b[h] - rows_t[LANE_SSD + h:LANE_SSD + h + 1, :]
                    dec = jnp.exp(jnp.where(tril, diff, -jnp.inf))
                    halves.append(_dot((cb * dec).astype(BF16), slab))
                y_diag = jnp.where(lo, halves[0], halves[1])
                y_slabs.append(y_diag + y_off[:, 128 * kk:128 * (kk + 1)] * ecs[k]
                               + xs[:, 128 * k:128 * (k + 1)] * dfull[:, 128 * k:128 * (k + 1)])
            prev_t[g_] = prev * jnp.concatenate(cdec[2 * g_:2 * g_ + 2], axis=1) + s_loc
        y = jnp.concatenate(y_slabs, axis=1)
        yz = y * stf[:, ST_Z:ST_MO]
        var = jnp.mean(yz * yz, axis=-1, keepdims=True)
        y_ref[0, y_rows, 0:SSD_WIDTH] = ((yz * lax.rsqrt(var + NORM_EPS)) * ssdnw_ref[...]).astype(BF16)

        bd_mask = row_lo == (col_i < 64)
        inv_sqrt_d = MLSTM_HEAD_DIM ** -0.5
        ones_b = jnp.ones((L, 128), BF16)
        mq = stb[:, ST_MQ:ST_MK]
        mk = stb[:, ST_MK:ST_MV]
        mv = stb[:, ST_MV:]
        mo = stf[:, ST_MO:ST_SMALL_A]
        mnw = mnw_ref[...]
        m_slabs = []
        for k in range(2):
            sl = slice(128 * k, 128 * (k + 1))
            q_b = mq[:, sl]
            k_b = mk[:, sl]
            v_b = mv[:, sl]
            v1 = jnp.concatenate([v_b, ones_b], axis=1)
            intra = []
            for hl in range(2):
                h = 2 * k + hl
                qm = jnp.where(lo if hl == 0 else jnp.logical_not(lo), q_b, jnp.zeros((), BF16))
                qk = _dot_nt(qm, k_b)
                arg = bcast(a_col, LANE_ML + h) + rows_t[LANE_ML + h:LANE_ML + h + 1, :]
                w_intra = jnp.exp(jnp.where(tril, arg, -jnp.inf))
                intra.append(_dot(((qk * inv_sqrt_d) * w_intra).astype(BF16), v1))
            num_intra = jnp.where(lo, intra[0][:, :128], intra[1][:, :128])
            den_intra = jnp.where(lo, intra[0][:, 128:], intra[1][:, 128:])
            ct_prev = ct[k]
            nm_prev = nm[k]
            wi = pair(w_inter, LANE_ML, k)
            num = num_intra + wi * _dot(q_b, ct_prev.astype(BF16))
            den = den_intra + wi * _dot(q_b, nm_prev.astype(BF16))
            denom = jnp.maximum(jnp.abs(den), pair(en, LANE_ML, k))
            hs = num / denom
            sq = hs * hs
            s_lo = jnp.sum(jnp.where(lo, sq, 0.0), axis=1, keepdims=True)
            s_hi = jnp.sum(jnp.where(lo, 0.0, sq), axis=1, keepdims=True)
            hvar = jnp.where(lo, s_lo, s_hi) * (1.0 / MLSTM_HEAD_DIM)
            hn = (hs * lax.rsqrt(hvar + NORM_EPS)) * mnw[:, sl]
            m_slabs.append(mo[:, sl] * hn)
            e_p = pair(e_end, LANE_ML, k)
            loc = _dot_tn(k_b, jnp.concatenate([(v_b.astype(F32) * e_p).astype(BF16), e_p.astype(BF16)], axis=1)
                          ) * inv_sqrt_d
            ap = pair(a_prev, LANE_ML, k)
            al = pair(a_loc, LANE_ML, k)
            ct[k] = ct_prev * ap + jnp.where(bd_mask, loc[:, :128], 0.0) * al
            nm[k] = nm_prev * ap + jnp.where(bd_mask, loc[:, 128:], 0.0) * al
        mst[...] = jnp.broadcast_to(m_new, mst.shape)
        y_ref[0, y_rows, SSD_WIDTH + DIFF_WIDTH:] = jnp.concatenate(m_slabs, axis=1).astype(BF16)

        q = stb[:, ST_Q:ST_MQ]
        zero_b = jnp.zeros((), BF16)
        for h in range(DIFF_HEADS):
            qs = q[:, 128 * (h // 2):128 * (h // 2 + 1)]
            base = 64 * (h % 2)
            q0 = jnp.where((lane128 >= base) & (lane128 < base + 32), qs, zero_b)
            q1 = jnp.where((lane128 >= base + 32) & (lane128 < base + 64), qs, zero_b)
            qbd[h] = jnp.concatenate([q0, q1], axis=0)

    lam = lam_ref[layer]

    def attention(c, y_rows, xpad_other, stf_other, stb_other, x_next, slot_next, c_next):
        n_groups = c // KEY_GROUP + 1
        gw = KEY_GROUP * L

        def fill_scores(kg, dst):
            r0 = pl.multiple_of(kg * gw, gw)
            for h in range(DIFF_HEADS):
                keys = kb[slot, pl.ds(r0, gw), 128 * (h // 2):128 * (h // 2 + 1)]
                add = jnp.concatenate(
                    [bias[jnp.clip(c - KEY_GROUP * kg - j, -1, 2) + 1, h] for j in range(KEY_GROUP)], axis=1)
                dst[h] = _dot_nt(qbd[h], keys) + jnp.concatenate([add, add], axis=0)

        def consume(kg, src):
            r0 = pl.multiple_of(kg * gw, gw)
            for h in range(DIFF_HEADS):
                s = src[h]
                t = s[:, :L]
                for j in range(1, KEY_GROUP):
                    t = jnp.maximum(t, s[:, L * j:L * (j + 1)])
                m_old = rstat[h]
                m_new = jnp.maximum(m_old, jnp.max(t, axis=1, keepdims=True))
                p = jnp.concatenate([jnp.exp2(s[:, L * j:L * (j + 1)] - m_new) for j in range(KEY_GROUP)], axis=1)
                acc_s[h] = acc_s[h] * jnp.exp2(m_old - m_new) + _dot(p.astype(BF16), vb[slot, h, pl.ds(r0, gw), :])
                rstat[h] = m_new

        rstat[...] = jnp.full(rstat.shape, MASKED, F32)
        acc_s[...] = jnp.zeros(acc_s.shape, F32)
        fill_scores(0, s_even)

        def key_step(kg, carry):
            @pl.when(kg % 2 == 0)
            def _():
                fill_scores(kg + 1, s_odd)
                consume(kg, s_even)

            @pl.when(kg % 2 == 1)
            def _():
                fill_scores(kg + 1, s_even)
                consume(kg, s_odd)

            return carry

        g_last = n_groups - 1
        lax.fori_loop(0, g_last, key_step, 0)

        @pl.when(g_last % 2 == 0)
        def _():
            consume(g_last, s_even)

        @pl.when(g_last % 2 == 1)
        def _():
            consume(g_last, s_odd)

        d_slabs = []
        for k in range(2):
            a0, a1 = acc_s[2 * k], acc_s[2 * k + 1]
            lo2 = jnp.concatenate([lo, lo], axis=0)
            r = jnp.where(lo2, a0, a1) * (1.0 / pltpu.roll(jnp.where(lo2, a1, a0), 64, 1))
            o = r[:L] - lam * r[L:]
            sq = o * o
            s_lo = jnp.sum(jnp.where(lo, sq, 0.0), axis=1, keepdims=True)
            s_hi = jnp.sum(jnp.where(lo, 0.0, sq), axis=1, keepdims=True)
            ovar = jnp.where(lo, s_lo, s_hi) * (1.0 / DIFF_V_DIM)
            d_slabs.append(((o * lax.rsqrt(ovar + NORM_EPS)) * dnw_ref[:, 128 * k:128 * (k + 1)])
                           * (1.0 - lam_init))
        y_ref[0, y_rows, SSD_WIDTH:SSD_WIDTH + DIFF_WIDTH] = jnp.concatenate(d_slabs, axis=1).astype(BF16)

        u_next = normed(x_next)
        project_a(u_next, xpad_other, stf_other)
        project_b(u_next, stf_other, stb_other, slot_next, pl.multiple_of(c_next * L, L))

    nb = n_blocks
    sets = ((xpad_a, stf_a, stb_a), (xpad_b, stf_b, stb_b))
    for j in range(CHUNKS_PER_STEP):
        c = CHUNKS_PER_STEP * cp + j
        rows = slice(j * L, (j + 1) * L)
        if j + 1 < CHUNKS_PER_STEP:
            nxt = (hc_ref[0, (j + 1) * L:(j + 2) * L, :], slot, c + 1)
        else:
            nxt = (hn_ref[0], lax.rem(b + (c + 1) // nb, 2), lax.rem(c + 1, nb))
        chunk_block(c, rows, *sets[j % 2], sets[(j + 1) % 2][0])
        attention(c, rows, *sets[(j + 1) % 2], *nxt)


def _mixer(h3d, lam, rel_bias, params, *, layer, lam_init):
    bsz, s, _ = h3d.shape
    L = CHUNK
    nb = s // L

    def par(a):
        return pl.BlockSpec((None,) + a.shape[1:], lambda b, c: (layer, 0, 0))

    smem = pl.BlockSpec(memory_space=pltpu.SMEM)
    cps = CHUNKS_PER_STEP
    cur = pl.BlockSpec((1, cps * L, D_MODEL), lambda b, c: (b, c, 0))
    nxt = pl.BlockSpec((1, L, D_MODEL),
                       lambda b, c: (jnp.minimum(b + (cps * c + cps) // nb, bsz - 1), lax.rem(cps * c + cps, nb), 0))
    w_spec = pl.BlockSpec((None, D_MODEL, PROJ_COLS), lambda b, c: (layer, 0, 0), pipeline_mode=pl.Buffered(1))
    in_specs = [smem, smem, cur, nxt, par(params[0]), w_spec] + [par(p) for p in params[2:]]
    stage = [
        pltpu.VMEM((L + 8, SSD_CONV_CH), F32),
        pltpu.VMEM((L, ST_F_COLS), F32),
        pltpu.VMEM((L, ST_B_COLS), BF16),
    ]
    scratch = [
        pltpu.VMEM((SSD_GROUPS, SSD_STATE, 256), F32),
        pltpu.VMEM((2, 128, 128), F32),
        pltpu.VMEM((2, 128, 128), F32),
        pltpu.VMEM((8, 128), F32),
        pltpu.VMEM((2, s, 256), BF16),
        pltpu.VMEM((2, DIFF_HEADS, s, 128), BF16),
        pltpu.VMEM((4, DIFF_HEADS, L, L), F32),
        pltpu.VMEM((DIFF_HEADS, 2 * L, 128), BF16),
        pltpu.VMEM((DIFF_HEADS, 2 * L, L), F32),
        pltpu.VMEM((DIFF_HEADS, 2 * L, 128), F32),
        pltpu.VMEM((DIFF_HEADS, 2 * L, KEY_GROUP * L), F32),
        pltpu.VMEM((DIFF_HEADS, 2 * L, KEY_GROUP * L), F32),
    ] + stage + stage
    return pl.pallas_call(
        functools.partial(_mixer_kernel, layer=layer, lam_init=lam_init, n_blocks=nb),
        grid=(bsz, nb // cps),
        in_specs=in_specs,
        out_specs=pl.BlockSpec((1, cps * L, MIX_WIDTH), lambda b, c: (b, c, 0)),
        out_shape=jax.ShapeDtypeStruct((bsz, s, MIX_WIDTH), BF16),
        scratch_shapes=scratch,
        compiler_params=pltpu.CompilerParams(dimension_semantics=("arbitrary", "arbitrary"),
                                             vmem_limit_bytes=V7X_VMEM_LIMIT),
        name="mixer",
    )(lam, rel_bias.astype(F32), h3d, h3d, *params)


def _dense_kernel(h_ref, y_ref, p_ref, wo_ref, n2_ref, wg_ref, wu_ref, wd_ref, pg_ref, pp_ref, fn_ref,
                  o_ref, *, final, h_chunk):
    h1 = h_ref[...] + _dot(y_ref[...], wo_ref[...])
    var = jnp.mean(h1 * h1, axis=-1, keepdims=True)
    u = ((h1 * lax.rsqrt(var + NORM_EPS)) * n2_ref[...]).astype(BF16)
    ffn = jnp.zeros(h1.shape, F32)
    for c0 in range(0, FFN_HIDDEN, h_chunk):
        g = _dot(u, wg_ref[:, c0:c0 + h_chunk])
        up = _dot(u, wu_ref[:, c0:c0 + h_chunk])
        a = ((g * _sigmoid(g)) * up).astype(BF16)
        ffn = ffn + _dot(a, wd_ref[c0:c0 + h_chunk, :])
    h2 = h1 + ffn
    gate = _sigmoid(_dot(h2.astype(BF16), pg_ref[...]))
    h3 = h2 + gate * _dot(p_ref[...].astype(BF16), pp_ref[...])
    if final:
        fvar = jnp.mean(h3 * h3, axis=-1, keepdims=True)
        h3 = (h3 * lax.rsqrt(fvar + NORM_EPS)) * fn_ref[...]
    o_ref[...] = h3


def _dense(h2d, y2d, p_all, weights, final_norm_w, *, layer, final, tm):
    t = h2d.shape[0]
    row = lambda width: pl.BlockSpec((tm, width), lambda i: (i, 0))
    res = lambda a: pl.BlockSpec((None,) + a.shape[1:], lambda i: (layer, 0, 0), pipeline_mode=pl.Buffered(1))
    return pl.pallas_call(
        functools.partial(_dense_kernel, final=final, h_chunk=FFN_CHUNK),
        grid=(t // tm,),
        in_specs=[row(D_MODEL), row(MIX_WIDTH), pl.BlockSpec((None, tm, PLE_DIM), lambda i: (layer, i, 0))]
        + [res(w) for w in weights] + [pl.BlockSpec((1, D_MODEL), lambda i: (0, 0))],
        out_specs=row(D_MODEL),
        out_shape=jax.ShapeDtypeStruct((t, D_MODEL), F32),
        compiler_params=pltpu.CompilerParams(dimension_semantics=("arbitrary",),
                                             vmem_limit_bytes=V7X_VMEM_LIMIT),
        name="dense",
    )(h2d, y2d, p_all, *weights, final_norm_w.reshape(1, D_MODEL).astype(F32))


def _pack_w_in(w):
    z, xbc, dt, dq, dk, dv, mq, mk, mv, mo, mi, mf = jnp.split(w, np.cumsum(IN_SPLITS)[:-1], axis=-1)
    zeros = lambda n: jnp.zeros(w.shape[:-1] + (n,), w.dtype)
    small_a = jnp.concatenate([dt, mi, zeros(128 - LANE_ML - MLSTM_HEADS)], axis=-1)
    small_b = jnp.concatenate([zeros(LANE_ML), mf, zeros(128 - LANE_ML - MLSTM_HEADS)], axis=-1)
    return jnp.concatenate([xbc, z, mo, small_a, small_b, dq, dk, dv, mq, mk, mv], axis=-1).astype(BF16)


def _lanes(depth, *placed):
    row = jnp.zeros((depth, 128), F32)
    for off, v in placed:
        row = row.at[:, off:off + v.shape[1]].set(v.astype(F32))
    return row[:, None, :]


def kernel(x, p, norm1_w, w_in, ssd_conv_w, ssd_conv_b, ssd_dt_bias, ssd_a_log, ssd_d, ssd_norm_w, diff_lq1, diff_lk1, diff_lq2, diff_lk2, diff_norm_w, rel_bias, mlstm_i_bias, mlstm_f_bias, mlstm_norm_w, w_out, norm2_w, w_ffn_gate, w_ffn_up, w_ffn_down, ple_gate_w, ple_proj_w, final_norm_w):
    bsz, s, d = x.shape
    depth = w_in.shape[0]
    assert d == D_MODEL and s % (KEY_GROUP * CHUNK) == 0 and s % (CHUNKS_PER_STEP * CHUNK) == 0
    t = bsz * s
    tm = ROW_TILE if s % ROW_TILE == 0 else CHUNK
    row3 = lambda a: a.astype(F32)[:, None, :]

    lam_inits = [0.8 - 0.6 * math.exp(-0.3 * i) for i in range(depth)]
    lam = (jnp.exp(jnp.sum(diff_lq1.astype(F32) * diff_lk1.astype(F32), axis=-1))
           - jnp.exp(jnp.sum(diff_lq2.astype(F32) * diff_lk2.astype(F32), axis=-1))
           + jnp.asarray(lam_inits, F32))
    mixer_params = [
        row3(norm1_w), _pack_w_in(w_in), ssd_conv_w.astype(F32), row3(ssd_conv_b),
        _lanes(depth, (LANE_SSD, ssd_dt_bias), (LANE_ML, mlstm_i_bias)), _lanes(depth, (LANE_ML, mlstm_f_bias)),
        _lanes(depth, (LANE_SSD, ssd_a_log)),
        row3(jnp.repeat(ssd_d, SSD_HEAD_DIM, axis=1)), row3(ssd_norm_w),
        row3(jnp.tile(diff_norm_w, (1, DIFF_HEADS))), row3(mlstm_norm_w),
    ]
    dense_weights = [w_out.astype(BF16), row3(norm2_w), w_ffn_gate.astype(BF16), w_ffn_up.astype(BF16),
                     w_ffn_down.astype(BF16), ple_gate_w.astype(BF16), ple_proj_w.astype(BF16)]
    p_all = p.reshape(depth, t, PLE_DIM).astype(F32)

    h = x.reshape(t, d).astype(F32)
    for i in range(depth):
        y = _mixer(h.reshape(bsz, s, d), lam, rel_bias, mixer_params, layer=i, lam_init=lam_inits[i])
        h = _dense(h, y.reshape(t, MIX_WIDTH), p_all, dense_weights, final_norm_w, layer=i,
                   final=(i == depth - 1), tm=tm)
    return h.reshape(bsz, s, d).astype(x.dtype)
```

```python
import functools
import math

import numpy as np
import jax
import jax.numpy as jnp
from jax import lax
from jax.experimental import pallas as pl
from jax.experimental.pallas import tpu as pltpu

F32 = jnp.float32
BF16 = jnp.bfloat16

D_MODEL = 1024
PLE_DIM = 256
NORM_EPS = 1e-6
CHUNK = 128
KEY_GROUP = 4
CHUNKS_PER_STEP = 4

SSD_HEADS = 8
SSD_HEAD_DIM = 64
SSD_WIDTH = 512
SSD_GROUPS = 2
SSD_STATE = 128
SSD_CONV = 4
SSD_CONV_CH = 1024
DIFF_HEADS = 4
DIFF_QK_DIM = 32
DIFF_V_DIM = 64
DIFF_WIDTH = 256
REL_BUCKETS = 32
REL_MAX_DIST = 128
MLSTM_HEADS = 4
MLSTM_HEAD_DIM = 64
MLSTM_WIDTH = 256
MIX_WIDTH = 1024
FFN_HIDDEN = 2816
IN_SPLITS = (512, 1024, 8, 256, 256, 256, 256, 256, 256, 256, 4, 4)

W_XBC, W_ZMO, W_SMALL, W_QKV = 0, 1024, 1792, 2048
PROJ_COLS = W_QKV + 6 * 256
ST_Z, ST_MO, ST_SMALL_A, ST_SMALL_B, ST_F_COLS = 0, 512, 768, 896, 1024
ST_Q, ST_MQ, ST_MK, ST_MV, ST_B_COLS = 0, 256, 512, 768, 1024
LANE_SSD, LANE_ML = 0, 8
ATTN_C1 = (DIFF_QK_DIM ** -0.5) * math.log2(math.e)

ROW_TILE = 512
FFN_CHUNK = 2816

V7X_VMEM_LIMIT = 56 * 1024 * 1024
MASKED = -1e30
LOG2E = math.log2(math.e)


def _t5_bucket_lower_bounds():
    max_exact = REL_BUCKETS // 2
    d = np.arange(0, 2 * CHUNK, dtype=np.int64)
    df = np.maximum(d, 1).astype(np.float32)
    large = max_exact + (np.log(df / np.float32(max_exact)) / np.float32(math.log(REL_MAX_DIST / max_exact))
                         * np.float32(REL_BUCKETS - max_exact)).astype(np.int32)
    large = np.minimum(large, REL_BUCKETS - 1)
    bucket = np.where(d < max_exact, d, large)
    assert np.all(np.diff(bucket) >= 0) and bucket[CHUNK] == REL_BUCKETS - 1
    return [int(np.argmax(bucket >= b)) for b in range(REL_BUCKETS)]


_BUCKET_LO = _t5_bucket_lower_bounds()


def _sigmoid(x):
    return 0.5 * jnp.tanh(0.5 * x) + 0.5


def _softplus(x):
    return jnp.maximum(x, 0.0) + jnp.log(1.0 + jnp.exp(-jnp.abs(x)))


def _dot(a, b):
    return jnp.dot(a, b, preferred_element_type=F32)


def _dot_nt(a, b):
    return lax.dot_general(a, b, (((1,), (1,)), ((), ())), preferred_element_type=F32)


def _dot_tn(a, b):
    return lax.dot_general(a, b, (((0,), (0,)), ((), ())), preferred_element_type=F32)


def _scan_rows(x, combine, identity):
    sub = lax.broadcasted_iota(jnp.int32, (8, x.shape[1]), 0)
    tiles, carry = [], None
    for i in range(x.shape[0] // 8):
        t = x[8 * i:8 * (i + 1)]
        for sh in (1, 2, 4):
            t = combine(t, jnp.where(sub >= sh, pltpu.roll(t, sh, 0), identity))
        if carry is not None:
            t = combine(t, carry)
        carry = t[7:8]
        tiles.append(t)
    return jnp.concatenate(tiles, axis=0)


def _mixer_kernel(lam_ref, relb_ref, hc_ref, hn_ref, n1_ref, w_ref, convw_ref, convb_ref,
                  biasa_ref, biasb_ref, alog_ref, dfull_ref, ssdnw_ref, dnw_ref, mnw_ref,
                  y_ref,
                  prev_t, ct, nm, mst, kb, vb, bias, qbd, rstat, acc_s, s_even, s_odd,
                  xpad_a, stf_a, stb_a, xpad_b, stf_b, stb_b,
                  *, layer, lam_init, n_blocks):
    L = CHUNK
    b = pl.program_id(0)
    cp = pl.program_id(1)

    lane128 = lax.broadcasted_iota(jnp.int32, (L, 128), 1)
    lo = lane128 < 64
    row_i = lax.broadcasted_iota(jnp.int32, (L, L), 0)
    col_i = lax.broadcasted_iota(jnp.int32, (L, L), 1)
    tril = row_i >= col_i
    row_lo = row_i < 64

    first_step = jnp.logical_and(b == 0, cp == 0)

    @pl.when(first_step)
    def _():
        for h in range(DIFF_HEADS):
            bias[0, h] = jnp.full((L, L), MASKED, F32)
            bias[3, h] = jnp.zeros((L, L), F32)
            far = relb_ref[REL_BUCKETS - 1, h]
            for which, off in ((1, 0), (2, L)):
                dist = row_i - col_i + off
                v = jnp.full((L, L), far, F32)
                for bkt in range(REL_BUCKETS - 2, -1, -1):
                    v = jnp.where(dist < _BUCKET_LO[bkt + 1], relb_ref[bkt, h], v)
                bias[which, h] = jnp.where(dist >= 0, (v - far) * LOG2E, MASKED)

    @pl.when(cp == 0)
    def _():
        prev_t[...] = jnp.zeros(prev_t.shape, F32)
        ct[...] = jnp.zeros(ct.shape, F32)
        nm[...] = jnp.zeros(nm.shape, F32)
        mst[...] = jnp.zeros(mst.shape, F32)

    def normed(x):
        var = jnp.mean(x * x, axis=-1, keepdims=True)
        return ((x * lax.rsqrt(var + NORM_EPS)) * n1_ref[...]).astype(BF16)

    def project_a(u, xpad, stf):
        xpad[8:8 + L, :] = _dot(u, w_ref[:, W_XBC:W_ZMO])
        zmo = _dot(u, w_ref[:, W_ZMO:W_SMALL])
        z = zmo[:, :SSD_WIDTH]
        stf[:, ST_Z:ST_MO] = z * _sigmoid(z)
        stf[:, ST_MO:ST_SMALL_A] = _sigmoid(zmo[:, SSD_WIDTH:])

    def project_b(u, stf, stb, slot, row0):
        stf[:, ST_SMALL_A:] = _dot(u, w_ref[:, W_SMALL:W_QKV])
        qkv = _dot(u, w_ref[:, W_QKV:])
        stb[:, ST_Q:ST_MQ] = (qkv[:, 0:256] * ATTN_C1).astype(BF16)
        stb[:, ST_MQ:] = qkv[:, 768:].astype(BF16)
        kb[slot, pl.ds(row0, L), :] = qkv[:, 256:512].astype(BF16)
        v = qkv[:, 512:768].astype(BF16)
        for h in range(DIFF_HEADS):
            own = lo if h % 2 == 0 else jnp.logical_not(lo)
            vb[slot, h, pl.ds(row0, L), :] = jnp.where(own, v[:, 128 * (h // 2):128 * (h // 2 + 1)],
                                                       jnp.ones((), BF16))

    slot = lax.rem(b, 2)

    @pl.when(first_step)
    def _():
        xpad_a[...] = jnp.zeros(xpad_a.shape, F32)
        xpad_b[...] = jnp.zeros(xpad_b.shape, F32)
        kb[...] = jnp.zeros(kb.shape, BF16)
        vb[...] = jnp.zeros(vb.shape, BF16)
        u0 = normed(hc_ref[0, 0:L, :])
        project_a(u0, xpad_a, stf_a)
        project_b(u0, stf_a, stb_a, 0, 0)

    def chunk_block(c, y_rows, xpad, stf, stb, xpad_other):
        xpad[0:8, :] = jnp.where(c > 0, xpad_other[L:L + 8, :], 0.0)

        pre_a = stf[:, ST_SMALL_A:ST_SMALL_B] + biasa_ref[...]
        log_f = -_softplus(-(stf[:, ST_SMALL_B:] + biasb_ref[...]))
        ssd_lane = lane128 < LANE_ML
        a_neg = jnp.where(ssd_lane[0:1], -jnp.exp(alog_ref[...]), 0.0)
        dt_c = _softplus(pre_a)
        csum = _scan_rows(jnp.where(ssd_lane, dt_c * a_neg, log_f), jnp.add, 0.0)
        last = csum[L - 1:L]
        u_c = pre_a - csum
        m_intra = csum + _scan_rows(u_c, jnp.maximum, -jnp.inf)
        w_end = last - csum + pre_a
        m_loc = jnp.max(w_end, axis=0, keepdims=True)
        e_end = jnp.exp(w_end - m_loc)
        m_prev = mst[0:1]
        inter_log = csum + m_prev
        m_t = jnp.maximum(inter_log, m_intra)
        w_inter = jnp.exp(inter_log - m_t)
        a_col = csum - m_t
        en = jnp.exp(-m_t)
        m_new = jnp.maximum(last + m_prev, m_loc)
        a_prev = jnp.exp(last + m_prev - m_new)
        a_loc = jnp.exp(m_loc - m_new)
        rows_t = jnp.where(ssd_lane, csum, u_c).T

        def bcast(arr, j):
            return jnp.broadcast_to(arr[:, j:j + 1], (arr.shape[0], 128))

        def pair(arr, j0, k):
            return jnp.where(lo[:arr.shape[0]], bcast(arr, j0 + 2 * k), bcast(arr, j0 + 2 * k + 1))

        conv = convb_ref[...]
        for k in range(SSD_CONV):
            conv = conv + convw_ref[k:k + 1, :] * xpad[5 + k:5 + k + L, :]
        xc = conv * _sigmoid(conv)
        xs = xc[:, :SSD_WIDTH]
        bmat = xc[:, SSD_WIDTH:SSD_WIDTH + 256]
        cmat = xc[:, SSD_WIDTH + 256:]
        dfull = dfull_ref[...]
        cs_b = [bcast(csum, LANE_SSD + h) for h in range(SSD_HEADS)]
        xdt, xds, ecs, cdec = [], [], [], []
        for k in range(4):
            cs_p = jnp.where(lo, cs_b[2 * k], cs_b[2 * k + 1])
            last_p = pair(last, LANE_SSD, k)
            xdt.append(xs[:, 128 * k:128 * (k + 1)] * pair(dt_c, LANE_SSD, k))
            xds.append((xdt[k] * jnp.exp(last_p - cs_p)).astype(BF16))
            ecs.append(jnp.exp(cs_p))
            cdec.append(jnp.exp(last_p))

        y_slabs = []
        for g_ in range(SSD_GROUPS):
            bm_g = bmat[:, 128 * g_:128 * (g_ + 1)]
            cm_b = cmat[:, 128 * g_:128 * (g_ + 1)].astype(BF16)
            cb = _dot_nt(cm_b, bm_g.astype(BF16))
            s_loc = _dot_tn(bm_g.astype(BF16), jnp.concatenate(xds[2 * g_:2 * g_ + 2], axis=1))
            prev = prev_t[g_]
            y_off = _dot(cm_b, prev.astype(BF16))
            for kk in range(2):
                k = 2 * g_ + kk
                slab = xdt[k].astype(BF16)
                halves = []
                for hl in range(2):
                    h = 2 * k + hl
                    diff = cs_b[h] - rows_t[LANE_SSD + h:LANE_SSD + h + 1, :]
                    dec = jnp.exp(jnp.where(tril, diff, -jnp.inf))
                    halves.append(_dot((cb * dec).astype(BF16), slab))
                y_diag = jnp.where(lo, halves[0], halves[1])
                y_slabs.append(y_diag + y_off[:, 128 * kk:128 * (kk + 1)] * ecs[k]
                               + xs[:, 128 * k:128 * (k + 1)] * dfull[:, 128 * k:128 * (k + 1)])
            prev_t[g_] = prev * jnp.concatenate(cdec[2 * g_:2 * g_ + 2], axis=1) + s_loc
        y = jnp.concatenate(y_slabs, axis=1)
        yz = y * stf[:, ST_Z:ST_MO]
        var = jnp.mean(yz * yz, axis=-1, keepdims=True)
        y_ref[0, y_rows, 0:SSD_WIDTH] = ((yz * lax.rsqrt(var + NORM_EPS)) * ssdnw_ref[...]).astype(BF16)

        bd_mask = row_lo == (col_i < 64)
        inv_sqrt_d = MLSTM_HEAD_DIM ** -0.5
        ones_b = jnp.ones((L, 128), BF16)
        mq = stb[:, ST_MQ:ST_MK]
        mk = stb[:, ST_MK:ST_MV]
        mv = stb[:, ST_MV:]
        mo = stf[:, ST_MO:ST_SMALL_A]
        mnw = mnw_ref[...]
        m_slabs = []
        for k in range(2):
            sl = slice(128 * k, 128 * (k + 1))
            q_b = mq[:, sl]
            k_b = mk[:, sl]
            v_b = mv[:, sl]
            v1 = jnp.concatenate([v_b, ones_b], axis=1)
            intra = []
            for hl in range(2):
                h = 2 * k + hl
                qm = jnp.where(lo if hl == 0 else jnp.logical_not(lo), q_b, jnp.zeros((), BF16))
                qk = _dot_nt(qm, k_b)
                arg = bcast(a_col, LANE_ML + h) + rows_t[LANE_ML + h:LANE_ML + h + 1, :]
                w_intra = jnp.exp(jnp.where(tril, arg, -jnp.inf))
                intra.append(_dot(((qk * inv_sqrt_d) * w_intra).astype(BF16), v1))
            num_intra = jnp.where(lo, intra[0][:, :128], intra[1][:, :128])
            den_intra = jnp.where(lo, intra[0][:, 128:], intra[1][:, 128:])
            ct_prev = ct[k]
            nm_prev = nm[k]
            wi = pair(w_inter, LANE_ML, k)
            inter = _dot(q_b, jnp.concatenate([ct_prev, nm_prev], axis=1).astype(BF16))
            num = num_intra + wi * inter[:, :128]
            den = den_intra + wi * inter[:, 128:]
            denom = jnp.maximum(jnp.abs(den), pair(en, LANE_ML, k))
            hs = num / denom
            sq = hs * hs
            s_lo = jnp.sum(jnp.where(lo, sq, 0.0), axis=1, keepdims=True)
            s_hi = jnp.sum(jnp.where(lo, 0.0, sq), axis=1, keepdims=True)
            hvar = jnp.where(lo, s_lo, s_hi) * (1.0 / MLSTM_HEAD_DIM)
            hn = (hs * lax.rsqrt(hvar + NORM_EPS)) * mnw[:, sl]
            m_slabs.append(mo[:, sl] * hn)
            e_p = pair(e_end, LANE_ML, k)
            loc = _dot_tn(k_b, jnp.concatenate([(v_b.astype(F32) * e_p).astype(BF16), e_p.astype(BF16)], axis=1)
                          ) * inv_sqrt_d
            ap = pair(a_prev, LANE_ML, k)
            al = pair(a_loc, LANE_ML, k)
            ct[k] = ct_prev * ap + jnp.where(bd_mask, loc[:, :128], 0.0) * al
            nm[k] = nm_prev * ap + jnp.where(bd_mask, loc[:, 128:], 0.0) * al
        mst[...] = jnp.broadcast_to(m_new, mst.shape)
        y_ref[0, y_rows, SSD_WIDTH + DIFF_WIDTH:] = jnp.concatenate(m_slabs, axis=1).astype(BF16)

        q = stb[:, ST_Q:ST_MQ]
        zero_b = jnp.zeros((), BF16)
        for h in range(DIFF_HEADS):
            qs = q[:, 128 * (h // 2):128 * (h // 2 + 1)]
            base = 64 * (h % 2)
            q0 = jnp.where((lane128 >= base) & (lane128 < base + 32), qs, zero_b)
            q1 = jnp.where((lane128 >= base + 32) & (lane128 < base + 64), qs, zero_b)
            qbd[h] = jnp.concatenate([q0, q1], axis=0)

    lam = lam_ref[layer]

    def attention(c, y_rows, xpad_other, stf_other, stb_other, x_next, slot_next, c_next):
        n_groups = c // KEY_GROUP + 1
        gw = KEY_GROUP * L

        def fill_scores(kg, dst):
            r0 = pl.multiple_of(kg * gw, gw)
            for h in range(DIFF_HEADS):
                keys = kb[slot, pl.ds(r0, gw), 128 * (h // 2):128 * (h // 2 + 1)]
                add = jnp.concatenate(
                    [bias[jnp.clip(c - KEY_GROUP * kg - j, -1, 2) + 1, h] for j in range(KEY_GROUP)], axis=1)
                dst[h] = _dot_nt(qbd[h], keys) + jnp.concatenate([add, add], axis=0)

        def consume(kg, src):
            r0 = pl.multiple_of(kg * gw, gw)
            for h in range(DIFF_HEADS):
                s = src[h]
                t = s[:, :L]
                for j in range(1, KEY_GROUP):
                    t = jnp.maximum(t, s[:, L * j:L * (j + 1)])
                m_old = rstat[h]
                m_new = jnp.maximum(m_old, jnp.max(t, axis=1, keepdims=True))
                p = jnp.concatenate([jnp.exp2(s[:, L * j:L * (j + 1)] - m_new) for j in range(KEY_GROUP)], axis=1)
                acc_s[h] = acc_s[h] * jnp.exp2(m_old - m_new) + _dot(p.astype(BF16), vb[slot, h, pl.ds(r0, gw), :])
                rstat[h] = m_new

        rstat[...] = jnp.full(rstat.shape, MASKED, F32)
        acc_s[...] = jnp.zeros(acc_s.shape, F32)
        fill_scores(0, s_even)

        def key_step(kg, carry):
            @pl.when(kg % 2 == 0)
            def _():
                fill_scores(kg + 1, s_odd)
                consume(kg, s_even)

            @pl.when(kg % 2 == 1)
            def _():
                fill_scores(kg + 1, s_even)
                consume(kg, s_odd)

            return carry

        g_last = n_groups - 1
        lax.fori_loop(0, g_last, key_step, 0)

        @pl.when(g_last % 2 == 0)
        def _():
            consume(g_last, s_even)

        @pl.when(g_last % 2 == 1)
        def _():
            consume(g_last, s_odd)

        d_slabs = []
        for k in range(2):
            a0, a1 = acc_s[2 * k], acc_s[2 * k + 1]
            lo2 = jnp.concatenate([lo, lo], axis=0)
            r = jnp.where(lo2, a0, a1) * (1.0 / pltpu.roll(jnp.where(lo2, a1, a0), 64, 1))
            o = r[:L] - lam * r[L:]
            sq = o * o
            s_lo = jnp.sum(jnp.where(lo, sq, 0.0), axis=1, keepdims=True)
            s_hi = jnp.sum(jnp.where(lo, 0.0, sq), axis=1, keepdims=True)
            ovar = jnp.where(lo, s_lo, s_hi) * (1.0 / DIFF_V_DIM)
            d_slabs.append(((o * lax.rsqrt(ovar + NORM_EPS)) * dnw_ref[:, 128 * k:128 * (k + 1)])
                           * (1.0 - lam_init))
        y_ref[0, y_rows, SSD_WIDTH:SSD_WIDTH + DIFF_WIDTH] = jnp.concatenate(d_slabs, axis=1).astype(BF16)

        u_next = normed(x_next)
        project_a(u_next, xpad_other, stf_other)
        project_b(u_next, stf_other, stb_other, slot_next, pl.multiple_of(c_next * L, L))

    nb = n_blocks
    sets = ((xpad_a, stf_a, stb_a), (xpad_b, stf_b, stb_b))
    for j in range(CHUNKS_PER_STEP):
        c = CHUNKS_PER_STEP * cp + j
        rows = slice(j * L, (j + 1) * L)
        if j + 1 < CHUNKS_PER_STEP:
            nxt = (hc_ref[0, (j + 1) * L:(j + 2) * L, :], slot, c + 1)
        else:
            nxt = (hn_ref[0], lax.rem(b + (c + 1) // nb, 2), lax.rem(c + 1, nb))
        chunk_block(c, rows, *sets[j % 2], sets[(j + 1) % 2][0])
        attention(c, rows, *sets[(j + 1) % 2], *nxt)


def _mixer(h3d, lam, rel_bias, params, *, layer, lam_init):
    bsz, s, _ = h3d.shape
    L = CHUNK
    nb = s // L

    def par(a):
        return pl.BlockSpec((None,) + a.shape[1:], lambda b, c: (layer, 0, 0))

    smem = pl.BlockSpec(memory_space=pltpu.SMEM)
    cps = CHUNKS_PER_STEP
    cur = pl.BlockSpec((1, cps * L, D_MODEL), lambda b, c: (b, c, 0))
    nxt = pl.BlockSpec((1, L, D_MODEL),
                       lambda b, c: (jnp.minimum(b + (cps * c + cps) // nb, bsz - 1), lax.rem(cps * c + cps, nb), 0))
    w_spec = pl.BlockSpec((None, D_MODEL, PROJ_COLS), lambda b, c: (layer, 0, 0), pipeline_mode=pl.Buffered(1))
    in_specs = [smem, smem, cur, nxt, par(params[0]), w_spec] + [par(p) for p in params[2:]]
    stage = [
        pltpu.VMEM((L + 8, SSD_CONV_CH), F32),
        pltpu.VMEM((L, ST_F_COLS), F32),
        pltpu.VMEM((L, ST_B_COLS), BF16),
    ]
    scratch = [
        pltpu.VMEM((SSD_GROUPS, SSD_STATE, 256), F32),
        pltpu.VMEM((2, 128, 128), F32),
        pltpu.VMEM((2, 128, 128), F32),
        pltpu.VMEM((8, 128), F32),
        pltpu.VMEM((2, s, 256), BF16),
        pltpu.VMEM((2, DIFF_HEADS, s, 128), BF16),
        pltpu.VMEM((4, DIFF_HEADS, L, L), F32),
        pltpu.VMEM((DIFF_HEADS, 2 * L, 128), BF16),
        pltpu.VMEM((DIFF_HEADS, 2 * L, L), F32),
        pltpu.VMEM((DIFF_HEADS, 2 * L, 128), F32),
        pltpu.VMEM((DIFF_HEADS, 2 * L, KEY_GROUP * L), F32),
        pltpu.VMEM((DIFF_HEADS, 2 * L, KEY_GROUP * L), F32),
    ] + stage + stage
    return pl.pallas_call(
        functools.partial(_mixer_kernel, layer=layer, lam_init=lam_init, n_blocks=nb),
        grid=(bsz, nb // cps),
        in_specs=in_specs,
        out_specs=pl.BlockSpec((1, cps * L, MIX_WIDTH), lambda b, c: (b, c, 0)),
        out_shape=jax.ShapeDtypeStruct((bsz, s, MIX_WIDTH), BF16),
        scratch_shapes=scratch,
        compiler_params=pltpu.CompilerParams(dimension_semantics=("arbitrary", "arbitrary"),
                                             vmem_limit_bytes=V7X_VMEM_LIMIT),
        name="mixer",
    )(lam, rel_bias.astype(F32), h3d, h3d, *params)


def _dense_kernel(h_ref, y_ref, p_ref, wo_ref, n2_ref, wg_ref, wu_ref, wd_ref, pg_ref, pp_ref, fn_ref,
                  o_ref, *, final, h_chunk):
    h1 = h_ref[...] + _dot(y_ref[...], wo_ref[...])
    var = jnp.mean(h1 * h1, axis=-1, keepdims=True)
    u = ((h1 * lax.rsqrt(var + NORM_EPS)) * n2_ref[...]).astype(BF16)
    ffn = jnp.zeros(h1.shape, F32)
    for c0 in range(0, FFN_HIDDEN, h_chunk):
        g = _dot(u, wg_ref[:, c0:c0 + h_chunk])
        up = _dot(u, wu_ref[:, c0:c0 + h_chunk])
        a = ((g * _sigmoid(g)) * up).astype(BF16)
        ffn = ffn + _dot(a, wd_ref[c0:c0 + h_chunk, :])
    h2 = h1 + ffn
    gate = _sigmoid(_dot(h2.astype(BF16), pg_ref[...]))
    h3 = h2 + gate * _dot(p_ref[...].astype(BF16), pp_ref[...])
    if final:
        fvar = jnp.mean(h3 * h3, axis=-1, keepdims=True)
        h3 = (h3 * lax.rsqrt(fvar + NORM_EPS)) * fn_ref[...]
    o_ref[...] = h3


def _dense(h2d, y2d, p_all, weights, final_norm_w, *, layer, final, tm):
    t = h2d.shape[0]
    row = lambda width: pl.BlockSpec((tm, width), lambda i: (i, 0))
    res = lambda a: pl.BlockSpec((None,) + a.shape[1:], lambda i: (layer, 0, 0), pipeline_mode=pl.Buffered(1))
    return pl.pallas_call(
        functools.partial(_dense_kernel, final=final, h_chunk=FFN_CHUNK),
        grid=(t // tm,),
        in_specs=[row(D_MODEL), row(MIX_WIDTH), pl.BlockSpec((None, tm, PLE_DIM), lambda i: (layer, i, 0))]
        + [res(w) for w in weights] + [pl.BlockSpec((1, D_MODEL), lambda i: (0, 0))],
        out_specs=row(D_MODEL),
        out_shape=jax.ShapeDtypeStruct((t, D_MODEL), F32),
        compiler_params=pltpu.CompilerParams(dimension_semantics=("arbitrary",),
                                             vmem_limit_bytes=V7X_VMEM_LIMIT),
        name="dense",
    )(h2d, y2d, p_all, *weights, final_norm_w.reshape(1, D_MODEL).astype(F32))


def _pack_w_in(w):
    z, xbc, dt, dq, dk, dv, mq, mk, mv, mo, mi, mf = jnp.split(w, np.cumsum(IN_SPLITS)[:-1], axis=-1)
    zeros = lambda n: jnp.zeros(w.shape[:-1] + (n,), w.dtype)
    small_a = jnp.concatenate([dt, mi, zeros(128 - LANE_ML - MLSTM_HEADS)], axis=-1)
    small_b = jnp.concatenate([zeros(LANE_ML), mf, zeros(128 - LANE_ML - MLSTM_HEADS)], axis=-1)
    return jnp.concatenate([xbc, z, mo, small_a, small_b, dq, dk, dv, mq, mk, mv], axis=-1).astype(BF16)


def _lanes(depth, *placed):
    row = jnp.zeros((depth, 128), F32)
    for off, v in placed:
        row = row.at[:, off:off + v.shape[1]].set(v.astype(F32))
    return row[:, None, :]


def kernel(x, p, norm1_w, w_in, ssd_conv_w, ssd_conv_b, ssd_dt_bias, ssd_a_log, ssd_d, ssd_norm_w, diff_lq1, diff_lk1, diff_lq2, diff_lk2, diff_norm_w, rel_bias, mlstm_i_bias, mlstm_f_bias, mlstm_norm_w, w_out, norm2_w, w_ffn_gate, w_ffn_up, w_ffn_down, ple_gate_w, ple_proj_w, final_norm_w):
    bsz, s, d = x.shape
    depth = w_in.shape[0]
    assert d == D_MODEL and s % (KEY_GROUP * CHUNK) == 0 and s % (CHUNKS_PER_STEP * CHUNK) == 0
    t = bsz * s
    tm = ROW_TILE if s % ROW_TILE == 0 else CHUNK
    row3 = lambda a: a.astype(F32)[:, None, :]

    lam_inits = [0.8 - 0.6 * math.exp(-0.3 * i) for i in range(depth)]
    lam = (jnp.exp(jnp.sum(diff_lq1.astype(F32) * diff_lk1.astype(F32), axis=-1))
           - jnp.exp(jnp.sum(diff_lq2.astype(F32) * diff_lk2.astype(F32), axis=-1))
           + jnp.asarray(lam_inits, F32))
    mixer_params = [
        row3(norm1_w), _pack_w_in(w_in), ssd_conv_w.astype(F32), row3(ssd_conv_b),
        _lanes(depth, (LANE_SSD, ssd_dt_bias), (LANE_ML, mlstm_i_bias)), _lanes(depth, (LANE_ML, mlstm_f_bias)),
        _lanes(depth, (LANE_SSD, ssd_a_log)),
        row3(jnp.repeat(ssd_d, SSD_HEAD_DIM, axis=1)), row3(ssd_norm_w),
        row3(jnp.tile(diff_norm_w, (1, DIFF_HEADS))), row3(mlstm_norm_w),
    ]
    dense_weights = [w_out.astype(BF16), row3(norm2_w), w_ffn_gate.astype(BF16), w_ffn_up.astype(BF16),
                     w_ffn_down.astype(BF16), ple_gate_w.astype(BF16), ple_proj_w.astype(BF16)]
    p_all = p.reshape(depth, t, PLE_DIM).astype(F32)

    h = x.reshape(t, d).astype(F32)
    for i in range(depth):
        y = _mixer(h.reshape(bsz, s, d), lam, rel_bias, mixer_params, layer=i, lam_init=lam_inits[i])
        h = _dense(h, y.reshape(t, MIX_WIDTH), p_all, dense_weights, final_norm_w, layer=i,
                   final=(i == depth - 1), tm=tm)
    return h.reshape(bsz, s, d).astype(x.dtype)
```

```python
import functools
import math

import numpy as np
import jax
import jax.numpy as jnp
from jax import lax
from jax.experimental import pallas as pl
from jax.experimental.pallas import tpu as pltpu

F32 = jnp.float32
BF16 = jnp.bfloat16

D_MODEL = 1024
PLE_DIM = 256
NORM_EPS = 1e-6
CHUNK = 128
KEY_GROUP = 4
CHUNKS_PER_STEP = 4

SSD_HEADS = 8
SSD_HEAD_DIM = 64
SSD_WIDTH = 512
SSD_GROUPS = 2
SSD_STATE = 128
SSD_CONV = 4
SSD_CONV_CH = 1024
DIFF_HEADS = 4
DIFF_QK_DIM = 32
DIFF_V_DIM = 64
DIFF_WIDTH = 256
REL_BUCKETS = 32
REL_MAX_DIST = 128
MLSTM_HEADS = 4
MLSTM_HEAD_DIM = 64
MLSTM_WIDTH = 256
MIX_WIDTH = 1024
FFN_HIDDEN = 2816
IN_SPLITS = (512, 1024, 8, 256, 256, 256, 256, 256, 256, 256, 4, 4)

W_XBC, W_ZMO, W_SMALL, W_QKV = 0, 1024, 1792, 2048
PROJ_COLS = W_QKV + 6 * 256
ST_Z, ST_MO, ST_SMALL_A, ST_SMALL_B, ST_F_COLS = 0, 512, 768, 896, 1024
ST_Q, ST_MQ, ST_MK, ST_MV, ST_B_COLS = 0, 256, 512, 768, 1024
LANE_SSD, LANE_ML = 0, 8
ATTN_C1 = (DIFF_QK_DIM ** -0.5) * math.log2(math.e)

ROW_TILE = 512
FFN_CHUNK = 2816

V7X_VMEM_LIMIT = 56 * 1024 * 1024
MASKED = -1e30
LOG2E = math.log2(math.e)


def _t5_bucket_lower_bounds():
    max_exact = REL_BUCKETS // 2
    d = np.arange(0, 2 * CHUNK, dtype=np.int64)
    df = np.maximum(d, 1).astype(np.float32)
    large = max_exact + (np.log(df / np.float32(max_exact)) / np.float32(math.log(REL_MAX_DIST / max_exact))
                         * np.float32(REL_BUCKETS - max_exact)).astype(np.int32)
    large = np.minimum(large, REL_BUCKETS - 1)
    bucket = np.where(d < max_exact, d, large)
    assert np.all(np.diff(bucket) >= 0) and bucket[CHUNK] == REL_BUCKETS - 1
    return [int(np.argmax(bucket >= b)) for b in range(REL_BUCKETS)]


_BUCKET_LO = _t5_bucket_lower_bounds()


def _sigmoid(x):
    return 0.5 * jnp.tanh(0.5 * x) + 0.5


def _softplus(x):
    return jnp.maximum(x, 0.0) + jnp.log(1.0 + jnp.exp(-jnp.abs(x)))


def _dot(a, b):
    return jnp.dot(a, b, preferred_element_type=F32)


def _dot_nt(a, b):
    return lax.dot_general(a, b, (((1,), (1,)), ((), ())), preferred_element_type=F32)


def _dot_tn(a, b):
    return lax.dot_general(a, b, (((0,), (0,)), ((), ())), preferred_element_type=F32)


def _scan_rows(x, combine, identity):
    sub = lax.broadcasted_iota(jnp.int32, (8, x.shape[1]), 0)
    tiles, carry = [], None
    for i in range(x.shape[0] // 8):
        t = x[8 * i:8 * (i + 1)]
        for sh in (1, 2, 4):
            t = combine(t, jnp.where(sub >= sh, pltpu.roll(t, sh, 0), identity))
        if carry is not None:
            t = combine(t, carry)
        carry = t[7:8]
        tiles.append(t)
    return jnp.concatenate(tiles, axis=0)


def _mixer_kernel(lam_ref, relb_ref, hc_ref, hn_ref, n1_ref, w_ref, convw_ref, convb_ref,
                  biasa_ref, biasb_ref, alog_ref, dfull_ref, ssdnw_ref, dnw_ref, mnw_ref,
                  y_ref,
                  prev_t, ct, nm, mst, kb, vb, bias, qbd, rstat, acc_s, s_even, s_odd,
                  xpad_a, stf_a, stb_a, xpad_b, stf_b, stb_b,
                  *, layer, lam_init, n_blocks):
    L = CHUNK
    b = pl.program_id(0)
    cp = pl.program_id(1)

    lane128 = lax.broadcasted_iota(jnp.int32, (L, 128), 1)
    lo = lane128 < 64
    row_i = lax.broadcasted_iota(jnp.int32, (L, L), 0)
    col_i = lax.broadcasted_iota(jnp.int32, (L, L), 1)
    tril = row_i >= col_i
    row_lo = row_i < 64

    first_step = jnp.logical_and(b == 0, cp == 0)

    @pl.when(first_step)
    def _():
        for h in range(DIFF_HEADS):
            bias[0, h] = jnp.full((L, L), MASKED, F32)
            bias[3, h] = jnp.zeros((L, L), F32)
            far = relb_ref[REL_BUCKETS - 1, h]
            for which, off in ((1, 0), (2, L)):
                dist = row_i - col_i + off
                v = jnp.full((L, L), far, F32)
                for bkt in range(REL_BUCKETS - 2, -1, -1):
                    v = jnp.where(dist < _BUCKET_LO[bkt + 1], relb_ref[bkt, h], v)
                bias[which, h] = jnp.where(dist >= 0, (v - far) * LOG2E, MASKED)

    @pl.when(cp == 0)
    def _():
        prev_t[...] = jnp.zeros(prev_t.shape, F32)
        ct[...] = jnp.zeros(ct.shape, F32)
        nm[...] = jnp.zeros(nm.shape, F32)
        mst[...] = jnp.zeros(mst.shape, F32)

    def normed(x):
        var = jnp.mean(x * x, axis=-1, keepdims=True)
        return ((x * lax.rsqrt(var + NORM_EPS)) * n1_ref[...]).astype(BF16)

    def project_a(u, xpad, stf):
        xpad[8:8 + L, :] = _dot(u, w_ref[:, W_XBC:W_ZMO])
        zmo = _dot(u, w_ref[:, W_ZMO:W_SMALL])
        z = zmo[:, :SSD_WIDTH]
        stf[:, ST_Z:ST_MO] = z * _sigmoid(z)
        stf[:, ST_MO:ST_SMALL_A] = _sigmoid(zmo[:, SSD_WIDTH:])

    def project_b(u, stf, stb, slot, row0):
        stf[:, ST_SMALL_A:] = _dot(u, w_ref[:, W_SMALL:W_QKV])
        qkv = _dot(u, w_ref[:, W_QKV:])
        stb[:, ST_Q:ST_MQ] = (qkv[:, 0:256] * ATTN_C1).astype(BF16)
        stb[:, ST_MQ:] = qkv[:, 768:].astype(BF16)
        kb[slot, pl.ds(row0, L), :] = qkv[:, 256:512].astype(BF16)
        v = qkv[:, 512:768].astype(BF16)
        for h in range(DIFF_HEADS):
            own = lo if h % 2 == 0 else jnp.logical_not(lo)
            vb[slot, h, pl.ds(row0, L), :] = jnp.where(own, v[:, 128 * (h // 2):128 * (h // 2 + 1)],
                                                       jnp.ones((), BF16))

    slot = lax.rem(b, 2)

    @pl.when(first_step)
    def _():
        xpad_a[...] = jnp.zeros(xpad_a.shape, F32)
        xpad_b[...] = jnp.zeros(xpad_b.shape, F32)
        kb[...] = jnp.zeros(kb.shape, BF16)
        vb[...] = jnp.zeros(vb.shape, BF16)
        u0 = normed(hc_ref[0, 0:L, :])
        project_a(u0, xpad_a, stf_a)
        project_b(u0, stf_a, stb_a, 0, 0)

    def chunk_block(c, y_rows, xpad, stf, stb, xpad_other):
        xpad[0:8, :] = jnp.where(c > 0, xpad_other[L:L + 8, :], 0.0)

        pre_a = stf[:, ST_SMALL_A:ST_SMALL_B] + biasa_ref[...]
        log_f = -_softplus(-(stf[:, ST_SMALL_B:] + biasb_ref[...]))
        ssd_lane = lane128 < LANE_ML
        a_neg = jnp.where(ssd_lane[0:1], -jnp.exp(alog_ref[...]), 0.0)
        dt_c = _softplus(pre_a)
        csum = _scan_rows(jnp.where(ssd_lane, dt_c * a_neg, log_f), jnp.add, 0.0)
        last = csum[L - 1:L]
        u_c = pre_a - csum
        m_intra = csum + _scan_rows(u_c, jnp.maximum, -jnp.inf)
        w_end = last - csum + pre_a
        m_loc = jnp.max(w_end, axis=0, keepdims=True)
        e_end = jnp.exp(w_end - m_loc)
        m_prev = mst[0:1]
        inter_log = csum + m_prev
        m_t = jnp.maximum(inter_log, m_intra)
        w_inter = jnp.exp(inter_log - m_t)
        a_col = csum - m_t
        en = jnp.exp(-m_t)
        m_new = jnp.maximum(last + m_prev, m_loc)
        a_prev = jnp.exp(last + m_prev - m_new)
        a_loc = jnp.exp(m_loc - m_new)
        rows_t = jnp.where(ssd_lane, csum, u_c).T

        def bcast(arr, j):
            return jnp.broadcast_to(arr[:, j:j + 1], (arr.shape[0], 128))

        def pair(arr, j0, k):
            return jnp.where(lo[:arr.shape[0]], bcast(arr, j0 + 2 * k), bcast(arr, j0 + 2 * k + 1))

        conv = convb_ref[...]
        for k in range(SSD_CONV):
            conv = conv + convw_ref[k:k + 1, :] * xpad[5 + k:5 + k + L, :]
        xc = conv * _sigmoid(conv)
        xs = xc[:, :SSD_WIDTH]
        bmat = xc[:, SSD_WIDTH:SSD_WIDTH + 256]
        cmat = xc[:, SSD_WIDTH + 256:]
        dfull = dfull_ref[...]
        cs_b = [bcast(csum, LANE_SSD + h) for h in range(SSD_HEADS)]
        xdt, xds, ecs, cdec = [], [], [], []
        for k in range(4):
            cs_p = jnp.where(lo, cs_b[2 * k], cs_b[2 * k + 1])
            last_p = pair(last, LANE_SSD, k)
            xdt.append(xs[:, 128 * k:128 * (k + 1)] * pair(dt_c, LANE_SSD, k))
            xds.append((xdt[k] * jnp.exp(last_p - cs_p)).astype(BF16))
            ecs.append(jnp.exp(cs_p))
            cdec.append(jnp.exp(last_p))

        y_slabs = []
        for g_ in range(SSD_GROUPS):
            bm_g = bmat[:, 128 * g_:128 * (g_ + 1)]
            cm_b = cmat[:, 128 * g_:128 * (g_ + 1)].astype(BF16)
            cb = _dot_nt(cm_b, bm_g.astype(BF16))
            s_loc = _dot_tn(bm_g.astype(BF16), jnp.concatenate(xds[2 * g_:2 * g_ + 2], axis=1))
            prev = prev_t[g_]
            y_off = _dot(cm_b, prev.astype(BF16))
            for kk in range(2):
                k = 2 * g_ + kk
                slab = xdt[k].astype(BF16)
                m2 = []
                for hl in range(2):
                    h = 2 * k + hl
                    diff = cs_b[h] - rows_t[LANE_SSD + h:LANE_SSD + h + 1, :]
                    dec = jnp.exp(jnp.where(tril, diff, -jnp.inf))
                    m2.append((cb * dec).astype(BF16))
                yd = _dot(jnp.concatenate(m2, axis=0), slab)
                y_diag = jnp.where(lo, yd[:L], yd[L:])
                y_slabs.append(y_diag + y_off[:, 128 * kk:128 * (kk + 1)] * ecs[k]
                               + xs[:, 128 * k:128 * (k + 1)] * dfull[:, 128 * k:128 * (k + 1)])
            prev_t[g_] = prev * jnp.concatenate(cdec[2 * g_:2 * g_ + 2], axis=1) + s_loc
        y = jnp.concatenate(y_slabs, axis=1)
        yz = y * stf[:, ST_Z:ST_MO]
        var = jnp.mean(yz * yz, axis=-1, keepdims=True)
        y_ref[0, y_rows, 0:SSD_WIDTH] = ((yz * lax.rsqrt(var + NORM_EPS)) * ssdnw_ref[...]).astype(BF16)

        bd_mask = row_lo == (col_i < 64)
        inv_sqrt_d = MLSTM_HEAD_DIM ** -0.5
        ones_b = jnp.ones((L, 128), BF16)
        mq = stb[:, ST_MQ:ST_MK]
        mk = stb[:, ST_MK:ST_MV]
        mv = stb[:, ST_MV:]
        mo = stf[:, ST_MO:ST_SMALL_A]
        mnw = mnw_ref[...]
        m_slabs = []
        for k in range(2):
            sl = slice(128 * k, 128 * (k + 1))
            q_b = mq[:, sl]
            k_b = mk[:, sl]
            v_b = mv[:, sl]
            v1 = jnp.concatenate([v_b, ones_b], axis=1)
            qm2 = jnp.concatenate([jnp.where(lo, q_b, jnp.zeros((), BF16)),
                                   jnp.where(lo, jnp.zeros((), BF16), q_b)], axis=0)
            qk2 = _dot_nt(qm2, k_b)
            a2 = []
            for hl in range(2):
                h = 2 * k + hl
                arg = bcast(a_col, LANE_ML + h) + rows_t[LANE_ML + h:LANE_ML + h + 1, :]
                w_intra = jnp.exp(jnp.where(tril, arg, -jnp.inf))
                a2.append(((qk2[hl * L:(hl + 1) * L] * inv_sqrt_d) * w_intra).astype(BF16))
            intra = _dot(jnp.concatenate(a2, axis=0), v1)
            num_intra = jnp.where(lo, intra[:L, :128], intra[L:, :128])
            den_intra = jnp.where(lo, intra[:L, 128:], intra[L:, 128:])
            ct_prev = ct[k]
            nm_prev = nm[k]
            wi = pair(w_inter, LANE_ML, k)
            inter = _dot(q_b, jnp.concatenate([ct_prev, nm_prev], axis=1).astype(BF16))
            num = num_intra + wi * inter[:, :128]
            den = den_intra + wi * inter[:, 128:]
            denom = jnp.maximum(jnp.abs(den), pair(en, LANE_ML, k))
            hs = num / denom
            sq = hs * hs
            s_lo = jnp.sum(jnp.where(lo, sq, 0.0), axis=1, keepdims=True)
            s_hi = jnp.sum(jnp.where(lo, 0.0, sq), axis=1, keepdims=True)
            hvar = jnp.where(lo, s_lo, s_hi) * (1.0 / MLSTM_HEAD_DIM)
            hn = (hs * lax.rsqrt(hvar + NORM_EPS)) * mnw[:, sl]
            m_slabs.append(mo[:, sl] * hn)
            e_p = pair(e_end, LANE_ML, k)
            loc = _dot_tn(k_b, jnp.concatenate([(v_b.astype(F32) * e_p).astype(BF16), e_p.astype(BF16)], axis=1)
                          ) * inv_sqrt_d
            ap = pair(a_prev, LANE_ML, k)
            al = pair(a_loc, LANE_ML, k)
            ct[k] = ct_prev * ap + jnp.where(bd_mask, loc[:, :128], 0.0) * al
            nm[k] = nm_prev * ap + jnp.where(bd_mask, loc[:, 128:], 0.0) * al
        mst[...] = jnp.broadcast_to(m_new, mst.shape)
        y_ref[0, y_rows, SSD_WIDTH + DIFF_WIDTH:] = jnp.concatenate(m_slabs, axis=1).astype(BF16)

        q = stb[:, ST_Q:ST_MQ]
        zero_b = jnp.zeros((), BF16)
        for h in range(DIFF_HEADS):
            qs = q[:, 128 * (h // 2):128 * (h // 2 + 1)]
            base = 64 * (h % 2)
            q0 = jnp.where((lane128 >= base) & (lane128 < base + 32), qs, zero_b)
            q1 = jnp.where((lane128 >= base + 32) & (lane128 < base + 64), qs, zero_b)
            qbd[h] = jnp.concatenate([q0, q1], axis=0)

    lam = lam_ref[layer]

    def attention(c, y_rows, xpad_other, stf_other, stb_other, x_next, slot_next, c_next):
        n_groups = c // KEY_GROUP + 1
        gw = KEY_GROUP * L

        def fill_scores(kg, dst):
            r0 = pl.multiple_of(kg * gw, gw)
            for k in range(DIFF_HEADS // 2):
                keys = kb[slot, pl.ds(r0, gw), 128 * k:128 * (k + 1)]
                sc = _dot_nt(qbd[2 * k:2 * k + 2].reshape(4 * L, 128), keys)
                for hl in range(2):
                    h = 2 * k + hl
                    add = jnp.concatenate(
                        [bias[jnp.clip(c - KEY_GROUP * kg - j, -1, 2) + 1, h] for j in range(KEY_GROUP)], axis=1)
                    dst[h] = sc[hl * 2 * L:(hl + 1) * 2 * L] + jnp.concatenate([add, add], axis=0)

        def consume(kg, src):
            r0 = pl.multiple_of(kg * gw, gw)
            for h in range(DIFF_HEADS):
                s = src[h]
                t = s[:, :L]
                for j in range(1, KEY_GROUP):
                    t = jnp.maximum(t, s[:, L * j:L * (j + 1)])
                m_old = rstat[h]
                m_new = jnp.maximum(m_old, jnp.max(t, axis=1, keepdims=True))
                p = jnp.concatenate([jnp.exp2(s[:, L * j:L * (j + 1)] - m_new) for j in range(KEY_GROUP)], axis=1)
                acc_s[h] = acc_s[h] * jnp.exp2(m_old - m_new) + _dot(p.astype(BF16), vb[slot, h, pl.ds(r0, gw), :])
                rstat[h] = m_new

        rstat[...] = jnp.full(rstat.shape, MASKED, F32)
        acc_s[...] = jnp.zeros(acc_s.shape, F32)
        fill_scores(0, s_even)

        def key_step(kg, carry):
            @pl.when(kg % 2 == 0)
            def _():
                fill_scores(kg + 1, s_odd)
                consume(kg, s_even)

            @pl.when(kg % 2 == 1)
            def _():
                fill_scores(kg + 1, s_even)
                consume(kg, s_odd)

            return carry

        g_last = n_groups - 1
        lax.fori_loop(0, g_last, key_step, 0)

        @pl.when(g_last % 2 == 0)
        def _():
            consume(g_last, s_even)

        @pl.when(g_last % 2 == 1)
        def _():
            consume(g_last, s_odd)

        d_slabs = []
        for k in range(2):
            a0, a1 = acc_s[2 * k], acc_s[2 * k + 1]
            lo2 = jnp.concatenate([lo, lo], axis=0)
            r = jnp.where(lo2, a0, a1) * (1.0 / pltpu.roll(jnp.where(lo2, a1, a0), 64, 1))
            o = r[:L] - lam * r[L:]
            sq = o * o
            s_lo = jnp.sum(jnp.where(lo, sq, 0.0), axis=1, keepdims=True)
            s_hi = jnp.sum(jnp.where(lo, 0.0, sq), axis=1, keepdims=True)
            ovar = jnp.where(lo, s_lo, s_hi) * (1.0 / DIFF_V_DIM)
            d_slabs.append(((o * lax.rsqrt(ovar + NORM_EPS)) * dnw_ref[:, 128 * k:128 * (k + 1)])
                           * (1.0 - lam_init))
        y_ref[0, y_rows, SSD_WIDTH:SSD_WIDTH + DIFF_WIDTH] = jnp.concatenate(d_slabs, axis=1).astype(BF16)

        u_next = normed(x_next)
        project_a(u_next, xpad_other, stf_other)
        project_b(u_next, stf_other, stb_other, slot_next, pl.multiple_of(c_next * L, L))

    nb = n_blocks
    sets = ((xpad_a, stf_a, stb_a), (xpad_b, stf_b, stb_b))
    for j in range(CHUNKS_PER_STEP):
        c = CHUNKS_PER_STEP * cp + j
        rows = slice(j * L, (j + 1) * L)
        if j + 1 < CHUNKS_PER_STEP:
            nxt = (hc_ref[0, (j + 1) * L:(j + 2) * L, :], slot, c + 1)
        else:
            nxt = (hn_ref[0], lax.rem(b + (c + 1) // nb, 2), lax.rem(c + 1, nb))
        chunk_block(c, rows, *sets[j % 2], sets[(j + 1) % 2][0])
        attention(c, rows, *sets[(j + 1) % 2], *nxt)


def _mixer(h3d, lam, rel_bias, params, *, layer, lam_init):
    bsz, s, _ = h3d.shape
    L = CHUNK
    nb = s // L

    def par(a):
        return pl.BlockSpec((None,) + a.shape[1:], lambda b, c: (layer, 0, 0))

    smem = pl.BlockSpec(memory_space=pltpu.SMEM)
    cps = CHUNKS_PER_STEP
    cur = pl.BlockSpec((1, cps * L, D_MODEL), lambda b, c: (b, c, 0))
    nxt = pl.BlockSpec((1, L, D_MODEL),
                       lambda b, c: (jnp.minimum(b + (cps * c + cps) // nb, bsz - 1), lax.rem(cps * c + cps, nb), 0))
    w_spec = pl.BlockSpec((None, D_MODEL, PROJ_COLS), lambda b, c: (layer, 0, 0), pipeline_mode=pl.Buffered(1))
    in_specs = [smem, smem, cur, nxt, par(params[0]), w_spec] + [par(p) for p in params[2:]]
    stage = [
        pltpu.VMEM((L + 8, SSD_CONV_CH), F32),
        pltpu.VMEM((L, ST_F_COLS), F32),
        pltpu.VMEM((L, ST_B_COLS), BF16),
    ]
    scratch = [
        pltpu.VMEM((SSD_GROUPS, SSD_STATE, 256), F32),
        pltpu.VMEM((2, 128, 128), F32),
        pltpu.VMEM((2, 128, 128), F32),
        pltpu.VMEM((8, 128), F32),
        pltpu.VMEM((2, s, 256), BF16),
        pltpu.VMEM((2, DIFF_HEADS, s, 128), BF16),
        pltpu.VMEM((4, DIFF_HEADS, L, L), F32),
        pltpu.VMEM((DIFF_HEADS, 2 * L, 128), BF16),
        pltpu.VMEM((DIFF_HEADS, 2 * L, L), F32),
        pltpu.VMEM((DIFF_HEADS, 2 * L, 128), F32),
        pltpu.VMEM((DIFF_HEADS, 2 * L, KEY_GROUP * L), F32),
        pltpu.VMEM((DIFF_HEADS, 2 * L, KEY_GROUP * L), F32),
    ] + stage + stage
    return pl.pallas_call(
        functools.partial(_mixer_kernel, layer=layer, lam_init=lam_init, n_blocks=nb),
        grid=(bsz, nb // cps),
        in_specs=in_specs,
        out_specs=pl.BlockSpec((1, cps * L, MIX_WIDTH), lambda b, c: (b, c, 0)),
        out_shape=jax.ShapeDtypeStruct((bsz, s, MIX_WIDTH), BF16),
        scratch_shapes=scratch,
        compiler_params=pltpu.CompilerParams(dimension_semantics=("arbitrary", "arbitrary"),
                                             vmem_limit_bytes=V7X_VMEM_LIMIT),
        name="mixer",
    )(lam, rel_bias.astype(F32), h3d, h3d, *params)


def _dense_kernel(h_ref, y_ref, p_ref, wo_ref, n2_ref, wg_ref, wu_ref, wd_ref, pg_ref, pp_ref, fn_ref,
                  o_ref, *, final, h_chunk):
    h1 = h_ref[...] + _dot(y_ref[...], wo_ref[...])
    var = jnp.mean(h1 * h1, axis=-1, keepdims=True)
    u = ((h1 * lax.rsqrt(var + NORM_EPS)) * n2_ref[...]).astype(BF16)
    ffn = jnp.zeros(h1.shape, F32)
    for c0 in range(0, FFN_HIDDEN, h_chunk):
        g = _dot(u, wg_ref[:, c0:c0 + h_chunk])
        up = _dot(u, wu_ref[:, c0:c0 + h_chunk])
        a = ((g * _sigmoid(g)) * up).astype(BF16)
        ffn = ffn + _dot(a, wd_ref[c0:c0 + h_chunk, :])
    h2 = h1 + ffn
    gate = _sigmoid(_dot(h2.astype(BF16), pg_ref[...]))
    h3 = h2 + gate * _dot(p_ref[...].astype(BF16), pp_ref[...])
    if final:
        fvar = jnp.mean(h3 * h3, axis=-1, keepdims=True)
        h3 = (h3 * lax.rsqrt(fvar + NORM_EPS)) * fn_ref[...]
    o_ref[...] = h3


def _dense(h2d, y2d, p_all, weights, final_norm_w, *, layer, final, tm):
    t = h2d.shape[0]
    row = lambda width: pl.BlockSpec((tm, width), lambda i: (i, 0))
    res = lambda a: pl.BlockSpec((None,) + a.shape[1:], lambda i: (layer, 0, 0), pipeline_mode=pl.Buffered(1))
    return pl.pallas_call(
        functools.partial(_dense_kernel, final=final, h_chunk=FFN_CHUNK),
        grid=(t // tm,),
        in_specs=[row(D_MODEL), row(MIX_WIDTH), pl.BlockSpec((None, tm, PLE_DIM), lambda i: (layer, i, 0))]
        + [res(w) for w in weights] + [pl.BlockSpec((1, D_MODEL), lambda i: (0, 0))],
        out_specs=row(D_MODEL),
        out_shape=jax.ShapeDtypeStruct((t, D_MODEL), F32),
        compiler_params=pltpu.CompilerParams(dimension_semantics=("arbitrary",),
                                             vmem_limit_bytes=V7X_VMEM_LIMIT),
        name="dense",
    )(h2d, y2d, p_all, *weights, final_norm_w.reshape(1, D_MODEL).astype(F32))


def _pack_w_in(w):
    z, xbc, dt, dq, dk, dv, mq, mk, mv, mo, mi, mf = jnp.split(w, np.cumsum(IN_SPLITS)[:-1], axis=-1)
    zeros = lambda n: jnp.zeros(w.shape[:-1] + (n,), w.dtype)
    small_a = jnp.concatenate([dt, mi, zeros(128 - LANE_ML - MLSTM_HEADS)], axis=-1)
    small_b = jnp.concatenate([zeros(LANE_ML), mf, zeros(128 - LANE_ML - MLSTM_HEADS)], axis=-1)
    return jnp.concatenate([xbc, z, mo, small_a, small_b, dq, dk, dv, mq, mk, mv], axis=-1).astype(BF16)


def _lanes(depth, *placed):
    row = jnp.zeros((depth, 128), F32)
    for off, v in placed:
        row = row.at[:, off:off + v.shape[1]].set(v.astype(F32))
    return row[:, None, :]


def kernel(x, p, norm1_w, w_in, ssd_conv_w, ssd_conv_b, ssd_dt_bias, ssd_a_log, ssd_d, ssd_norm_w, diff_lq1, diff_lk1, diff_lq2, diff_lk2, diff_norm_w, rel_bias, mlstm_i_bias, mlstm_f_bias, mlstm_norm_w, w_out, norm2_w, w_ffn_gate, w_ffn_up, w_ffn_down, ple_gate_w, ple_proj_w, final_norm_w):
    bsz, s, d = x.shape
    depth = w_in.shape[0]
    assert d == D_MODEL and s % (KEY_GROUP * CHUNK) == 0 and s % (CHUNKS_PER_STEP * CHUNK) == 0
    t = bsz * s
    tm = ROW_TILE if s % ROW_TILE == 0 else CHUNK
    row3 = lambda a: a.astype(F32)[:, None, :]

    lam_inits = [0.8 - 0.6 * math.exp(-0.3 * i) for i in range(depth)]
    lam = (jnp.exp(jnp.sum(diff_lq1.astype(F32) * diff_lk1.astype(F32), axis=-1))
           - jnp.exp(jnp.sum(diff_lq2.astype(F32) * diff_lk2.astype(F32), axis=-1))
           + jnp.asarray(lam_inits, F32))
    mixer_params = [
        row3(norm1_w), _pack_w_in(w_in), ssd_conv_w.astype(F32), row3(ssd_conv_b),
        _lanes(depth, (LANE_SSD, ssd_dt_bias), (LANE_ML, mlstm_i_bias)), _lanes(depth, (LANE_ML, mlstm_f_bias)),
        _lanes(depth, (LANE_SSD, ssd_a_log)),
        row3(jnp.repeat(ssd_d, SSD_HEAD_DIM, axis=1)), row3(ssd_norm_w),
        row3(jnp.tile(diff_norm_w, (1, DIFF_HEADS))), row3(mlstm_norm_w),
    ]
    dense_weights = [w_out.astype(BF16), row3(norm2_w), w_ffn_gate.astype(BF16), w_ffn_up.astype(BF16),
                     w_ffn_down.astype(BF16), ple_gate_w.astype(BF16), ple_proj_w.astype(BF16)]
    p_all = p.reshape(depth, t, PLE_DIM).astype(F32)

    h = x.reshape(t, d).astype(F32)
    for i in range(depth):
        y = _mixer(h.reshape(bsz, s, d), lam, rel_bias, mixer_params, layer=i, lam_init=lam_inits[i])
        h = _dense(h, y.reshape(t, MIX_WIDTH), p_all, dense_weights, final_norm_w, layer=i,
                   final=(i == depth - 1), tm=tm)
    return h.reshape(bsz, s, d).astype(x.dtype)
```

```python
import functools
import math

import numpy as np
import jax
import jax.numpy as jnp
from jax import lax
from jax.experimental import pallas as pl
from jax.experimental.pallas import tpu as pltpu

F32 = jnp.float32
BF16 = jnp.bfloat16

D_MODEL = 1024
PLE_DIM = 256
NORM_EPS = 1e-6
CHUNK = 128
KEY_GROUP = 4
CHUNKS_PER_STEP = 4

SSD_HEADS = 8
SSD_HEAD_DIM = 64
SSD_WIDTH = 512
SSD_GROUPS = 2
SSD_STATE = 128
SSD_CONV = 4
SSD_CONV_CH = 1024
DIFF_HEADS = 4
DIFF_QK_DIM = 32
DIFF_V_DIM = 64
DIFF_WIDTH = 256
REL_BUCKETS = 32
REL_MAX_DIST = 128
MLSTM_HEADS = 4
MLSTM_HEAD_DIM = 64
MLSTM_WIDTH = 256
MIX_WIDTH = 1024
FFN_HIDDEN = 2816
IN_SPLITS = (512, 1024, 8, 256, 256, 256, 256, 256, 256, 256, 4, 4)

W_XBC, W_ZMO, W_SMALL, W_QKV = 0, 1024, 1792, 2048
PROJ_COLS = W_QKV + 6 * 256
ST_Z, ST_MO, ST_SMALL_A, ST_SMALL_B, ST_F_COLS = 0, 512, 768, 896, 1024
ST_Q, ST_MQ, ST_MK, ST_MV, ST_B_COLS = 0, 256, 512, 768, 1024
LANE_SSD, LANE_ML = 0, 8
ATTN_C1 = (DIFF_QK_DIM ** -0.5) * math.log2(math.e)

ROW_TILE = 512
FFN_CHUNK = 2816

V7X_VMEM_LIMIT = 56 * 1024 * 1024
MASKED = -1e30
LOG2E = math.log2(math.e)


def _t5_bucket_lower_bounds():
    max_exact = REL_BUCKETS // 2
    d = np.arange(0, 2 * CHUNK, dtype=np.int64)
    df = np.maximum(d, 1).astype(np.float32)
    large = max_exact + (np.log(df / np.float32(max_exact)) / np.float32(math.log(REL_MAX_DIST / max_exact))
                         * np.float32(REL_BUCKETS - max_exact)).astype(np.int32)
    large = np.minimum(large, REL_BUCKETS - 1)
    bucket = np.where(d < max_exact, d, large)
    assert np.all(np.diff(bucket) >= 0) and bucket[CHUNK] == REL_BUCKETS - 1
    return [int(np.argmax(bucket >= b)) for b in range(REL_BUCKETS)]


_BUCKET_LO = _t5_bucket_lower_bounds()


def _sigmoid(x):
    return 0.5 * jnp.tanh(0.5 * x) + 0.5


def _softplus(x):
    return jnp.maximum(x, 0.0) + jnp.log(1.0 + jnp.exp(-jnp.abs(x)))


def _dot(a, b):
    return jnp.dot(a, b, preferred_element_type=F32)


def _dot_nt(a, b):
    return lax.dot_general(a, b, (((1,), (1,)), ((), ())), preferred_element_type=F32)


def _dot_tn(a, b):
    return lax.dot_general(a, b, (((0,), (0,)), ((), ())), preferred_element_type=F32)


def _scan_rows(x, combine, identity):
    sub = lax.broadcasted_iota(jnp.int32, (8, x.shape[1]), 0)
    tiles, carry = [], None
    for i in range(x.shape[0] // 8):
        t = x[8 * i:8 * (i + 1)]
        for sh in (1, 2, 4):
            t = combine(t, jnp.where(sub >= sh, pltpu.roll(t, sh, 0), identity))
        if carry is not None:
            t = combine(t, carry)
        carry = t[7:8]
        tiles.append(t)
    return jnp.concatenate(tiles, axis=0)


def _mixer_kernel(lam_ref, relb_ref, hc_ref, hn_ref, n1_ref, w_ref, convw_ref, convb_ref,
                  biasa_ref, biasb_ref, alog_ref, dfull_ref, ssdnw_ref, dnw_ref, mnw_ref,
                  y_ref,
                  prev_t, ct, nm, mst, kb, vb, bias, qbd, rstat, acc_s, s_even, s_odd,
                  xpad_a, stf_a, stb_a, xpad_b, stf_b, stb_b,
                  *, layer, lam_init, n_blocks):
    L = CHUNK
    b = pl.program_id(0)
    cp = pl.program_id(1)

    lane128 = lax.broadcasted_iota(jnp.int32, (L, 128), 1)
    lo = lane128 < 64
    row_i = lax.broadcasted_iota(jnp.int32, (L, L), 0)
    col_i = lax.broadcasted_iota(jnp.int32, (L, L), 1)
    tril = row_i >= col_i
    row_lo = row_i < 64

    first_step = jnp.logical_and(b == 0, cp == 0)

    @pl.when(first_step)
    def _():
        for h in range(DIFF_HEADS):
            bias[0, h] = jnp.full((L, L), MASKED, F32)
            bias[3, h] = jnp.zeros((L, L), F32)
            far = relb_ref[REL_BUCKETS - 1, h]
            for which, off in ((1, 0), (2, L)):
                dist = row_i - col_i + off
                v = jnp.full((L, L), far, F32)
                for bkt in range(REL_BUCKETS - 2, -1, -1):
                    v = jnp.where(dist < _BUCKET_LO[bkt + 1], relb_ref[bkt, h], v)
                bias[which, h] = jnp.where(dist >= 0, (v - far) * LOG2E, MASKED)

    @pl.when(cp == 0)
    def _():
        prev_t[...] = jnp.zeros(prev_t.shape, F32)
        ct[...] = jnp.zeros(ct.shape, F32)
        nm[...] = jnp.zeros(nm.shape, F32)
        mst[...] = jnp.zeros(mst.shape, F32)

    def normed(x):
        var = jnp.mean(x * x, axis=-1, keepdims=True)
        return ((x * lax.rsqrt(var + NORM_EPS)) * n1_ref[...]).astype(BF16)

    def project_a(u, xpad, stf):
        xpad[8:8 + L, :] = _dot(u, w_ref[:, W_XBC:W_ZMO])
        zmo = _dot(u, w_ref[:, W_ZMO:W_SMALL])
        z = zmo[:, :SSD_WIDTH]
        stf[:, ST_Z:ST_MO] = z * _sigmoid(z)
        stf[:, ST_MO:ST_SMALL_A] = _sigmoid(zmo[:, SSD_WIDTH:])

    def project_b(u, stf, stb, slot, row0):
        stf[:, ST_SMALL_A:] = _dot(u, w_ref[:, W_SMALL:W_QKV])
        qkv = _dot(u, w_ref[:, W_QKV:])
        stb[:, ST_Q:ST_MQ] = (qkv[:, 0:256] * ATTN_C1).astype(BF16)
        stb[:, ST_MQ:] = qkv[:, 768:].astype(BF16)
        kb[slot, pl.ds(row0, L), :] = qkv[:, 256:512].astype(BF16)
        v = qkv[:, 512:768].astype(BF16)
        for h in range(DIFF_HEADS):
            own = lo if h % 2 == 0 else jnp.logical_not(lo)
            vb[slot, h, pl.ds(row0, L), :] = jnp.where(own, v[:, 128 * (h // 2):128 * (h // 2 + 1)],
                                                       jnp.ones((), BF16))

    slot = lax.rem(b, 2)

    @pl.when(first_step)
    def _():
        xpad_a[...] = jnp.zeros(xpad_a.shape, F32)
        xpad_b[...] = jnp.zeros(xpad_b.shape, F32)
        kb[...] = jnp.zeros(kb.shape, BF16)
        vb[...] = jnp.zeros(vb.shape, BF16)
        u0 = normed(hc_ref[0, 0:L, :])
        project_a(u0, xpad_a, stf_a)
        project_b(u0, stf_a, stb_a, 0, 0)

    def chunk_block(c, y_rows, xpad, stf, stb, xpad_other):
        xpad[0:8, :] = jnp.where(c > 0, xpad_other[L:L + 8, :], 0.0)

        pre_a = stf[:, ST_SMALL_A:ST_SMALL_B] + biasa_ref[...]
        log_f = -_softplus(-(stf[:, ST_SMALL_B:] + biasb_ref[...]))
        ssd_lane = lane128 < LANE_ML
        a_neg = jnp.where(ssd_lane[0:1], -jnp.exp(alog_ref[...]), 0.0)
        dt_c = _softplus(pre_a)
        csum = _scan_rows(jnp.where(ssd_lane, dt_c * a_neg, log_f), jnp.add, 0.0)
        last = csum[L - 1:L]
        u_c = pre_a - csum
        m_intra = csum + _scan_rows(u_c, jnp.maximum, -jnp.inf)
        w_end = last - csum + pre_a
        m_loc = jnp.max(w_end, axis=0, keepdims=True)
        e_end = jnp.exp(w_end - m_loc)
        m_prev = mst[0:1]
        inter_log = csum + m_prev
        m_t = jnp.maximum(inter_log, m_intra)
        w_inter = jnp.exp(inter_log - m_t)
        a_col = csum - m_t
        en = jnp.exp(-m_t)
        m_new = jnp.maximum(last + m_prev, m_loc)
        a_prev = jnp.exp(last + m_prev - m_new)
        a_loc = jnp.exp(m_loc - m_new)
        rows_t = jnp.where(ssd_lane, csum, u_c).T

        def bcast(arr, j):
            return jnp.broadcast_to(arr[:, j:j + 1], (arr.shape[0], 128))

        def pair(arr, j0, k):
            return jnp.where(lo[:arr.shape[0]], bcast(arr, j0 + 2 * k), bcast(arr, j0 + 2 * k + 1))

        x_in = xpad[8:8 + L, :]
        tail = xpad[0:8, :]
        sub8 = lax.broadcasted_iota(jnp.int32, (8, SSD_CONV_CH), 0)
        conv = convb_ref[...] + convw_ref[SSD_CONV - 1:SSD_CONV, :] * x_in
        for sh in range(1, SSD_CONV):
            r = pltpu.roll(x_in, sh, 0)
            top = jnp.where(sub8 < sh, pltpu.roll(tail, sh, 0), r[0:8])
            shifted = jnp.concatenate([top, r[8:]], axis=0)
            conv = conv + convw_ref[SSD_CONV - 1 - sh:SSD_CONV - sh, :] * shifted
        xc = conv * _sigmoid(conv)
        xs = xc[:, :SSD_WIDTH]
        bmat = xc[:, SSD_WIDTH:SSD_WIDTH + 256]
        cmat = xc[:, SSD_WIDTH + 256:]
        dfull = dfull_ref[...]
        cs_b = [bcast(csum, LANE_SSD + h) for h in range(SSD_HEADS)]
        xdt, xds, ecs, cdec = [], [], [], []
        for k in range(4):
            cs_p = jnp.where(lo, cs_b[2 * k], cs_b[2 * k + 1])
            last_p = pair(last, LANE_SSD, k)
            xdt.append(xs[:, 128 * k:128 * (k + 1)] * pair(dt_c, LANE_SSD, k))
            xds.append((xdt[k] * jnp.exp(last_p - cs_p)).astype(BF16))
            ecs.append(jnp.exp(cs_p))
            cdec.append(jnp.exp(last_p))

        y_slabs = []
        for g_ in range(SSD_GROUPS):
            bm_g = bmat[:, 128 * g_:128 * (g_ + 1)]
            cm_b = cmat[:, 128 * g_:128 * (g_ + 1)].astype(BF16)
            cb = _dot_nt(cm_b, bm_g.astype(BF16))
            s_loc = _dot_tn(bm_g.astype(BF16), jnp.concatenate(xds[2 * g_:2 * g_ + 2], axis=1))
            prev = prev_t[g_]
            y_off = _dot(cm_b, prev.astype(BF16))
            for kk in range(2):
                k = 2 * g_ + kk
                slab = xdt[k].astype(BF16)
                m2 = []
                for hl in range(2):
                    h = 2 * k + hl
                    diff = cs_b[h] - rows_t[LANE_SSD + h:LANE_SSD + h + 1, :]
                    dec = jnp.exp(jnp.where(tril, diff, -jnp.inf))
                    m2.append((cb * dec).astype(BF16))
                yd = _dot(jnp.concatenate(m2, axis=0), slab)
                y_diag = jnp.where(lo, yd[:L], yd[L:])
                y_slabs.append(y_diag + y_off[:, 128 * kk:128 * (kk + 1)] * ecs[k]
                               + xs[:, 128 * k:128 * (k + 1)] * dfull[:, 128 * k:128 * (k + 1)])
            prev_t[g_] = prev * jnp.concatenate(cdec[2 * g_:2 * g_ + 2], axis=1) + s_loc
        y = jnp.concatenate(y_slabs, axis=1)
        yz = y * stf[:, ST_Z:ST_MO]
        var = jnp.mean(yz * yz, axis=-1, keepdims=True)
        y_ref[0, y_rows, 0:SSD_WIDTH] = ((yz * lax.rsqrt(var + NORM_EPS)) * ssdnw_ref[...]).astype(BF16)

        bd_mask = row_lo == (col_i < 64)
        inv_sqrt_d = MLSTM_HEAD_DIM ** -0.5
        ones_b = jnp.ones((L, 128), BF16)
        mq = stb[:, ST_MQ:ST_MK]
        mk = stb[:, ST_MK:ST_MV]
        mv = stb[:, ST_MV:]
        mo = stf[:, ST_MO:ST_SMALL_A]
        mnw = mnw_ref[...]
        m_slabs = []
        for k in range(2):
            sl = slice(128 * k, 128 * (k + 1))
            q_b = mq[:, sl]
            k_b = mk[:, sl]
            v_b = mv[:, sl]
            v1 = jnp.concatenate([v_b, ones_b], axis=1)
            qm2 = jnp.concatenate([jnp.where(lo, q_b, jnp.zeros((), BF16)),
                                   jnp.where(lo, jnp.zeros((), BF16), q_b)], axis=0)
            qk2 = _dot_nt(qm2, k_b)
            a2 = []
            for hl in range(2):
                h = 2 * k + hl
                arg = bcast(a_col, LANE_ML + h) + rows_t[LANE_ML + h:LANE_ML + h + 1, :]
                w_intra = jnp.exp(jnp.where(tril, arg, -jnp.inf))
                a2.append(((qk2[hl * L:(hl + 1) * L] * inv_sqrt_d) * w_intra).astype(BF16))
            intra = _dot(jnp.concatenate(a2, axis=0), v1)
            num_intra = jnp.where(lo, intra[:L, :128], intra[L:, :128])
            den_intra = jnp.where(lo, intra[:L, 128:], intra[L:, 128:])
            ct_prev = ct[k]
            nm_prev = nm[k]
            wi = pair(w_inter, LANE_ML, k)
            inter = _dot(q_b, jnp.concatenate([ct_prev, nm_prev], axis=1).astype(BF16))
            num = num_intra + wi * inter[:, :128]
            den = den_intra + wi * inter[:, 128:]
            denom = jnp.maximum(jnp.abs(den), pair(en, LANE_ML, k))
            hs = num / denom
            sq = hs * hs
            s_lo = jnp.sum(jnp.where(lo, sq, 0.0), axis=1, keepdims=True)
            s_hi = jnp.sum(jnp.where(lo, 0.0, sq), axis=1, keepdims=True)
            hvar = jnp.where(lo, s_lo, s_hi) * (1.0 / MLSTM_HEAD_DIM)
            hn = (hs * lax.rsqrt(hvar + NORM_EPS)) * mnw[:, sl]
            m_slabs.append(mo[:, sl] * hn)
            e_p = pair(e_end, LANE_ML, k)
            loc = _dot_tn(k_b, jnp.concatenate([(v_b.astype(F32) * e_p).astype(BF16), e_p.astype(BF16)], axis=1)
                          ) * inv_sqrt_d
            ap = pair(a_prev, LANE_ML, k)
            al = pair(a_loc, LANE_ML, k)
            ct[k] = ct_prev * ap + jnp.where(bd_mask, loc[:, :128], 0.0) * al
            nm[k] = nm_prev * ap + jnp.where(bd_mask, loc[:, 128:], 0.0) * al
        mst[...] = jnp.broadcast_to(m_new, mst.shape)
        y_ref[0, y_rows, SSD_WIDTH + DIFF_WIDTH:] = jnp.concatenate(m_slabs, axis=1).astype(BF16)

        q = stb[:, ST_Q:ST_MQ]
        zero_b = jnp.zeros((), BF16)
        for h in range(DIFF_HEADS):
            qs = q[:, 128 * (h // 2):128 * (h // 2 + 1)]
            base = 64 * (h % 2)
            q0 = jnp.where((lane128 >= base) & (lane128 < base + 32), qs, zero_b)
            q1 = jnp.where((lane128 >= base + 32) & (lane128 < base + 64), qs, zero_b)
            qbd[h] = jnp.concatenate([q0, q1], axis=0)

    lam = lam_ref[layer]

    def attention(c, y_rows, xpad_other, stf_other, stb_other, x_next, slot_next, c_next):
        n_groups = c // KEY_GROUP + 1
        gw = KEY_GROUP * L

        def fill_scores(kg, dst):
            r0 = pl.multiple_of(kg * gw, gw)
            for k in range(DIFF_HEADS // 2):
                keys = kb[slot, pl.ds(r0, gw), 128 * k:128 * (k + 1)]
                sc = _dot_nt(qbd[2 * k:2 * k + 2].reshape(4 * L, 128), keys)
                for hl in range(2):
                    h = 2 * k + hl
                    add = jnp.concatenate(
                        [bias[jnp.clip(c - KEY_GROUP * kg - j, -1, 2) + 1, h] for j in range(KEY_GROUP)], axis=1)
                    dst[h] = sc[hl * 2 * L:(hl + 1) * 2 * L] + jnp.concatenate([add, add], axis=0)

        def consume(kg, src):
            r0 = pl.multiple_of(kg * gw, gw)
            for h in range(DIFF_HEADS):
                s = src[h]
                t = s[:, :L]
                for j in range(1, KEY_GROUP):
                    t = jnp.maximum(t, s[:, L * j:L * (j + 1)])
                m_old = rstat[h]
                m_new = jnp.maximum(m_old, jnp.max(t, axis=1, keepdims=True))
                p = jnp.concatenate([jnp.exp2(s[:, L * j:L * (j + 1)] - m_new) for j in range(KEY_GROUP)], axis=1)
                acc_s[h] = acc_s[h] * jnp.exp2(m_old - m_new) + _dot(p.astype(BF16), vb[slot, h, pl.ds(r0, gw), :])
                rstat[h] = m_new

        rstat[...] = jnp.full(rstat.shape, MASKED, F32)
        acc_s[...] = jnp.zeros(acc_s.shape, F32)
        fill_scores(0, s_even)

        def key_step(kg, carry):
            @pl.when(kg % 2 == 0)
            def _():
                fill_scores(kg + 1, s_odd)
                consume(kg, s_even)

            @pl.when(kg % 2 == 1)
            def _():
                fill_scores(kg + 1, s_even)
                consume(kg, s_odd)

            return carry

        g_last = n_groups - 1
        lax.fori_loop(0, g_last, key_step, 0)

        @pl.when(g_last % 2 == 0)
        def _():
            consume(g_last, s_even)

        @pl.when(g_last % 2 == 1)
        def _():
            consume(g_last, s_odd)

        d_slabs = []
        for k in range(2):
            a0, a1 = acc_s[2 * k], acc_s[2 * k + 1]
            lo2 = jnp.concatenate([lo, lo], axis=0)
            r = jnp.where(lo2, a0, a1) * (1.0 / pltpu.roll(jnp.where(lo2, a1, a0), 64, 1))
            o = r[:L] - lam * r[L:]
            sq = o * o
            s_lo = jnp.sum(jnp.where(lo, sq, 0.0), axis=1, keepdims=True)
            s_hi = jnp.sum(jnp.where(lo, 0.0, sq), axis=1, keepdims=True)
            ovar = jnp.where(lo, s_lo, s_hi) * (1.0 / DIFF_V_DIM)
            d_slabs.append(((o * lax.rsqrt(ovar + NORM_EPS)) * dnw_ref[:, 128 * k:128 * (k + 1)])
                           * (1.0 - lam_init))
        y_ref[0, y_rows, SSD_WIDTH:SSD_WIDTH + DIFF_WIDTH] = jnp.concatenate(d_slabs, axis=1).astype(BF16)

        u_next = normed(x_next)
        project_a(u_next, xpad_other, stf_other)
        project_b(u_next, stf_other, stb_other, slot_next, pl.multiple_of(c_next * L, L))

    nb = n_blocks
    sets = ((xpad_a, stf_a, stb_a), (xpad_b, stf_b, stb_b))
    for j in range(CHUNKS_PER_STEP):
        c = CHUNKS_PER_STEP * cp + j
        rows = slice(j * L, (j + 1) * L)
        if j + 1 < CHUNKS_PER_STEP:
            nxt = (hc_ref[0, (j + 1) * L:(j + 2) * L, :], slot, c + 1)
        else:
            nxt = (hn_ref[0], lax.rem(b + (c + 1) // nb, 2), lax.rem(c + 1, nb))
        chunk_block(c, rows, *sets[j % 2], sets[(j + 1) % 2][0])
        attention(c, rows, *sets[(j + 1) % 2], *nxt)


def _mixer(h3d, lam, rel_bias, params, *, layer, lam_init):
    bsz, s, _ = h3d.shape
    L = CHUNK
    nb = s // L

    def par(a):
        return pl.BlockSpec((None,) + a.shape[1:], lambda b, c: (layer, 0, 0))

    smem = pl.BlockSpec(memory_space=pltpu.SMEM)
    cps = CHUNKS_PER_STEP
    cur = pl.BlockSpec((1, cps * L, D_MODEL), lambda b, c: (b, c, 0))
    nxt = pl.BlockSpec((1, L, D_MODEL),
                       lambda b, c: (jnp.minimum(b + (cps * c + cps) // nb, bsz - 1), lax.rem(cps * c + cps, nb), 0))
    w_spec = pl.BlockSpec((None, D_MODEL, PROJ_COLS), lambda b, c: (layer, 0, 0), pipeline_mode=pl.Buffered(1))
    in_specs = [smem, smem, cur, nxt, par(params[0]), w_spec] + [par(p) for p in params[2:]]
    stage = [
        pltpu.VMEM((L + 8, SSD_CONV_CH), F32),
        pltpu.VMEM((L, ST_F_COLS), F32),
        pltpu.VMEM((L, ST_B_COLS), BF16),
    ]
    scratch = [
        pltpu.VMEM((SSD_GROUPS, SSD_STATE, 256), F32),
        pltpu.VMEM((2, 128, 128), F32),
        pltpu.VMEM((2, 128, 128), F32),
        pltpu.VMEM((8, 128), F32),
        pltpu.VMEM((2, s, 256), BF16),
        pltpu.VMEM((2, DIFF_HEADS, s, 128), BF16),
        pltpu.VMEM((4, DIFF_HEADS, L, L), F32),
        pltpu.VMEM((DIFF_HEADS, 2 * L, 128), BF16),
        pltpu.VMEM((DIFF_HEADS, 2 * L, L), F32),
        pltpu.VMEM((DIFF_HEADS, 2 * L, 128), F32),
        pltpu.VMEM((DIFF_HEADS, 2 * L, KEY_GROUP * L), F32),
        pltpu.VMEM((DIFF_HEADS, 2 * L, KEY_GROUP * L), F32),
    ] + stage + stage
    return pl.pallas_call(
        functools.partial(_mixer_kernel, layer=layer, lam_init=lam_init, n_blocks=nb),
        grid=(bsz, nb // cps),
        in_specs=in_specs,
        out_specs=pl.BlockSpec((1, cps * L, MIX_WIDTH), lambda b, c: (b, c, 0)),
        out_shape=jax.ShapeDtypeStruct((bsz, s, MIX_WIDTH), BF16),
        scratch_shapes=scratch,
        compiler_params=pltpu.CompilerParams(dimension_semantics=("arbitrary", "arbitrary"),
                                             vmem_limit_bytes=V7X_VMEM_LIMIT),
        name="mixer",
    )(lam, rel_bias.astype(F32), h3d, h3d, *params)


def _dense_kernel(h_ref, y_ref, p_ref, wo_ref, n2_ref, wg_ref, wu_ref, wd_ref, pg_ref, pp_ref, fn_ref,
                  o_ref, *, final, h_chunk):
    h1 = h_ref[...] + _dot(y_ref[...], wo_ref[...])
    var = jnp.mean(h1 * h1, axis=-1, keepdims=True)
    u = ((h1 * lax.rsqrt(var + NORM_EPS)) * n2_ref[...]).astype(BF16)
    ffn = jnp.zeros(h1.shape, F32)
    for c0 in range(0, FFN_HIDDEN, h_chunk):
        g = _dot(u, wg_ref[:, c0:c0 + h_chunk])
        up = _dot(u, wu_ref[:, c0:c0 + h_chunk])
        a = ((g * _sigmoid(g)) * up).astype(BF16)
        ffn = ffn + _dot(a, wd_ref[c0:c0 + h_chunk, :])
    h2 = h1 + ffn
    gate = _sigmoid(_dot(h2.astype(BF16), pg_ref[...]))
    h3 = h2 + gate * _dot(p_ref[...].astype(BF16), pp_ref[...])
    if final:
        fvar = jnp.mean(h3 * h3, axis=-1, keepdims=True)
        h3 = (h3 * lax.rsqrt(fvar + NORM_EPS)) * fn_ref[...]
    o_ref[...] = h3


def _dense(h2d, y2d, p_all, weights, final_norm_w, *, layer, final, tm):
    t = h2d.shape[0]
    row = lambda width: pl.BlockSpec((tm, width), lambda i: (i, 0))
    res = lambda a: pl.BlockSpec((None,) + a.shape[1:], lambda i: (layer, 0, 0), pipeline_mode=pl.Buffered(1))
    return pl.pallas_call(
        functools.partial(_dense_kernel, final=final, h_chunk=FFN_CHUNK),
        grid=(t // tm,),
        in_specs=[row(D_MODEL), row(MIX_WIDTH), pl.BlockSpec((None, tm, PLE_DIM), lambda i: (layer, i, 0))]
        + [res(w) for w in weights] + [pl.BlockSpec((1, D_MODEL), lambda i: (0, 0))],
        out_specs=row(D_MODEL),
        out_shape=jax.ShapeDtypeStruct((t, D_MODEL), F32),
        compiler_params=pltpu.CompilerParams(dimension_semantics=("arbitrary",),
                                             vmem_limit_bytes=V7X_VMEM_LIMIT),
        name="dense",
    )(h2d, y2d, p_all, *weights, final_norm_w.reshape(1, D_MODEL).astype(F32))


def _pack_w_in(w):
    z, xbc, dt, dq, dk, dv, mq, mk, mv, mo, mi, mf = jnp.split(w, np.cumsum(IN_SPLITS)[:-1], axis=-1)
    zeros = lambda n: jnp.zeros(w.shape[:-1] + (n,), w.dtype)
    small_a = jnp.concatenate([dt, mi, zeros(128 - LANE_ML - MLSTM_HEADS)], axis=-1)
    small_b = jnp.concatenate([zeros(LANE_ML), mf, zeros(128 - LANE_ML - MLSTM_HEADS)], axis=-1)
    return jnp.concatenate([xbc, z, mo, small_a, small_b, dq, dk, dv, mq, mk, mv], axis=-1).astype(BF16)


def _lanes(depth, *placed):
    row = jnp.zeros((depth, 128), F32)
    for off, v in placed:
        row = row.at[:, off:off + v.shape[1]].set(v.astype(F32))
    return row[:, None, :]


def kernel(x, p, norm1_w, w_in, ssd_conv_w, ssd_conv_b, ssd_dt_bias, ssd_a_log, ssd_d, ssd_norm_w, diff_lq1, diff_lk1, diff_lq2, diff_lk2, diff_norm_w, rel_bias, mlstm_i_bias, mlstm_f_bias, mlstm_norm_w, w_out, norm2_w, w_ffn_gate, w_ffn_up, w_ffn_down, ple_gate_w, ple_proj_w, final_norm_w):
    bsz, s, d = x.shape
    depth = w_in.shape[0]
    assert d == D_MODEL and s % (KEY_GROUP * CHUNK) == 0 and s % (CHUNKS_PER_STEP * CHUNK) == 0
    t = bsz * s
    tm = ROW_TILE if s % ROW_TILE == 0 else CHUNK
    row3 = lambda a: a.astype(F32)[:, None, :]

    lam_inits = [0.8 - 0.6 * math.exp(-0.3 * i) for i in range(depth)]
    lam = (jnp.exp(jnp.sum(diff_lq1.astype(F32) * diff_lk1.astype(F32), axis=-1))
           - jnp.exp(jnp.sum(diff_lq2.astype(F32) * diff_lk2.astype(F32), axis=-1))
           + jnp.asarray(lam_inits, F32))
    mixer_params = [
        row3(norm1_w), _pack_w_in(w_in), ssd_conv_w.astype(F32), row3(ssd_conv_b),
        _lanes(depth, (LANE_SSD, ssd_dt_bias), (LANE_ML, mlstm_i_bias)), _lanes(depth, (LANE_ML, mlstm_f_bias)),
        _lanes(depth, (LANE_SSD, ssd_a_log)),
        row3(jnp.repeat(ssd_d, SSD_HEAD_DIM, axis=1)), row3(ssd_norm_w),
        row3(jnp.tile(diff_norm_w, (1, DIFF_HEADS))), row3(mlstm_norm_w),
    ]
    dense_weights = [w_out.astype(BF16), row3(norm2_w), w_ffn_gate.astype(BF16), w_ffn_up.astype(BF16),
                     w_ffn_down.astype(BF16), ple_gate_w.astype(BF16), ple_proj_w.astype(BF16)]
    p_all = p.reshape(depth, t, PLE_DIM).astype(F32)

    h = x.reshape(t, d).astype(F32)
    for i in range(depth):
        y = _mixer(h.reshape(bsz, s, d), lam, rel_bias, mixer_params, layer=i, lam_init=lam_inits[i])
        h = _dense(h, y.reshape(t, MIX_WIDTH), p_all, dense_weights, final_norm_w, layer=i,
                   final=(i == depth - 1), tm=tm)
    return h.reshape(bsz, s, d).astype(x.dtype)
```

```python
import functools
import math

import numpy as np
import jax
import jax.numpy as jnp
from jax import lax
from jax.experimental import pallas as pl
from jax.experimental.pallas import tpu as pltpu

F32 = jnp.float32
BF16 = jnp.bfloat16

D_MODEL = 1024
PLE_DIM = 256
NORM_EPS = 1e-6
CHUNK = 128
KEY_GROUP = 4
CHUNKS_PER_STEP = 4

SSD_HEADS = 8
SSD_HEAD_DIM = 64
SSD_WIDTH = 512
SSD_GROUPS = 2
SSD_STATE = 128
SSD_CONV = 4
SSD_CONV_CH = 1024
DIFF_HEADS = 4
DIFF_QK_DIM = 32
DIFF_V_DIM = 64
DIFF_WIDTH = 256
REL_BUCKETS = 32
REL_MAX_DIST = 128
MLSTM_HEADS = 4
MLSTM_HEAD_DIM = 64
MLSTM_WIDTH = 256
MIX_WIDTH = 1024
FFN_HIDDEN = 2816
IN_SPLITS = (512, 1024, 8, 256, 256, 256, 256, 256, 256, 256, 4, 4)

W_XBC, W_ZMO, W_SMALL, W_QKV = 0, 1024, 1792, 2048
PROJ_COLS = W_QKV + 6 * 256
ST_Z, ST_MO, ST_SMALL_A, ST_SMALL_B, ST_F_COLS = 0, 512, 768, 896, 1024
ST_Q, ST_MQ, ST_MK, ST_MV, ST_B_COLS = 0, 256, 512, 768, 1024
LANE_SSD, LANE_ML = 0, 8
ATTN_C1 = (DIFF_QK_DIM ** -0.5) * math.log2(math.e)

ROW_TILE = 512
FFN_CHUNK = 2816

V7X_VMEM_LIMIT = 56 * 1024 * 1024
MASKED = -1e30
LOG2E = math.log2(math.e)


def _t5_bucket_lower_bounds():
    max_exact = REL_BUCKETS // 2
    d = np.arange(0, 2 * CHUNK, dtype=np.int64)
    df = np.maximum(d, 1).astype(np.float32)
    large = max_exact + (np.log(df / np.float32(max_exact)) / np.float32(math.log(REL_MAX_DIST / max_exact))
                         * np.float32(REL_BUCKETS - max_exact)).astype(np.int32)
    large = np.minimum(large, REL_BUCKETS - 1)
    bucket = np.where(d < max_exact, d, large)
    assert np.all(np.diff(bucket) >= 0) and bucket[CHUNK] == REL_BUCKETS - 1
    return [int(np.argmax(bucket >= b)) for b in range(REL_BUCKETS)]


_BUCKET_LO = _t5_bucket_lower_bounds()


def _sigmoid(x):
    return 0.5 * jnp.tanh(0.5 * x) + 0.5


def _silu(x):
    hx = 0.5 * x
    return hx + hx * jnp.tanh(hx)


def _softplus(x):
    return jnp.maximum(x, 0.0) + jnp.log(1.0 + jnp.exp(-jnp.abs(x)))


def _dot(a, b):
    return jnp.dot(a, b, preferred_element_type=F32)


def _dot_nt(a, b):
    return lax.dot_general(a, b, (((1,), (1,)), ((), ())), preferred_element_type=F32)


def _dot_tn(a, b):
    return lax.dot_general(a, b, (((0,), (0,)), ((), ())), preferred_element_type=F32)


def _scan_rows(x, combine, identity):
    sub = lax.broadcasted_iota(jnp.int32, (8, x.shape[1]), 0)
    tiles, carry = [], None
    for i in range(x.shape[0] // 8):
        t = x[8 * i:8 * (i + 1)]
        for sh in (1, 2, 4):
            t = combine(t, jnp.where(sub >= sh, pltpu.roll(t, sh, 0), identity))
        if carry is not None:
            t = combine(t, carry)
        carry = t[7:8]
        tiles.append(t)
    return jnp.concatenate(tiles, axis=0)


def _mixer_kernel(lam_ref, relb_ref, hc_ref, hn_ref, n1_ref, w_ref, convw_ref, convb_ref,
                  biasa_ref, biasb_ref, alog_ref, dfull_ref, ssdnw_ref, dnw_ref, mnw_ref,
                  y_ref,
                  prev_t, ct, nm, mst, kb, vb, bias, qbd, rstat, acc_s, s_even, s_odd,
                  xpad_a, stf_a, stb_a, xpad_b, stf_b, stb_b,
                  *, layer, lam_init, n_blocks):
    L = CHUNK
    b = pl.program_id(0)
    cp = pl.program_id(1)

    lane128 = lax.broadcasted_iota(jnp.int32, (L, 128), 1)
    lo = lane128 < 64
    row_i = lax.broadcasted_iota(jnp.int32, (L, L), 0)
    col_i = lax.broadcasted_iota(jnp.int32, (L, L), 1)
    tril = row_i >= col_i
    row_lo = row_i < 64

    first_step = jnp.logical_and(b == 0, cp == 0)

    @pl.when(first_step)
    def _():
        for h in range(DIFF_HEADS):
            bias[0, h] = jnp.full((L, L), MASKED, F32)
            bias[3, h] = jnp.zeros((L, L), F32)
            far = relb_ref[REL_BUCKETS - 1, h]
            for which, off in ((1, 0), (2, L)):
                dist = row_i - col_i + off
                v = jnp.full((L, L), far, F32)
                for bkt in range(REL_BUCKETS - 2, -1, -1):
                    v = jnp.where(dist < _BUCKET_LO[bkt + 1], relb_ref[bkt, h], v)
                bias[which, h] = jnp.where(dist >= 0, (v - far) * LOG2E, MASKED)

    @pl.when(cp == 0)
    def _():
        prev_t[...] = jnp.zeros(prev_t.shape, F32)
        ct[...] = jnp.zeros(ct.shape, F32)
        nm[...] = jnp.zeros(nm.shape, F32)
        mst[...] = jnp.zeros(mst.shape, F32)

    def normed(x):
        var = jnp.mean(x * x, axis=-1, keepdims=True)
        return ((x * lax.rsqrt(var + NORM_EPS)) * n1_ref[...]).astype(BF16)

    def project_a(u, xpad, stf):
        xpad[8:8 + L, :] = _dot(u, w_ref[:, W_XBC:W_ZMO])
        zmo = _dot(u, w_ref[:, W_ZMO:W_SMALL])
        z = zmo[:, :SSD_WIDTH]
        stf[:, ST_Z:ST_MO] = _silu(z)
        stf[:, ST_MO:ST_SMALL_A] = _sigmoid(zmo[:, SSD_WIDTH:])

    def project_b(u, stf, stb, slot, row0):
        stf[:, ST_SMALL_A:] = _dot(u, w_ref[:, W_SMALL:W_QKV])
        qkv = _dot(u, w_ref[:, W_QKV:])
        stb[:, ST_Q:ST_MQ] = (qkv[:, 0:256] * ATTN_C1).astype(BF16)
        stb[:, ST_MQ:] = qkv[:, 768:].astype(BF16)
        kb[slot, pl.ds(row0, L), :] = qkv[:, 256:512].astype(BF16)
        v = qkv[:, 512:768].astype(BF16)
        for h in range(DIFF_HEADS):
            own = lo if h % 2 == 0 else jnp.logical_not(lo)
            vb[slot, h, pl.ds(row0, L), :] = jnp.where(own, v[:, 128 * (h // 2):128 * (h // 2 + 1)],
                                                       jnp.ones((), BF16))

    slot = lax.rem(b, 2)

    @pl.when(first_step)
    def _():
        xpad_a[...] = jnp.zeros(xpad_a.shape, F32)
        xpad_b[...] = jnp.zeros(xpad_b.shape, F32)
        kb[...] = jnp.zeros(kb.shape, BF16)
        vb[...] = jnp.zeros(vb.shape, BF16)
        u0 = normed(hc_ref[0, 0:L, :])
        project_a(u0, xpad_a, stf_a)
        project_b(u0, stf_a, stb_a, 0, 0)

    def chunk_block(c, y_rows, xpad, stf, stb, xpad_other):
        xpad[0:8, :] = jnp.where(c > 0, xpad_other[L:L + 8, :], 0.0)

        pre_a = stf[:, ST_SMALL_A:ST_SMALL_B] + biasa_ref[...]
        log_f = -_softplus(-(stf[:, ST_SMALL_B:] + biasb_ref[...]))
        ssd_lane = lane128 < LANE_ML
        a_neg = jnp.where(ssd_lane[0:1], -jnp.exp(alog_ref[...]), 0.0)
        dt_c = _softplus(pre_a)
        csum = _scan_rows(jnp.where(ssd_lane, dt_c * a_neg, log_f), jnp.add, 0.0)
        last = csum[L - 1:L]
        u_c = pre_a - csum
        m_intra = csum + _scan_rows(u_c, jnp.maximum, -jnp.inf)
        w_end = last - csum + pre_a
        m_loc = jnp.max(w_end, axis=0, keepdims=True)
        e_end = jnp.exp(w_end - m_loc)
        m_prev = mst[0:1]
        inter_log = csum + m_prev
        m_t = jnp.maximum(inter_log, m_intra)
        w_inter = jnp.exp(inter_log - m_t)
        a_col = csum - m_t
        en = jnp.exp(-m_t)
        m_new = jnp.maximum(last + m_prev, m_loc)
        a_prev = jnp.exp(last + m_prev - m_new)
        a_loc = jnp.exp(m_loc - m_new)
        rows_t = jnp.where(ssd_lane, csum, u_c).T

        def bcast(arr, j):
            return jnp.broadcast_to(arr[:, j:j + 1], (arr.shape[0], 128))

        def pair(arr, j0, k):
            return jnp.where(lo[:arr.shape[0]], bcast(arr, j0 + 2 * k), bcast(arr, j0 + 2 * k + 1))

        x_in = xpad[8:8 + L, :]
        tail = xpad[0:8, :]
        sub8 = lax.broadcasted_iota(jnp.int32, (8, SSD_CONV_CH), 0)
        conv = convb_ref[...] + convw_ref[SSD_CONV - 1:SSD_CONV, :] * x_in
        for sh in range(1, SSD_CONV):
            r = pltpu.roll(x_in, sh, 0)
            top = jnp.where(sub8 < sh, pltpu.roll(tail, sh, 0), r[0:8])
            shifted = jnp.concatenate([top, r[8:]], axis=0)
            conv = conv + convw_ref[SSD_CONV - 1 - sh:SSD_CONV - sh, :] * shifted
        xc = _silu(conv)
        xs = xc[:, :SSD_WIDTH]
        bmat = xc[:, SSD_WIDTH:SSD_WIDTH + 256]
        cmat = xc[:, SSD_WIDTH + 256:]
        dfull = dfull_ref[...]
        cs_b = [bcast(csum, LANE_SSD + h) for h in range(SSD_HEADS)]
        xdt, xds, ecs, cdec = [], [], [], []
        for k in range(4):
            cs_p = jnp.where(lo, cs_b[2 * k], cs_b[2 * k + 1])
            last_p = pair(last, LANE_SSD, k)
            xdt.append(xs[:, 128 * k:128 * (k + 1)] * pair(dt_c, LANE_SSD, k))
            xds.append((xdt[k] * jnp.exp(last_p - cs_p)).astype(BF16))
            ecs.append(jnp.exp(cs_p))
            cdec.append(jnp.exp(last_p))

        y_slabs = []
        for g_ in range(SSD_GROUPS):
            bm_g = bmat[:, 128 * g_:128 * (g_ + 1)]
            cm_b = cmat[:, 128 * g_:128 * (g_ + 1)].astype(BF16)
            cb = _dot_nt(cm_b, bm_g.astype(BF16))
            s_loc = _dot_tn(bm_g.astype(BF16), jnp.concatenate(xds[2 * g_:2 * g_ + 2], axis=1))
            prev = prev_t[g_]
            y_off = _dot(cm_b, prev.astype(BF16))
            for kk in range(2):
                k = 2 * g_ + kk
                slab = xdt[k].astype(BF16)
                m2 = []
                for hl in range(2):
                    h = 2 * k + hl
                    diff = cs_b[h] - rows_t[LANE_SSD + h:LANE_SSD + h + 1, :]
                    dec = jnp.exp(jnp.where(tril, diff, -jnp.inf))
                    m2.append((cb * dec).astype(BF16))
                yd = _dot(jnp.concatenate(m2, axis=0), slab)
                y_diag = jnp.where(lo, yd[:L], yd[L:])
                y_slabs.append(y_diag + y_off[:, 128 * kk:128 * (kk + 1)] * ecs[k]
                               + xs[:, 128 * k:128 * (k + 1)] * dfull[:, 128 * k:128 * (k + 1)])
            prev_t[g_] = prev * jnp.concatenate(cdec[2 * g_:2 * g_ + 2], axis=1) + s_loc
        y = jnp.concatenate(y_slabs, axis=1)
        yz = y * stf[:, ST_Z:ST_MO]
        var = jnp.mean(yz * yz, axis=-1, keepdims=True)
        y_ref[0, y_rows, 0:SSD_WIDTH] = ((yz * lax.rsqrt(var + NORM_EPS)) * ssdnw_ref[...]).astype(BF16)

        bd_mask = row_lo == (col_i < 64)
        inv_sqrt_d = MLSTM_HEAD_DIM ** -0.5
        ones_b = jnp.ones((L, 128), BF16)
        mq = stb[:, ST_MQ:ST_MK]
        mk = stb[:, ST_MK:ST_MV]
        mv = stb[:, ST_MV:]
        mo = stf[:, ST_MO:ST_SMALL_A]
        mnw = mnw_ref[...]
        m_slabs = []
        for k in range(2):
            sl = slice(128 * k, 128 * (k + 1))
            q_b = mq[:, sl]
            k_b = mk[:, sl]
            v_b = mv[:, sl]
            v1 = jnp.concatenate([v_b, ones_b], axis=1)
            qm2 = jnp.concatenate([jnp.where(lo, q_b, jnp.zeros((), BF16)),
                                   jnp.where(lo, jnp.zeros((), BF16), q_b)], axis=0)
            qk2 = _dot_nt(qm2, k_b)
            a2 = []
            for hl in range(2):
                h = 2 * k + hl
                arg = bcast(a_col, LANE_ML + h) + rows_t[LANE_ML + h:LANE_ML + h + 1, :]
                w_intra = jnp.exp(jnp.where(tril, arg, -jnp.inf))
                a2.append(((qk2[hl * L:(hl + 1) * L] * inv_sqrt_d) * w_intra).astype(BF16))
            intra = _dot(jnp.concatenate(a2, axis=0), v1)
            num_intra = jnp.where(lo, intra[:L, :128], intra[L:, :128])
            den_intra = jnp.where(lo, intra[:L, 128:], intra[L:, 128:])
            ct_prev = ct[k]
            nm_prev = nm[k]
            wi = pair(w_inter, LANE_ML, k)
            inter = _dot(q_b, jnp.concatenate([ct_prev, nm_prev], axis=1).astype(BF16))
            num = num_intra + wi * inter[:, :128]
            den = den_intra + wi * inter[:, 128:]
            denom = jnp.maximum(jnp.abs(den), pair(en, LANE_ML, k))
            hs = num / denom
            sq = hs * hs
            s_lo = jnp.sum(jnp.where(lo, sq, 0.0), axis=1, keepdims=True)
            s_hi = jnp.sum(jnp.where(lo, 0.0, sq), axis=1, keepdims=True)
            hvar = jnp.where(lo, s_lo, s_hi) * (1.0 / MLSTM_HEAD_DIM)
            hn = (hs * lax.rsqrt(hvar + NORM_EPS)) * mnw[:, sl]
            m_slabs.append(mo[:, sl] * hn)
            e_p = pair(e_end, LANE_ML, k)
            loc = _dot_tn(k_b, jnp.concatenate([(v_b.astype(F32) * e_p).astype(BF16), e_p.astype(BF16)], axis=1)
                          ) * inv_sqrt_d
            ap = pair(a_prev, LANE_ML, k)
            al = pair(a_loc, LANE_ML, k)
            ct[k] = ct_prev * ap + jnp.where(bd_mask, loc[:, :128], 0.0) * al
            nm[k] = nm_prev * ap + jnp.where(bd_mask, loc[:, 128:], 0.0) * al
        mst[...] = jnp.broadcast_to(m_new, mst.shape)
        y_ref[0, y_rows, SSD_WIDTH + DIFF_WIDTH:] = jnp.concatenate(m_slabs, axis=1).astype(BF16)

        q = stb[:, ST_Q:ST_MQ]
        zero_b = jnp.zeros((), BF16)
        for h in range(DIFF_HEADS):
            qs = q[:, 128 * (h // 2):128 * (h // 2 + 1)]
            base = 64 * (h % 2)
            q0 = jnp.where((lane128 >= base) & (lane128 < base + 32), qs, zero_b)
            q1 = jnp.where((lane128 >= base + 32) & (lane128 < base + 64), qs, zero_b)
            qbd[h] = jnp.concatenate([q0, q1], axis=0)

    lam = lam_ref[layer]

    def attention(c, y_rows, xpad_other, stf_other, stb_other, x_next, slot_next, c_next):
        n_groups = c // KEY_GROUP + 1
        gw = KEY_GROUP * L

        def fill_scores(kg, dst):
            r0 = pl.multiple_of(kg * gw, gw)
            for k in range(DIFF_HEADS // 2):
                keys = kb[slot, pl.ds(r0, gw), 128 * k:128 * (k + 1)]
                sc = _dot_nt(qbd[2 * k:2 * k + 2].reshape(4 * L, 128), keys)
                for hl in range(2):
                    h = 2 * k + hl
                    add = jnp.concatenate(
                        [bias[jnp.clip(c - KEY_GROUP * kg - j, -1, 2) + 1, h] for j in range(KEY_GROUP)], axis=1)
                    dst[h] = sc[hl * 2 * L:(hl + 1) * 2 * L] + jnp.concatenate([add, add], axis=0)

        def consume(kg, src):
            r0 = pl.multiple_of(kg * gw, gw)
            for h in range(DIFF_HEADS):
                s = src[h]
                t = s[:, :L]
                for j in range(1, KEY_GROUP):
                    t = jnp.maximum(t, s[:, L * j:L * (j + 1)])
                m_old = rstat[h]
                m_new = jnp.maximum(m_old, jnp.max(t, axis=1, keepdims=True))
                p = jnp.concatenate([jnp.exp2(s[:, L * j:L * (j + 1)] - m_new) for j in range(KEY_GROUP)], axis=1)
                acc_s[h] = acc_s[h] * jnp.exp2(m_old - m_new) + _dot(p.astype(BF16), vb[slot, h, pl.ds(r0, gw), :])
                rstat[h] = m_new

        rstat[...] = jnp.full(rstat.shape, MASKED, F32)
        acc_s[...] = jnp.zeros(acc_s.shape, F32)
        fill_scores(0, s_even)

        def key_step(kg, carry):
            @pl.when(kg % 2 == 0)
            def _():
                fill_scores(kg + 1, s_odd)
                consume(kg, s_even)

            @pl.when(kg % 2 == 1)
            def _():
                fill_scores(kg + 1, s_even)
                consume(kg, s_odd)

            return carry

        g_last = n_groups - 1
        lax.fori_loop(0, g_last, key_step, 0)

        @pl.when(g_last % 2 == 0)
        def _():
            consume(g_last, s_even)

        @pl.when(g_last % 2 == 1)
        def _():
            consume(g_last, s_odd)

        d_slabs = []
        for k in range(2):
            a0, a1 = acc_s[2 * k], acc_s[2 * k + 1]
            lo2 = jnp.concatenate([lo, lo], axis=0)
            r = jnp.where(lo2, a0, a1) * (1.0 / pltpu.roll(jnp.where(lo2, a1, a0), 64, 1))
            o = r[:L] - lam * r[L:]
            sq = o * o
            s_lo = jnp.sum(jnp.where(lo, sq, 0.0), axis=1, keepdims=True)
            s_hi = jnp.sum(jnp.where(lo, 0.0, sq), axis=1, keepdims=True)
            ovar = jnp.where(lo, s_lo, s_hi) * (1.0 / DIFF_V_DIM)
            d_slabs.append(((o * lax.rsqrt(ovar + NORM_EPS)) * dnw_ref[:, 128 * k:128 * (k + 1)])
                           * (1.0 - lam_init))
        y_ref[0, y_rows, SSD_WIDTH:SSD_WIDTH + DIFF_WIDTH] = jnp.concatenate(d_slabs, axis=1).astype(BF16)

        u_next = normed(x_next)
        project_a(u_next, xpad_other, stf_other)
        project_b(u_next, stf_other, stb_other, slot_next, pl.multiple_of(c_next * L, L))

    nb = n_blocks
    sets = ((xpad_a, stf_a, stb_a), (xpad_b, stf_b, stb_b))
    for j in range(CHUNKS_PER_STEP):
        c = CHUNKS_PER_STEP * cp + j
        rows = slice(j * L, (j + 1) * L)
        if j + 1 < CHUNKS_PER_STEP:
            nxt = (hc_ref[0, (j + 1) * L:(j + 2) * L, :], slot, c + 1)
        else:
            nxt = (hn_ref[0], lax.rem(b + (c + 1) // nb, 2), lax.rem(c + 1, nb))
        chunk_block(c, rows, *sets[j % 2], sets[(j + 1) % 2][0])
        attention(c, rows, *sets[(j + 1) % 2], *nxt)


def _mixer(h3d, lam, rel_bias, params, *, layer, lam_init):
    bsz, s, _ = h3d.shape
    L = CHUNK
    nb = s // L

    def par(a):
        return pl.BlockSpec((None,) + a.shape[1:], lambda b, c: (layer, 0, 0))

    smem = pl.BlockSpec(memory_space=pltpu.SMEM)
    cps = CHUNKS_PER_STEP
    cur = pl.BlockSpec((1, cps * L, D_MODEL), lambda b, c: (b, c, 0))
    nxt = pl.BlockSpec((1, L, D_MODEL),
                       lambda b, c: (jnp.minimum(b + (cps * c + cps) // nb, bsz - 1), lax.rem(cps * c + cps, nb), 0))
    w_spec = pl.BlockSpec((None, D_MODEL, PROJ_COLS), lambda b, c: (layer, 0, 0), pipeline_mode=pl.Buffered(1))
    in_specs = [smem, smem, cur, nxt, par(params[0]), w_spec] + [par(p) for p in params[2:]]
    stage = [
        pltpu.VMEM((L + 8, SSD_CONV_CH), F32),
        pltpu.VMEM((L, ST_F_COLS), F32),
        pltpu.VMEM((L, ST_B_COLS), BF16),
    ]
    scratch = [
        pltpu.VMEM((SSD_GROUPS, SSD_STATE, 256), F32),
        pltpu.VMEM((2, 128, 128), F32),
        pltpu.VMEM((2, 128, 128), F32),
        pltpu.VMEM((8, 128), F32),
        pltpu.VMEM((2, s, 256), BF16),
        pltpu.VMEM((2, DIFF_HEADS, s, 128), BF16),
        pltpu.VMEM((4, DIFF_HEADS, L, L), F32),
        pltpu.VMEM((DIFF_HEADS, 2 * L, 128), BF16),
        pltpu.VMEM((DIFF_HEADS, 2 * L, L), F32),
        pltpu.VMEM((DIFF_HEADS, 2 * L, 128), F32),
        pltpu.VMEM((DIFF_HEADS, 2 * L, KEY_GROUP * L), F32),
        pltpu.VMEM((DIFF_HEADS, 2 * L, KEY_GROUP * L), F32),
    ] + stage + stage
    return pl.pallas_call(
        functools.partial(_mixer_kernel, layer=layer, lam_init=lam_init, n_blocks=nb),
        grid=(bsz, nb // cps),
        in_specs=in_specs,
        out_specs=pl.BlockSpec((1, cps * L, MIX_WIDTH), lambda b, c: (b, c, 0)),
        out_shape=jax.ShapeDtypeStruct((bsz, s, MIX_WIDTH), BF16),
        scratch_shapes=scratch,
        compiler_params=pltpu.CompilerParams(dimension_semantics=("arbitrary", "arbitrary"),
                                             vmem_limit_bytes=V7X_VMEM_LIMIT),
        name="mixer",
    )(lam, rel_bias.astype(F32), h3d, h3d, *params)


def _dense_kernel(h_ref, y_ref, p_ref, wo_ref, n2_ref, wg_ref, wu_ref, wd_ref, pg_ref, pp_ref, fn_ref,
                  o_ref, *, final, h_chunk):
    h1 = h_ref[...] + _dot(y_ref[...], wo_ref[...])
    var = jnp.mean(h1 * h1, axis=-1, keepdims=True)
    u = ((h1 * lax.rsqrt(var + NORM_EPS)) * n2_ref[...]).astype(BF16)
    ffn = jnp.zeros(h1.shape, F32)
    for c0 in range(0, FFN_HIDDEN, h_chunk):
        g = _dot(u, wg_ref[:, c0:c0 + h_chunk])
        up = _dot(u, wu_ref[:, c0:c0 + h_chunk])
        a = (_silu(g) * up).astype(BF16)
        ffn = ffn + _dot(a, wd_ref[c0:c0 + h_chunk, :])
    h2 = h1 + ffn
    gate = _sigmoid(_dot(h2.astype(BF16), pg_ref[...]))
    h3 = h2 + gate * _dot(p_ref[...].astype(BF16), pp_ref[...])
    if final:
        fvar = jnp.mean(h3 * h3, axis=-1, keepdims=True)
        h3 = (h3 * lax.rsqrt(fvar + NORM_EPS)) * fn_ref[...]
    o_ref[...] = h3


def _dense(h2d, y2d, p_all, weights, final_norm_w, *, layer, final, tm):
    t = h2d.shape[0]
    row = lambda width: pl.BlockSpec((tm, width), lambda i: (i, 0))
    res = lambda a: pl.BlockSpec((None,) + a.shape[1:], lambda i: (layer, 0, 0), pipeline_mode=pl.Buffered(1))
    return pl.pallas_call(
        functools.partial(_dense_kernel, final=final, h_chunk=FFN_CHUNK),
        grid=(t // tm,),
        in_specs=[row(D_MODEL), row(MIX_WIDTH), pl.BlockSpec((None, tm, PLE_DIM), lambda i: (layer, i, 0))]
        + [res(w) for w in weights] + [pl.BlockSpec((1, D_MODEL), lambda i: (0, 0))],
        out_specs=row(D_MODEL),
        out_shape=jax.ShapeDtypeStruct((t, D_MODEL), F32),
        compiler_params=pltpu.CompilerParams(dimension_semantics=("arbitrary",),
                                             vmem_limit_bytes=V7X_VMEM_LIMIT),
        name="dense",
    )(h2d, y2d, p_all, *weights, final_norm_w.reshape(1, D_MODEL).astype(F32))


def _pack_w_in(w):
    z, xbc, dt, dq, dk, dv, mq, mk, mv, mo, mi, mf = jnp.split(w, np.cumsum(IN_SPLITS)[:-1], axis=-1)
    zeros = lambda n: jnp.zeros(w.shape[:-1] + (n,), w.dtype)
    small_a = jnp.concatenate([dt, mi, zeros(128 - LANE_ML - MLSTM_HEADS)], axis=-1)
    small_b = jnp.concatenate([zeros(LANE_ML), mf, zeros(128 - LANE_ML - MLSTM_HEADS)], axis=-1)
    return jnp.concatenate([xbc, z, mo, small_a, small_b, dq, dk, dv, mq, mk, mv], axis=-1).astype(BF16)


def _lanes(depth, *placed):
    row = jnp.zeros((depth, 128), F32)
    for off, v in placed:
        row = row.at[:, off:off + v.shape[1]].set(v.astype(F32))
    return row[:, None, :]


def kernel(x, p, norm1_w, w_in, ssd_conv_w, ssd_conv_b, ssd_dt_bias, ssd_a_log, ssd_d, ssd_norm_w, diff_lq1, diff_lk1, diff_lq2, diff_lk2, diff_norm_w, rel_bias, mlstm_i_bias, mlstm_f_bias, mlstm_norm_w, w_out, norm2_w, w_ffn_gate, w_ffn_up, w_ffn_down, ple_gate_w, ple_proj_w, final_norm_w):
    bsz, s, d = x.shape
    depth = w_in.shape[0]
    assert d == D_MODEL and s % (KEY_GROUP * CHUNK) == 0 and s % (CHUNKS_PER_STEP * CHUNK) == 0
    t = bsz * s
    tm = ROW_TILE if s % ROW_TILE == 0 else CHUNK
    row3 = lambda a: a.astype(F32)[:, None, :]

    lam_inits = [0.8 - 0.6 * math.exp(-0.3 * i) for i in range(depth)]
    lam = (jnp.exp(jnp.sum(diff_lq1.astype(F32) * diff_lk1.astype(F32), axis=-1))
           - jnp.exp(jnp.sum(diff_lq2.astype(F32) * diff_lk2.astype(F32), axis=-1))
           + jnp.asarray(lam_inits, F32))
    mixer_params = [
        row3(norm1_w), _pack_w_in(w_in), ssd_conv_w.astype(F32), row3(ssd_conv_b),
        _lanes(depth, (LANE_SSD, ssd_dt_bias), (LANE_ML, mlstm_i_bias)), _lanes(depth, (LANE_ML, mlstm_f_bias)),
        _lanes(depth, (LANE_SSD, ssd_a_log)),
        row3(jnp.repeat(ssd_d, SSD_HEAD_DIM, axis=1)), row3(ssd_norm_w),
        row3(jnp.tile(diff_norm_w, (1, DIFF_HEADS))), row3(mlstm_norm_w),
    ]
    dense_weights = [w_out.astype(BF16), row3(norm2_w), w_ffn_gate.astype(BF16), w_ffn_up.astype(BF16),
                     w_ffn_down.astype(BF16), ple_gate_w.astype(BF16), ple_proj_w.astype(BF16)]
    p_all = p.reshape(depth, t, PLE_DIM).astype(F32)

    h = x.reshape(t, d).astype(F32)
    for i in range(depth):
        y = _mixer(h.reshape(bsz, s, d), lam, rel_bias, mixer_params, layer=i, lam_init=lam_inits[i])
        h = _dense(h, y.reshape(t, MIX_WIDTH), p_all, dense_weights, final_norm_w, layer=i,
                   final=(i == depth - 1), tm=tm)
    return h.reshape(bsz, s, d).astype(x.dtype)
```
